```python
import jax, jax.numpy as jnp
from jax import lax
import numpy as np

D_MODEL = 2048
BATCH = 32
SEQ = 256
DEPTH = 1
DEC_BATCH = 8
DEC_SEQ = 4096
PAST_LEN = 256

GRID_W = 64
MIX_W = D_MODEL
MLA_W = D_MODEL // 2
RET_W = MIX_W - MLA_W
MLA_NOPE = 128
ROPE_DIM = 64
QK_HEAD = MLA_NOPE + ROPE_DIM
MLA_V = 128
MLA_HEADS = MLA_W // MLA_V
Q_LORA = 384
KV_LORA = 256
RET_DV = 256
RET_DK = 128
RET_HEADS = RET_W // RET_DV
RET_QK = RET_HEADS * RET_DK
RET_CHUNK = 128
Q_BLOCK = 128
ROPE_BASE = 10000.0
EPS = 1e-6
IN_SIZES = (Q_LORA, KV_LORA, ROPE_DIM, MLA_W, RET_QK, RET_QK, RET_W, RET_W)
IN_COLS = Q_LORA + KV_LORA + ROPE_DIM + MLA_W + 2 * RET_QK + 2 * RET_W

kernel_name = 'hybrid_mla_retention_diffusion_step'


def rms_norm(x, w):
    xf = x.astype(jnp.float32)
    y = xf * lax.rsqrt(jnp.mean(xf * xf, axis=-1, keepdims=True) + EPS)
    return (y * w.astype(jnp.float32)).astype(x.dtype)


def rope_angles(pos, dim):
    half = dim // 2
    freqs = ROPE_BASE ** (-jnp.arange(half, dtype=jnp.float32) / half)
    return pos.astype(jnp.float32)[:, None] * freqs[None, :]


def apply_rotary(x, ang):
    cos = jnp.cos(ang)[None, :, None, :]
    sin = jnp.sin(ang)[None, :, None, :]
    xf = x.astype(jnp.float32)
    x1, x2 = jnp.split(xf, 2, axis=-1)
    return jnp.concatenate([x1 * cos - x2 * sin, x2 * cos + x1 * sin], axis=-1).astype(x.dtype)


def axial_rope_on_head(x, ang_row, ang_col):
    nope, rope = x[..., :MLA_NOPE], x[..., MLA_NOPE:]
    half = ROPE_DIM // 2
    rope = jnp.concatenate([apply_rotary(rope[..., :half], ang_row),
                            apply_rotary(rope[..., half:], ang_col)], axis=-1)
    return jnp.concatenate([nope, rope], axis=-1)


def adaln(x, mod, norm_w):
    shift, scale, gate = jnp.split(mod, 3, axis=-1)
    h = rms_norm(x, norm_w) * (1.0 + scale[:, None, :]) + shift[:, None, :]
    return h, gate[:, None, :]


def input_branches(h, w_in, q_norm_w, w_uq, qk_q_w, kv_norm_w):
    B, L, _ = h.shape
    offs = [int(v) for v in np.cumsum(IN_SIZES)[:-1]]
    q_lat, ckv_raw, k_rope, g_a, rq, rk, rv, g_b = jnp.split(h @ w_in, offs, axis=-1)
    q = (rms_norm(q_lat, q_norm_w) @ w_uq).reshape(B, L, MLA_HEADS, QK_HEAD)
    q = rms_norm(q, qk_q_w)
    ckv = rms_norm(ckv_raw, kv_norm_w)
    rq = rq.reshape(B, L, RET_HEADS, RET_DK)
    rk = rk.reshape(B, L, RET_HEADS, RET_DK) * (RET_DK ** -0.5)
    rv = rv.reshape(B, L, RET_HEADS, RET_DV)
    return q, ckv, k_rope, g_a, rq, rk, rv, g_b


def mla_keys_values(ckv, k_rope, w_uk, w_uv, qk_k_w):
    B, L, _ = ckv.shape
    k_nope = (ckv @ w_uk).reshape(B, L, MLA_HEADS, MLA_NOPE)
    k_r = jnp.broadcast_to(k_rope[:, :, None, :], (B, L, MLA_HEADS, ROPE_DIM))
    k = rms_norm(jnp.concatenate([k_nope, k_r], axis=-1), qk_k_w)
    v = (ckv @ w_uv).reshape(B, L, MLA_HEADS, MLA_V)
    return k, v


def block_attention(q, k, v):
    B, Lq, H, Dh = q.shape
    nblk = Lq // Q_BLOCK
    qb = q.reshape(B, nblk, Q_BLOCK, H, Dh).transpose(1, 0, 2, 3, 4)
    scale = Dh ** -0.5

    def one_block(qi):
        s = jnp.einsum('bqhd,bkhd->bhqk', qi, k).astype(jnp.float32) * scale
        p = jax.nn.softmax(s, axis=-1)
        return jnp.einsum('bhqk,bkhe->bqhe', p.astype(v.dtype), v)

    out = lax.map(one_block, qb)
    return out.transpose(1, 0, 2, 3, 4).reshape(B, Lq, H, v.shape[-1])


def retention_scan(q, k, v, log_gamma, s0, strict):
    B, L, H, DK = q.shape
    DV = v.shape[-1]
    C = RET_CHUNK
    n = L // C
    lg = log_gamma.astype(jnp.float32)
    idx = jnp.arange(C, dtype=jnp.float32)
    diff = idx[:, None] - idx[None, :]
    mask = (diff > 0) if strict else (diff >= 0)
    intra = jnp.where(mask[None], jnp.exp(jnp.where(mask, diff, 0.0)[None] * lg[:, None, None]), 0.0)
    q_dec = jnp.exp((idx[:, None] + 1.0) * lg[None, :])
    k_dec = jnp.exp((C - 1.0 - idx)[:, None] * lg[None, :])
    chunk_dec = jnp.exp(C * lg)

    def chunks(a):
        return a.reshape(B, n, C, H, a.shape[-1]).transpose(1, 0, 2, 3, 4).astype(jnp.float32)

    def step(S, xs):
        qc, kc, vc = xs
        s = jnp.einsum('bihd,bjhd->bhij', qc, kc) * intra[None]
        o = (jnp.einsum('bhij,bjhe->bihe', s, vc)
             + jnp.einsum('bihd,bhde->bihe', qc * q_dec[None, :, :, None], S))
        S = (S * chunk_dec[None, :, None, None]
             + jnp.einsum('bjhd,bjhe->bhde', kc * k_dec[None, :, :, None], vc))
        return S, o

    S, o = lax.scan(step, s0.astype(jnp.float32), (chunks(q), chunks(k), chunks(v)))
    o = o.transpose(1, 0, 2, 3, 4).reshape(B, L, H, DV).astype(v.dtype)
    return o, S


def bidir_retention(q, k, v, lg_f, lg_b, s_f, s_b):
    o_f, S_f = retention_scan(q, k, v, lg_f, s_f, False)
    o_b, S_b = retention_scan(jnp.flip(q, 1), jnp.flip(k, 1), jnp.flip(v, 1), lg_b, s_b, True)
    return o_f + jnp.flip(o_b, 1), S_f, S_b


def merge_output(attn, ret, g_a, g_b, gn_w, w_out):
    B, L = attn.shape[:2]
    ret = rms_norm(ret, gn_w)
    mix = jnp.concatenate([jax.nn.silu(g_a) * attn.reshape(B, L, MLA_W),
                           jax.nn.silu(g_b) * ret.reshape(B, L, RET_W)], axis=-1)
    return mix @ w_out


def setup_inputs(seed: int = 0) -> dict:
    key = jax.random.key(seed)
    ks = jax.random.split(key, 24)
    nrm = jax.random.normal
    f32 = jnp.float32
    a0 = jnp.asarray(np.log(-np.log1p(-2.0 ** (-5.0 - np.arange(RET_HEADS)))), dtype=f32)
    return {
        'x_prompt': nrm(ks[0], (BATCH, SEQ, D_MODEL), f32),
        'x_sample': nrm(ks[1], (DEC_BATCH, DEC_SEQ, D_MODEL), f32),
        'c': nrm(ks[2], (DEC_BATCH, D_MODEL), f32),
        'cache_mla_ckv': nrm(ks[3], (DEC_BATCH, DEPTH, PAST_LEN, KV_LORA), f32),
        'cache_mla_krope': nrm(ks[4], (DEC_BATCH, DEPTH, PAST_LEN, ROPE_DIM), f32),
        'state_ret_fwd': 0.1 * nrm(ks[5], (DEC_BATCH, DEPTH, RET_HEADS, RET_DK, RET_DV), f32),
        'state_ret_bwd': 0.1 * nrm(ks[6], (DEC_BATCH, DEPTH, RET_HEADS, RET_DK, RET_DV), f32),
        'c_ctx': nrm(ks[7], (D_MODEL,), f32),
        'norm_w': 1.0 + 0.01 * nrm(ks[8], (DEPTH, D_MODEL), f32),
        'w_mod': 0.5 * D_MODEL ** -0.5 * nrm(ks[9], (DEPTH, D_MODEL, 3 * D_MODEL), f32),
        'b_mod': 0.01 * nrm(ks[10], (DEPTH, 3 * D_MODEL), f32),
        'w_in': D_MODEL ** -0.5 * nrm(ks[11], (DEPTH, D_MODEL, IN_COLS), f32),
        'mla_q_norm_w': 1.0 + 0.01 * nrm(ks[12], (DEPTH, Q_LORA), f32),
        'mla_w_uq': Q_LORA ** -0.5 * nrm(ks[13], (DEPTH, Q_LORA, MLA_HEADS * QK_HEAD), f32),
        'mla_kv_norm_w': 1.0 + 0.01 * nrm(ks[14], (DEPTH, KV_LORA), f32),
        'mla_w_uk': KV_LORA ** -0.5 * nrm(ks[15], (DEPTH, KV_LORA, MLA_HEADS * MLA_NOPE), f32),
        'mla_w_uv': KV_LORA ** -0.5 * nrm(ks[16], (DEPTH, KV_LORA, MLA_W), f32),
        'mla_qk_q_w': 1.0 + 0.01 * nrm(ks[17], (DEPTH, QK_HEAD), f32),
        'mla_qk_k_w': 1.0 + 0.01 * nrm(ks[18], (DEPTH, QK_HEAD), f32),
        'ret_log_decay_fwd': a0[None, :] + 0.01 * nrm(ks[19], (DEPTH, RET_HEADS), f32),
        'ret_log_decay_bwd': a0[None, :] + 0.01 * nrm(ks[20], (DEPTH, RET_HEADS), f32),
        'ret_gn_w': 1.0 + 0.01 * nrm(ks[21], (DEPTH, RET_HEADS, RET_DV), f32),
        'w_out': MIX_W ** -0.5 * nrm(ks[22], (DEPTH, MIX_W, D_MODEL), f32),
    }


def reference(x_prompt, x_sample, c, cache_mla_ckv, cache_mla_krope, state_ret_fwd, state_ret_bwd,
              c_ctx, norm_w, w_mod, b_mod, w_in, mla_q_norm_w, mla_w_uq, mla_kv_norm_w, mla_w_uk,
              mla_w_uv, mla_qk_q_w, mla_qk_k_w, ret_log_decay_fwd, ret_log_decay_bwd, ret_gn_w, w_out):
    L_lat = x_sample.shape[1]
    ROWS = L_lat // GRID_W
    row = jnp.repeat(jnp.arange(ROWS), GRID_W)
    col = jnp.tile(jnp.arange(GRID_W), ROWS)
    ang_row = rope_angles(row, ROPE_DIM // 2)
    ang_col = rope_angles(col, ROPE_DIM // 2)
    ang_ret = rope_angles(jnp.arange(L_lat), RET_DK)

    xp = x_prompt
    Bp = x_prompt.shape[0]
    ckv_list, krope_list, sf_list, sb_list = [], [], [], []
    for l in range(DEPTH):
        lg_f = -jnp.exp(ret_log_decay_fwd[l].astype(jnp.float32))
        lg_b = -jnp.exp(ret_log_decay_bwd[l].astype(jnp.float32))
        mod = jax.nn.silu(c_ctx)[None, :] @ w_mod[l] + b_mod[l]
        h, gate = adaln(xp, mod, norm_w[l])
        q, ckv, k_rope, g_a, rq, rk, rv, g_b = input_branches(
            h, w_in[l], mla_q_norm_w[l], mla_w_uq[l], mla_qk_q_w[l], mla_kv_norm_w[l])
        k, v = mla_keys_values(ckv, k_rope, mla_w_uk[l], mla_w_uv[l], mla_qk_k_w[l])
        attn = block_attention(q, k, v)
        zero_state = jnp.zeros((Bp, RET_HEADS, RET_DK, RET_DV), jnp.float32)
        ret, s_f, s_b = bidir_retention(rq, rk, rv, lg_f, lg_b, zero_state, zero_state)
        xp = xp + gate * merge_output(attn, ret, g_a, g_b, ret_gn_w[l], w_out[l])
        ckv_list.append(ckv)
        krope_list.append(k_rope)
        sf_list.append(s_f)
        sb_list.append(s_b)

    xs = x_sample
    for l in range(DEPTH):
        lg_f = -jnp.exp(ret_log_decay_fwd[l].astype(jnp.float32))
        lg_b = -jnp.exp(ret_log_decay_bwd[l].astype(jnp.float32))
        mod = jax.nn.silu(c) @ w_mod[l] + b_mod[l]
        h, gate = adaln(xs, mod, norm_w[l])
        q, ckv, k_rope, g_a, rq, rk, rv, g_b = input_branches(
            h, w_in[l], mla_q_norm_w[l], mla_w_uq[l], mla_qk_q_w[l], mla_kv_norm_w[l])
        q = axial_rope_on_head(q, ang_row, ang_col)
        k_lat, v_lat = mla_keys_values(ckv, k_rope, mla_w_uk[l], mla_w_uv[l], mla_qk_k_w[l])
        k_lat = axial_rope_on_head(k_lat, ang_row, ang_col)
        k_ctx, v_ctx = mla_keys_values(cache_mla_ckv[:, l], cache_mla_krope[:, l],
                                       mla_w_uk[l], mla_w_uv[l], mla_qk_k_w[l])
        attn = block_attention(q, jnp.concatenate([k_lat, k_ctx], axis=1),
                               jnp.concatenate([v_lat, v_ctx], axis=1))
        rq = apply_rotary(rq, ang_ret)
        rk = apply_rotary(rk, ang_ret)
        ret, _, _ = bidir_retention(rq, rk, rv, lg_f, lg_b, state_ret_fwd[:, l], state_ret_bwd[:, l])
        xs = xs + gate * merge_output(attn, ret, g_a, g_b, ret_gn_w[l], w_out[l])

    new_ckv = jnp.stack(ckv_list, axis=1)
    new_krope = jnp.stack(krope_list, axis=1)
    new_sf = jnp.stack(sf_list, axis=1)
    new_sb = jnp.stack(sb_list, axis=1)
    return (xp, xs, new_ckv, new_krope, new_sf, new_sb)
```

```python
import functools
import math

import jax
import jax.numpy as jnp
from jax import lax
from jax.experimental import pallas as pl
from jax.experimental.pallas import tpu as pltpu

F32 = jnp.float32
BF16 = jnp.bfloat16

D_MODEL = 2048
GRID_W = 64
MLA_W = 1024
RET_W = 1024
MLA_NOPE = 128
ROPE_DIM = 64
QK_HEAD = MLA_NOPE + ROPE_DIM
MLA_V = 128
MLA_HEADS = 8
Q_LORA = 384
KV_LORA = 256
RET_DV = 256
RET_DK = 128
RET_HEADS = 4
RET_QK = RET_HEADS * RET_DK
RET_CHUNK = 128
ROPE_BASE = 10000.0
EPS = 1e-6

LANE = 128
QK_PAD = 2 * LANE
MOD_ROWS = 16
VMEM_LIMIT = 56 * 1024 * 1024

OFF_QLAT = 0
OFF_CKV = OFF_QLAT + Q_LORA
OFF_KROPE = OFF_CKV + KV_LORA
OFF_GA = OFF_KROPE + LANE
OFF_RQ = OFF_GA + MLA_W
OFF_RK = OFF_RQ + RET_QK
OFF_RV = OFF_RK + RET_QK
OFF_GB = OFF_RV + RET_W
IN_COLS_PAD = OFF_GB + RET_W

NT_DIMS = (((1,), (1,)), ((), ()))


def _silu(x):
    return x * (1.0 / (1.0 + jnp.exp(-x)))


def _rms(x, w, n):
    ms = jnp.sum(x * x, axis=-1, keepdims=True) * (1.0 / n)
    return x * lax.rsqrt(ms + EPS) * w


def _dot(a, b):
    return jnp.dot(a, b, preferred_element_type=F32)


def _mod_kernel(c_ref, w_ref, b_ref, o_ref):
    s = _silu(c_ref[...]).astype(BF16)
    o_ref[...] = _dot(s, w_ref[...].astype(BF16)) + b_ref[...]


def _mod_call(cvec, w_mod, b_mod):
    tn = 512
    n = w_mod.shape[1]
    return pl.pallas_call(
        _mod_kernel,
        grid=(n // tn,),
        in_specs=[pl.BlockSpec((MOD_ROWS, D_MODEL), lambda j: (0, 0)),
                  pl.BlockSpec((D_MODEL, tn), lambda j: (0, j)),
                  pl.BlockSpec((1, tn), lambda j: (0, j))],
        out_specs=pl.BlockSpec((MOD_ROWS, tn), lambda j: (0, j)),
        out_shape=jax.ShapeDtypeStruct((MOD_ROWS, n), F32),
        compiler_params=pltpu.CompilerParams(dimension_semantics=("arbitrary",),
                                             vmem_limit_bytes=VMEM_LIMIT),
        name="mod",
    )(cvec, w_mod, b_mod)


def _mla_rope(r, c, sa, sb):
    return r * c + pltpu.roll(r, LANE - 16, 1) * sa + pltpu.roll(r, 16, 1) * sb


def _keys_values(ckv, kr, w_uk_ref, w_uvt_ref, kkn, kkr, rope, k_ref, vt_ref):
    ckv_b = ckv.astype(BF16)
    kn_all = _dot(ckv_b, w_uk_ref[...])
    krw = kr * kkr
    if rope is not None:
        krw = _mla_rope(krw, *rope)
    kr_ss = jnp.sum(kr * kr, axis=-1, keepdims=True)
    for h in range(MLA_HEADS):
        kn = kn_all[:, h * MLA_NOPE:(h + 1) * MLA_NOPE]
        ss = jnp.sum(kn * kn, axis=-1, keepdims=True) + kr_ss
        inv = lax.rsqrt(ss * (1.0 / QK_HEAD) + EPS)
        k_ref[0, h, :, :LANE] = (kn * inv * kkn).astype(BF16)
        k_ref[0, h, :, LANE:] = (krw * inv).astype(BF16)
    vt = lax.dot_general(w_uvt_ref[...], ckv_b, NT_DIMS, preferred_element_type=F32)
    vt_ref[0] = vt.astype(BF16)


def _in_kernel(*refs, rope, emit_cache):
    it = iter(refs)
    x_ref, mod_ref, nw_ref, w_ref = next(it), next(it), next(it), next(it)
    qnw_ref, wuq_ref, qkq_ref = next(it), next(it), next(it)
    kvw_ref, wuk_ref, wuvt_ref, kkn_ref, kkr_ref = next(it), next(it), next(it), next(it), next(it)
    if rope:
        mc_ref, msa_ref, msb_ref, rc_ref, rs_ref = next(it), next(it), next(it), next(it), next(it)
    q_ref, k_ref, vt_ref, ga_ref = next(it), next(it), next(it), next(it)
    rq_ref, rk_ref, rv_ref, gb_ref = next(it), next(it), next(it), next(it)
    if emit_cache:
        ckv_ref, kr_ref = next(it), next(it)

    x = x_ref[0]
    shift = mod_ref[0, 0:1, :]
    scale = mod_ref[0, 1:2, :]
    h = (_rms(x, nw_ref[...], D_MODEL) * (1.0 + scale) + shift).astype(BF16)

    mla_rope = (mc_ref[...], msa_ref[...], msb_ref[...]) if rope else None

    a = _dot(h, w_ref[:, OFF_QLAT:OFF_GA])
    qn = _rms(a[:, OFF_QLAT:OFF_CKV], qnw_ref[...], Q_LORA).astype(BF16)
    ckv = _rms(a[:, OFF_CKV:OFF_KROPE], kvw_ref[...], KV_LORA)
    kr = a[:, OFF_KROPE:OFF_GA]
    if emit_cache:
        ckv_ref[0] = ckv
        kr_ref[0] = kr[:, :ROPE_DIM]

    q_all = _dot(qn, wuq_ref[...])
    qkq = qkq_ref[...]
    for hh in range(MLA_HEADS):
        qh = q_all[:, hh * QK_PAD:(hh + 1) * QK_PAD]
        inv = lax.rsqrt(jnp.sum(qh * qh, axis=-1, keepdims=True) * (1.0 / QK_HEAD) + EPS)
        qh = qh * inv * qkq
        r = qh[:, LANE:]
        if rope:
            r = _mla_rope(r, *mla_rope)
        q_ref[0, hh, :, :LANE] = qh[:, :LANE].astype(BF16)
        q_ref[0, hh, :, LANE:] = r.astype(BF16)

    _keys_values(ckv, kr, wuk_ref, wuvt_ref, kkn_ref[...], kkr_ref[...], mla_rope, k_ref, vt_ref)

    ga_ref[0] = _silu(_dot(h, w_ref[:, OFF_GA:OFF_RQ])).astype(BF16)
    gb_ref[0] = _silu(_dot(h, w_ref[:, OFF_GB:IN_COLS_PAD])).astype(BF16)

    rqk = _dot(h, w_ref[:, OFF_RQ:OFF_RV])
    for hh in range(2 * RET_HEADS):
        t = rqk[:, hh * RET_DK:(hh + 1) * RET_DK]
        if hh >= RET_HEADS:
            t = t * (RET_DK ** -0.5)
        if rope:
            t = t * rc_ref[...] + pltpu.roll(t, RET_DK // 2, 1) * rs_ref[...]
        if hh < RET_HEADS:
            rq_ref[0, :, hh * RET_DK:(hh + 1) * RET_DK] = t.astype(BF16)
        else:
            g = hh - RET_HEADS
            rk_ref[0, :, g * RET_DK:(g + 1) * RET_DK] = t.astype(BF16)
    rv_ref[0] = _dot(h, w_ref[:, OFF_RV:OFF_GB]).astype(BF16)


def _const_spec(shape):
    return pl.BlockSpec(shape, lambda b, i: (0,) * len(shape), pipeline_mode=pl.Buffered(1))


def _in_call(x, mod3, ctx_mod, wts, rope_tabs, tm):
    B, L, _ = x.shape
    rope = rope_tabs is not None
    emit_cache = not rope
    mod_map = (lambda b, i: (MOD_ROWS - 1, 0, 0)) if ctx_mod else (lambda b, i: (b, 0, 0))
    in_specs = [pl.BlockSpec((1, tm, D_MODEL), lambda b, i: (b, i, 0)),
                pl.BlockSpec((1, 3, D_MODEL), mod_map),
                _const_spec((1, D_MODEL)),
                _const_spec((D_MODEL, IN_COLS_PAD)),
                _const_spec((1, Q_LORA)),
                _const_spec((Q_LORA, MLA_HEADS * QK_PAD)),
                _const_spec((1, QK_PAD)),
                _const_spec((1, KV_LORA)),
                _const_spec((KV_LORA, MLA_HEADS * MLA_NOPE)),
                _const_spec((MLA_W, KV_LORA)),
                _const_spec((1, LANE)),
                _const_spec((1, LANE))]
    args = [x, mod3, wts["norm_w"], wts["w_in"], wts["q_norm_w"], wts["w_uq"], wts["qk_q_w"],
            wts["kv_norm_w"], wts["w_uk"], wts["w_uvt"], wts["qk_k_n"], wts["qk_k_r"]]
    if rope:
        in_specs += [pl.BlockSpec((tm, LANE), lambda b, i: (i, 0))] * 5
        args += list(rope_tabs)
    tok = lambda w: pl.BlockSpec((1, tm, w), lambda b, i: (b, i, 0))
    out_specs = [pl.BlockSpec((1, MLA_HEADS, tm, QK_PAD), lambda b, i: (b, 0, i, 0)),
                 pl.BlockSpec((1, MLA_HEADS, tm, QK_PAD), lambda b, i: (b, 0, i, 0)),
                 pl.BlockSpec((1, MLA_W, tm), lambda b, i: (b, 0, i)),
                 tok(MLA_W), tok(RET_QK), tok(RET_QK), tok(RET_W), tok(RET_W)]
    out_shape = [jax.ShapeDtypeStruct((B, MLA_HEADS, L, QK_PAD), BF16),
                 jax.ShapeDtypeStruct((B, MLA_HEADS, L, QK_PAD), BF16),
                 jax.ShapeDtypeStruct((B, MLA_W, L), BF16),
                 jax.ShapeDtypeStruct((B, L, MLA_W), BF16),
                 jax.ShapeDtypeStruct((B, L, RET_QK), BF16),
                 jax.ShapeDtypeStruct((B, L, RET_QK), BF16),
                 jax.ShapeDtypeStruct((B, L, RET_W), BF16),
                 jax.ShapeDtypeStruct((B, L, RET_W), BF16)]
    if emit_cache:
        out_specs += [tok(KV_LORA), tok(ROPE_DIM)]
        out_shape += [jax.ShapeDtypeStruct((B, L, KV_LORA), F32),
                      jax.ShapeDtypeStruct((B, L, ROPE_DIM), F32)]
    return pl.pallas_call(
        functools.partial(_in_kernel, rope=rope, emit_cache=emit_cache),
        grid=(B, L // tm),
        in_specs=in_specs, out_specs=out_specs, out_shape=out_shape,
        compiler_params=pltpu.CompilerParams(dimension_semantics=("parallel", "parallel"),
                                             vmem_limit_bytes=VMEM_LIMIT),
        name="in_proj_rope" if rope else "in_proj",
    )(*args)


def _ctx_kernel(ckv_ref, kr_ref, wuk_ref, wuvt_ref, kkn_ref, kkr_ref, k_ref, vt_ref):
    _keys_values(ckv_ref[0], kr_ref[0], wuk_ref, wuvt_ref, kkn_ref[...], kkr_ref[...], None,
                 k_ref, vt_ref)


def _ctx_call(ckv, kr_pad, wts):
    B, P, _ = ckv.shape
    cs = lambda shape: pl.BlockSpec(shape, lambda b: (0,) * len(shape))
    return pl.pallas_call(
        _ctx_kernel,
        grid=(B,),
        in_specs=[pl.BlockSpec((1, P, KV_LORA), lambda b: (b, 0, 0)),
                  pl.BlockSpec((1, P, LANE), lambda b: (b, 0, 0)),
                  cs((KV_LORA, MLA_HEADS * MLA_NOPE)), cs((MLA_W, KV_LORA)),
                  cs((1, LANE)), cs((1, LANE))],
        out_specs=[pl.BlockSpec((1, MLA_HEADS, P, QK_PAD), lambda b: (b, 0, 0, 0)),
                   pl.BlockSpec((1, MLA_W, P), lambda b: (b, 0, 0))],
        out_shape=[jax.ShapeDtypeStruct((B, MLA_HEADS, P, QK_PAD), BF16),
                   jax.ShapeDtypeStruct((B, MLA_W, P), BF16)],
        compiler_params=pltpu.CompilerParams(dimension_semantics=("parallel",),
                                             vmem_limit_bytes=VMEM_LIMIT),
        name="ctx_kv",
    )(ckv, kr_pad, wts["w_uk"], wts["w_uvt"], wts["qk_k_n"], wts["qk_k_r"])


def _attn_kernel(*refs, tk, n_chunks, has_ctx):
    if has_ctx:
        q_ref, k_ref, vt_ref, kc_ref, vtc_ref, ga_ref, o_ref = refs
    else:
        q_ref, k_ref, vt_ref, ga_ref, o_ref = refs
    q = q_ref[0, 0]
    tq = q.shape[0]

    def step(kc, vtc, carry):
        m, l, acc = carry
        st = lax.dot_general(kc, q, NT_DIMS, preferred_element_type=F32)
        m_new = jnp.maximum(m, jnp.max(st, axis=0, keepdims=True))
        alpha = jnp.exp2(m - m_new)
        p = jnp.exp2(st - m_new)
        l = alpha * l + jnp.sum(p, axis=0, keepdims=True)
        acc = alpha * acc + _dot(vtc, p.astype(BF16))
        return m_new, l, acc

    def body(j, carry):
        s0 = pl.multiple_of(j * tk, tk)
        return step(k_ref[0, 0, pl.ds(s0, tk), :], vt_ref[0, :, pl.ds(s0, tk)], carry)

    carry = (jnp.full((1, tq), -1e30, F32), jnp.zeros((1, tq), F32), jnp.zeros((MLA_V, tq), F32))
    carry = lax.fori_loop(0, n_chunks, body, carry)
    if has_ctx:
        carry = step(kc_ref[0, 0], vtc_ref[0], carry)
    _, l, acc = carry
    out = (acc * (1.0 / l)).T
    o_ref[0] = (out * ga_ref[0].astype(F32)).astype(BF16)


def _attn_call(q, k, vt, ga, ctx, tq, tk):
    B, H, L, _ = q.shape
    has_ctx = ctx is not None
    in_specs = [pl.BlockSpec((1, 1, tq, QK_PAD), lambda b, h, i: (b, h, i, 0)),
                pl.BlockSpec((1, 1, L, QK_PAD), lambda b, h, i: (b, h, 0, 0)),
                pl.BlockSpec((1, MLA_V, L), lambda b, h, i: (b, h, 0))]
    args = [q, k, vt]
    if has_ctx:
        kc, vtc = ctx
        P = kc.shape[2]
        in_specs += [pl.BlockSpec((1, 1, P, QK_PAD), lambda b, h, i: (b, h, 0, 0)),
                     pl.BlockSpec((1, MLA_V, P), lambda b, h, i: (b, h, 0))]
        args += [kc, vtc]
    in_specs.append(pl.BlockSpec((1, tq, MLA_V), lambda b, h, i: (b, i, h)))
    args.append(ga)
    return pl.pallas_call(
        functools.partial(_attn_kernel, tk=tk, n_chunks=L // tk, has_ctx=has_ctx),
        grid=(B, H, L // tq),
        in_specs=in_specs,
        out_specs=pl.BlockSpec((1, tq, MLA_V), lambda b, h, i: (b, i, h)),
        out_shape=jax.ShapeDtypeStruct((B, L, MLA_W), BF16),
        compiler_params=pltpu.CompilerParams(
            dimension_semantics=("parallel", "parallel", "parallel"),
            vmem_limit_bytes=VMEM_LIMIT),
        name="attn_ctx" if has_ctx else "attn",
    )(*args)


def _ret_kernel(*refs, n_chunks, has_state):
    it = iter(refs)
    lgf_ref, lgb_ref = next(it), next(it)
    q_ref, k_ref, v_ref, gb_ref, gnw_ref = next(it), next(it), next(it), next(it), next(it)
    if has_state:
        s0f_ref, s0b_ref = next(it), next(it)
    out_ref = next(it)
    if not has_state:
        sf_ref, sb_ref = next(it), next(it)
    o_scr, s_scr = next(it), next(it)

    C = RET_CHUNK
    hd = pl.program_id(1)
    lgf = lgf_ref[hd]
    lgb = lgb_ref[hd]
    ri = lax.broadcasted_iota(jnp.int32, (C, C), 0).astype(F32)
    ci = lax.broadcasted_iota(jnp.int32, (C, C), 1).astype(F32)
    diff = ri - ci
    low = diff >= 0
    mask = (jnp.where(low, jnp.exp(jnp.where(low, diff, 0.0) * lgf), 0.0)
            + jnp.where(low, 0.0, jnp.exp(jnp.where(low, 0.0, -diff) * lgb)))
    qd_f = jnp.exp((ri + 1.0) * lgf)
    kd_f = jnp.exp((C - 1.0 - ri) * lgf)
    qd_b = jnp.exp((C - ri) * lgb)
    kd_b = jnp.exp(ri * lgb)
    cd_f = jnp.exp(jnp.full((RET_DK, RET_DV), C, F32) * lgf)
    cd_b = jnp.exp(jnp.full((RET_DK, RET_DV), C, F32) * lgb)
    gnw = gnw_ref[0]

    def chunk(n):
        s0 = pl.multiple_of(n * C, C)
        sl = pl.ds(s0, C)
        return sl, q_ref[0, sl, :], k_ref[0, sl, :], v_ref[0, sl, :]

    def state_update(kc, vc, kd, cd):
        kt = (kc.astype(F32) * kd).T.astype(BF16)
        s_scr[...] = s_scr[...] * cd + _dot(kt, vc)

    if has_state:
        s_scr[...] = s0f_ref[0, 0]
    else:
        s_scr[...] = jnp.zeros((RET_DK, RET_DV), F32)

    def fwd(n, _):
        sl, qc, kc, vc = chunk(n)
        s = lax.dot_general(qc, kc, NT_DIMS, preferred_element_type=F32) * mask
        qs = (qc.astype(F32) * qd_f).astype(BF16)
        o_scr[sl, :] = _dot(s.astype(BF16), vc) + _dot(qs, s_scr[...].astype(BF16))
        state_update(kc, vc, kd_f, cd_f)
        return 0

    lax.fori_loop(0, n_chunks, fwd, 0)
    if not has_state:
        sf_ref[0, 0] = s_scr[...]

    if has_state:
        s_scr[...] = s0b_ref[0, 0]
    else:
        s_scr[...] = jnp.zeros((RET_DK, RET_DV), F32)

    def bwd(t, _):
        sl, qc, kc, vc = chunk(n_chunks - 1 - t)
        qs = (qc.astype(F32) * qd_b).astype(BF16)
        o = o_scr[sl, :] + _dot(qs, s_scr[...].astype(BF16))
        y = _rms(o, gnw, RET_DV) * gb_ref[0, sl, :].astype(F32)
        out_ref[0, sl, :] = y.astype(BF16)
        state_update(kc, vc, kd_b, cd_b)
        return 0

    lax.fori_loop(0, n_chunks, bwd, 0)
    if not has_state:
        sb_ref[0, 0] = s_scr[...]


def _ret_call(lgf, lgb, rq, rk, rv, gb, gnw, states):
    B, L, _ = rq.shape
    has_state = states is not None
    smem = pl.BlockSpec(memory_space=pltpu.SMEM)
    st_spec = pl.BlockSpec((1, 1, RET_DK, RET_DV), lambda b, h: (b, h, 0, 0))
    in_specs = [smem, smem,
                pl.BlockSpec((1, L, RET_DK), lambda b, h: (b, 0, h)),
                pl.BlockSpec((1, L, RET_DK), lambda b, h: (b, 0, h)),
                pl.BlockSpec((1, L, RET_DV), lambda b, h: (b, 0, h)),
                pl.BlockSpec((1, L, RET_DV), lambda b, h: (b, 0, h)),
                pl.BlockSpec((1, 1, RET_DV), lambda b, h: (h, 0, 0))]
    args = [lgf, lgb, rq, rk, rv, gb, gnw]
    out_specs = [pl.BlockSpec((1, L, RET_DV), lambda b, h: (b, 0, h))]
    out_shape = [jax.ShapeDtypeStruct((B, L, RET_W), BF16)]
    if has_state:
        in_specs += [st_spec, st_spec]
        args += list(states)
    else:
        out_specs += [st_spec, st_spec]
        out_shape += [jax.ShapeDtypeStruct((B, RET_HEADS, RET_DK, RET_DV), F32)] * 2
    return pl.pallas_call(
        functools.partial(_ret_kernel, n_chunks=L // RET_CHUNK, has_state=has_state),
        grid=(B, RET_HEADS),
        in_specs=in_specs, out_specs=out_specs, out_shape=out_shape,
        scratch_shapes=[pltpu.VMEM((L, RET_DV), F32), pltpu.VMEM((RET_DK, RET_DV), F32)],
        compiler_params=pltpu.CompilerParams(dimension_semantics=("parallel", "parallel"),
                                             vmem_limit_bytes=VMEM_LIMIT),
        name="ret_state" if has_state else "ret",
    )(*args)


def _out_kernel(ma_ref, mb_ref, w_ref, x_ref, mod_ref, o_ref):
    acc = _dot(ma_ref[0], w_ref[:MLA_W, :]) + _dot(mb_ref[0], w_ref[MLA_W:, :])
    o_ref[0] = x_ref[0] + mod_ref[0, 2:3, :] * acc


def _out_call(mix_a, mix_b, w_out, x, mod3, ctx_mod, tm):
    B, L, _ = x.shape
    mod_map = (lambda b, i: (MOD_ROWS - 1, 0, 0)) if ctx_mod else (lambda b, i: (b, 0, 0))
    return pl.pallas_call(
        _out_kernel,
        grid=(B, L // tm),
        in_specs=[pl.BlockSpec((1, tm, MLA_W), lambda b, i: (b, i, 0)),
                  pl.BlockSpec((1, tm, RET_W), lambda b, i: (b, i, 0)),
                  _const_spec((D_MODEL, D_MODEL)),
                  pl.BlockSpec((1, tm, D_MODEL), lambda b, i: (b, i, 0)),
                  pl.BlockSpec((1, 3, D_MODEL), mod_map)],
        out_specs=pl.BlockSpec((1, tm, D_MODEL), lambda b, i: (b, i, 0)),
        out_shape=jax.ShapeDtypeStruct((B, L, D_MODEL), F32),
        compiler_params=pltpu.CompilerParams(dimension_semantics=("parallel", "parallel"),
                                             vmem_limit_bytes=VMEM_LIMIT),
        name="out_proj",
    )(mix_a, mix_b, w_out, x, mod3)


def _rope_angles(pos, dim):
    half = dim // 2
    freqs = ROPE_BASE ** (-jnp.arange(half, dtype=F32) / half)
    return pos.astype(F32)[:, None] * freqs[None, :]


def _rope_tables(L):
    rows = L // GRID_W
    row = jnp.repeat(jnp.arange(rows), GRID_W)
    col = jnp.tile(jnp.arange(GRID_W), rows)
    ar = _rope_angles(row, ROPE_DIM // 2)
    ac = _rope_angles(col, ROPE_DIM // 2)
    cr, sr, cc, sc = jnp.cos(ar), jnp.sin(ar), jnp.cos(ac), jnp.sin(ac)
    z16 = jnp.zeros_like(cr)
    z64 = jnp.zeros((L, LANE - ROPE_DIM), F32)
    mc = jnp.concatenate([cr, cr, cc, cc, z64], axis=-1)
    msa = jnp.concatenate([-sr, z16, -sc, z16, z64], axis=-1)
    msb = jnp.concatenate([z16, sr, z16, sc, z64], axis=-1)
    at = _rope_angles(jnp.arange(L), RET_DK)
    rc = jnp.concatenate([jnp.cos(at), jnp.cos(at)], axis=-1)
    rs = jnp.concatenate([-jnp.sin(at), jnp.sin(at)], axis=-1)
    return mc, msa, msb, rc, rs


def _prep_weights(l, norm_w, w_in, q_norm_w, w_uq, kv_norm_w, w_uk, w_uv, qk_q_w, qk_k_w):
    wi = w_in[l]
    w_in_p = jnp.concatenate(
        [wi[:, :OFF_KROPE + ROPE_DIM], jnp.zeros((D_MODEL, LANE - ROPE_DIM), wi.dtype),
         wi[:, OFF_KROPE + ROPE_DIM:]], axis=1).astype(BF16)
    wq = w_uq[l].reshape(Q_LORA, MLA_HEADS, QK_HEAD)
    wq = jnp.pad(wq, ((0, 0), (0, 0), (0, QK_PAD - QK_HEAD))).reshape(Q_LORA, MLA_HEADS * QK_PAD)
    qscale = (QK_HEAD ** -0.5) * math.log2(math.e)
    qkq = jnp.pad(qk_q_w[l] * qscale, (0, QK_PAD - QK_HEAD))[None, :]
    kk = qk_k_w[l]
    return {
        "norm_w": norm_w[l][None, :],
        "w_in": w_in_p,
        "q_norm_w": q_norm_w[l][None, :],
        "w_uq": wq.astype(BF16),
        "qk_q_w": qkq,
        "kv_norm_w": kv_norm_w[l][None, :],
        "w_uk": w_uk[l].astype(BF16),
        "w_uvt": w_uv[l].T.astype(BF16),
        "qk_k_n": kk[None, :MLA_NOPE],
        "qk_k_r": jnp.pad(kk[MLA_NOPE:], (0, LANE - ROPE_DIM))[None, :],
    }


def _layer(x, mod3, ctx_mod, wts, w_out, lgf, lgb, gnw, rope_tabs, ctx, states, tm, tq, tk):
    outs = _in_call(x, mod3, ctx_mod, wts, rope_tabs, tm)
    q, k, vt, ga, rq, rk, rv, gb = outs[:8]
    mix_a = _attn_call(q, k, vt, ga, ctx, tq, tk)
    ret = _ret_call(lgf, lgb, rq, rk, rv, gb, gnw, states)
    y = _out_call(mix_a, ret[0], w_out, x, mod3, ctx_mod, tm)
    return y, outs[8:], ret[1:]


def kernel(x_prompt, x_sample, c, cache_mla_ckv, cache_mla_krope, state_ret_fwd, state_ret_bwd,
           c_ctx, norm_w, w_mod, b_mod, w_in, mla_q_norm_w, mla_w_uq, mla_kv_norm_w, mla_w_uk,
           mla_w_uv, mla_qk_q_w, mla_qk_k_w, ret_log_decay_fwd, ret_log_decay_bwd, ret_gn_w, w_out):
    depth = w_in.shape[0]
    dec_b = x_sample.shape[0]
    assert dec_b < MOD_ROWS
    cvec = jnp.zeros((MOD_ROWS, D_MODEL), F32).at[:dec_b].set(c).at[MOD_ROWS - 1].set(c_ctx)
    rope_tabs = _rope_tables(x_sample.shape[1])

    xp, xs = x_prompt, x_sample
    ckv_l, kr_l, sf_l, sb_l = [], [], [], []
    for l in range(depth):
        wts = _prep_weights(l, norm_w, w_in, mla_q_norm_w, mla_w_uq, mla_kv_norm_w, mla_w_uk,
                            mla_w_uv, mla_qk_q_w, mla_qk_k_w)
        w_out_b = w_out[l].astype(BF16)
        lgf = -jnp.exp(ret_log_decay_fwd[l].astype(F32))
        lgb = -jnp.exp(ret_log_decay_bwd[l].astype(F32))
        gnw = ret_gn_w[l][:, None, :]
        mod3 = _mod_call(cvec, w_mod[l], b_mod[l][None, :]).reshape(MOD_ROWS, 3, D_MODEL)

        xp, (ckv, kr), (sf, sb) = _layer(xp, mod3, True, wts, w_out_b, lgf, lgb, gnw,
                                         None, None, None, tm=256, tq=256, tk=256)
        ckv_l.append(ckv)
        kr_l.append(kr)
        sf_l.append(sf)
        sb_l.append(sb)

        kr_pad = jnp.pad(cache_mla_krope[:, l], ((0, 0), (0, 0), (0, LANE - ROPE_DIM)))
        ctx = _ctx_call(cache_mla_ckv[:, l], kr_pad, wts)
        xs, _, _ = _layer(xs, mod3, False, wts, w_out_b, lgf, lgb, gnw, rope_tabs, ctx,
                          (state_ret_fwd[:, l], state_ret_bwd[:, l]), tm=256, tq=512, tk=512)

    return (xp, xs, jnp.stack(ckv_l, axis=1), jnp.stack(kr_l, axis=1),
            jnp.stack(sf_l, axis=1), jnp.stack(sb_l, axis=1))
```

```python
import functools
import math

import jax
import jax.numpy as jnp
from jax import lax
from jax.experimental import pallas as pl
from jax.experimental.pallas import tpu as pltpu

F32 = jnp.float32
BF16 = jnp.bfloat16

D_MODEL = 2048
GRID_W = 64
MLA_W = 1024
RET_W = 1024
MLA_NOPE = 128
ROPE_DIM = 64
QK_HEAD = MLA_NOPE + ROPE_DIM
MLA_V = 128
MLA_HEADS = 8
Q_LORA = 384
KV_LORA = 256
RET_DV = 256
RET_DK = 128
RET_HEADS = 4
RET_QK = RET_HEADS * RET_DK
RET_CHUNK = 128
ROPE_BASE = 10000.0
EPS = 1e-6

LANE = 128
QK_PAD = 2 * LANE
V_AUG = MLA_V + 16
OVERFLOW_GUARD = 2.0 ** 100
MOD_ROWS = 16
VMEM_LIMIT = 56 * 1024 * 1024

OFF_QLAT = 0
OFF_CKV = OFF_QLAT + Q_LORA
OFF_KROPE = OFF_CKV + KV_LORA
OFF_GA = OFF_KROPE + LANE
OFF_RQ = OFF_GA + MLA_W
OFF_RK = OFF_RQ + RET_QK
OFF_RV = OFF_RK + RET_QK
OFF_GB = OFF_RV + RET_W
IN_COLS_PAD = OFF_GB + RET_W

NT_DIMS = (((1,), (1,)), ((), ()))


def _silu(x):
    return x * (1.0 / (1.0 + jnp.exp(-x)))


def _rms(x, w, n):
    ms = jnp.sum(x * x, axis=-1, keepdims=True) * (1.0 / n)
    return x * lax.rsqrt(ms + EPS) * w


def _dot(a, b):
    return jnp.dot(a, b, preferred_element_type=F32)


def _mod_kernel(c_ref, w_ref, b_ref, o_ref):
    s = _silu(c_ref[...]).astype(BF16)
    o_ref[...] = _dot(s, w_ref[...].astype(BF16)) + b_ref[...]


def _mod_call(cvec, w_mod, b_mod):
    tn = 512
    n = w_mod.shape[1]
    return pl.pallas_call(
        _mod_kernel,
        grid=(n // tn,),
        in_specs=[pl.BlockSpec((MOD_ROWS, D_MODEL), lambda j: (0, 0)),
                  pl.BlockSpec((D_MODEL, tn), lambda j: (0, j)),
                  pl.BlockSpec((1, tn), lambda j: (0, j))],
        out_specs=pl.BlockSpec((MOD_ROWS, tn), lambda j: (0, j)),
        out_shape=jax.ShapeDtypeStruct((MOD_ROWS, n), F32),
        compiler_params=pltpu.CompilerParams(dimension_semantics=("arbitrary",),
                                             vmem_limit_bytes=VMEM_LIMIT),
        name="mod",
    )(cvec, w_mod, b_mod)


def _mla_rope(r, c, sa, sb):
    return r * c + pltpu.roll(r, LANE - 16, 1) * sa + pltpu.roll(r, 16, 1) * sb


def _keys_values(ckv, kr, w_uk_ref, w_uvt_ref, kkn, kkr, rope, k_ref, vt_ref):
    ckv_b = ckv.astype(BF16)
    kn_all = _dot(ckv_b, w_uk_ref[...])
    krw = kr * kkr
    if rope is not None:
        krw = _mla_rope(krw, *rope)
    kr_ss = jnp.sum(kr * kr, axis=-1, keepdims=True)
    for h in range(MLA_HEADS):
        kn = kn_all[:, h * MLA_NOPE:(h + 1) * MLA_NOPE]
        ss = jnp.sum(kn * kn, axis=-1, keepdims=True) + kr_ss
        inv = lax.rsqrt(ss * (1.0 / QK_HEAD) + EPS)
        k_ref[0, h, :, :LANE] = (kn * inv * kkn).astype(BF16)
        k_ref[0, h, :, LANE:] = (krw * inv).astype(BF16)
    vt = lax.dot_general(w_uvt_ref[...], ckv_b, NT_DIMS, preferred_element_type=F32)
    ones_row = jnp.where(lax.broadcasted_iota(jnp.int32, (V_AUG - MLA_V, vt.shape[1]), 0) == 0,
                         1.0, 0.0).astype(BF16)
    for h in range(MLA_HEADS):
        vt_ref[0, h, :MLA_V, :] = vt[h * MLA_V:(h + 1) * MLA_V].astype(BF16)
        vt_ref[0, h, MLA_V:, :] = ones_row


def _in_kernel(*refs, rope, emit_cache):
    it = iter(refs)
    x_ref, mod_ref, nw_ref, w_ref = next(it), next(it), next(it), next(it)
    qnw_ref, wuq_ref, qkq_ref = next(it), next(it), next(it)
    kvw_ref, wuk_ref, wuvt_ref, kkn_ref, kkr_ref = next(it), next(it), next(it), next(it), next(it)
    if rope:
        mc_ref, msa_ref, msb_ref, rc_ref, rs_ref = next(it), next(it), next(it), next(it), next(it)
    q_ref, k_ref, vt_ref, ga_ref = next(it), next(it), next(it), next(it)
    rq_ref, rk_ref, rv_ref, gb_ref = next(it), next(it), next(it), next(it)
    if emit_cache:
        ckv_ref, kr_ref = next(it), next(it)

    x = x_ref[0]
    shift = mod_ref[0, 0:1, :]
    scale = mod_ref[0, 1:2, :]
    h = (_rms(x, nw_ref[...], D_MODEL) * (1.0 + scale) + shift).astype(BF16)

    mla_rope = (mc_ref[...], msa_ref[...], msb_ref[...]) if rope else None

    a = _dot(h, w_ref[:, OFF_QLAT:OFF_GA])
    qn = _rms(a[:, OFF_QLAT:OFF_CKV], qnw_ref[...], Q_LORA).astype(BF16)
    ckv = _rms(a[:, OFF_CKV:OFF_KROPE], kvw_ref[...], KV_LORA)
    kr = a[:, OFF_KROPE:OFF_GA]
    if emit_cache:
        ckv_ref[0] = ckv
        kr_ref[0] = kr[:, :ROPE_DIM]

    q_all = _dot(qn, wuq_ref[...])
    qkq = qkq_ref[...]
    for hh in range(MLA_HEADS):
        qh = q_all[:, hh * QK_PAD:(hh + 1) * QK_PAD]
        inv = lax.rsqrt(jnp.sum(qh * qh, axis=-1, keepdims=True) * (1.0 / QK_HEAD) + EPS)
        qh = qh * inv * qkq
        r = qh[:, LANE:]
        if rope:
            r = _mla_rope(r, *mla_rope)
        q_ref[0, hh, :, :LANE] = qh[:, :LANE].astype(BF16)
        q_ref[0, hh, :, LANE:] = r.astype(BF16)

    _keys_values(ckv, kr, wuk_ref, wuvt_ref, kkn_ref[...], kkr_ref[...], mla_rope, k_ref, vt_ref)

    ga_ref[0] = _silu(_dot(h, w_ref[:, OFF_GA:OFF_RQ])).astype(BF16)
    gb_ref[0] = _silu(_dot(h, w_ref[:, OFF_GB:IN_COLS_PAD])).astype(BF16)

    rqk = _dot(h, w_ref[:, OFF_RQ:OFF_RV])
    for hh in range(2 * RET_HEADS):
        t = rqk[:, hh * RET_DK:(hh + 1) * RET_DK]
        if hh >= RET_HEADS:
            t = t * (RET_DK ** -0.5)
        if rope:
            t = t * rc_ref[...] + pltpu.roll(t, RET_DK // 2, 1) * rs_ref[...]
        if hh < RET_HEADS:
            rq_ref[0, :, hh * RET_DK:(hh + 1) * RET_DK] = t.astype(BF16)
        else:
            g = hh - RET_HEADS
            rk_ref[0, :, g * RET_DK:(g + 1) * RET_DK] = t.astype(BF16)
    rv_ref[0] = _dot(h, w_ref[:, OFF_RV:OFF_GB]).astype(BF16)


def _const_spec(shape):
    return pl.BlockSpec(shape, lambda b, i: (0,) * len(shape), pipeline_mode=pl.Buffered(1))


def _in_call(x, mod3, ctx_mod, wts, rope_tabs, tm):
    B, L, _ = x.shape
    rope = rope_tabs is not None
    emit_cache = not rope
    mod_map = (lambda b, i: (MOD_ROWS - 1, 0, 0)) if ctx_mod else (lambda b, i: (b, 0, 0))
    in_specs = [pl.BlockSpec((1, tm, D_MODEL), lambda b, i: (b, i, 0)),
                pl.BlockSpec((1, 3, D_MODEL), mod_map),
                _const_spec((1, D_MODEL)),
                _const_spec((D_MODEL, IN_COLS_PAD)),
                _const_spec((1, Q_LORA)),
                _const_spec((Q_LORA, MLA_HEADS * QK_PAD)),
                _const_spec((1, QK_PAD)),
                _const_spec((1, KV_LORA)),
                _const_spec((KV_LORA, MLA_HEADS * MLA_NOPE)),
                _const_spec((MLA_W, KV_LORA)),
                _const_spec((1, LANE)),
                _const_spec((1, LANE))]
    args = [x, mod3, wts["norm_w"], wts["w_in"], wts["q_norm_w"], wts["w_uq"], wts["qk_q_w"],
            wts["kv_norm_w"], wts["w_uk"], wts["w_uvt"], wts["qk_k_n"], wts["qk_k_r"]]
    if rope:
        in_specs += [pl.BlockSpec((tm, LANE), lambda b, i: (i, 0))] * 5
        args += list(rope_tabs)
    tok = lambda w: pl.BlockSpec((1, tm, w), lambda b, i: (b, i, 0))
    out_specs = [pl.BlockSpec((1, MLA_HEADS, tm, QK_PAD), lambda b, i: (b, 0, i, 0)),
                 pl.BlockSpec((1, MLA_HEADS, tm, QK_PAD), lambda b, i: (b, 0, i, 0)),
                 pl.BlockSpec((1, MLA_HEADS, V_AUG, tm), lambda b, i: (b, 0, 0, i)),
                 tok(MLA_W), tok(RET_QK), tok(RET_QK), tok(RET_W), tok(RET_W)]
    out_shape = [jax.ShapeDtypeStruct((B, MLA_HEADS, L, QK_PAD), BF16),
                 jax.ShapeDtypeStruct((B, MLA_HEADS, L, QK_PAD), BF16),
                 jax.ShapeDtypeStruct((B, MLA_HEADS, V_AUG, L), BF16),
                 jax.ShapeDtypeStruct((B, L, MLA_W), BF16),
                 jax.ShapeDtypeStruct((B, L, RET_QK), BF16),
                 jax.ShapeDtypeStruct((B, L, RET_QK), BF16),
                 jax.ShapeDtypeStruct((B, L, RET_W), BF16),
                 jax.ShapeDtypeStruct((B, L, RET_W), BF16)]
    if emit_cache:
        out_specs += [tok(KV_LORA), tok(ROPE_DIM)]
        out_shape += [jax.ShapeDtypeStruct((B, L, KV_LORA), F32),
                      jax.ShapeDtypeStruct((B, L, ROPE_DIM), F32)]
    return pl.pallas_call(
        functools.partial(_in_kernel, rope=rope, emit_cache=emit_cache),
        grid=(B, L // tm),
        in_specs=in_specs, out_specs=out_specs, out_shape=out_shape,
        compiler_params=pltpu.CompilerParams(dimension_semantics=("parallel", "parallel"),
                                             vmem_limit_bytes=VMEM_LIMIT),
        name="in_proj_rope" if rope else "in_proj",
    )(*args)


def _ctx_kernel(ckv_ref, kr_ref, wuk_ref, wuvt_ref, kkn_ref, kkr_ref, k_ref, vt_ref):
    _keys_values(ckv_ref[0], kr_ref[0], wuk_ref, wuvt_ref, kkn_ref[...], kkr_ref[...], None,
                 k_ref, vt_ref)


def _ctx_call(ckv, kr_pad, wts):
    B, P, _ = ckv.shape
    cs = lambda shape: pl.BlockSpec(shape, lambda b: (0,) * len(shape))
    return pl.pallas_call(
        _ctx_kernel,
        grid=(B,),
        in_specs=[pl.BlockSpec((1, P, KV_LORA), lambda b: (b, 0, 0)),
                  pl.BlockSpec((1, P, LANE), lambda b: (b, 0, 0)),
                  cs((KV_LORA, MLA_HEADS * MLA_NOPE)), cs((MLA_W, KV_LORA)),
                  cs((1, LANE)), cs((1, LANE))],
        out_specs=[pl.BlockSpec((1, MLA_HEADS, P, QK_PAD), lambda b: (b, 0, 0, 0)),
                   pl.BlockSpec((1, MLA_HEADS, V_AUG, P), lambda b: (b, 0, 0, 0))],
        out_shape=[jax.ShapeDtypeStruct((B, MLA_HEADS, P, QK_PAD), BF16),
                   jax.ShapeDtypeStruct((B, MLA_HEADS, V_AUG, P), BF16)],
        compiler_params=pltpu.CompilerParams(dimension_semantics=("parallel",),
                                             vmem_limit_bytes=VMEM_LIMIT),
        name="ctx_kv",
    )(ckv, kr_pad, wts["w_uk"], wts["w_uvt"], wts["qk_k_n"], wts["qk_k_r"])


def _attn_kernel(*refs, tk, n_chunks, has_ctx):
    if has_ctx:
        q_ref, k_ref, vt_ref, kc_ref, vtc_ref, ga_ref, o_ref, s_scr = refs
    else:
        q_ref, k_ref, vt_ref, ga_ref, o_ref, s_scr = refs
    q = q_ref[0, 0]
    tq = q.shape[0]

    def scores(kc):
        return lax.dot_general(kc, q, NT_DIMS, preferred_element_type=F32)

    def chunk(j):
        s0 = pl.multiple_of(j * tk, tk)
        return k_ref[0, 0, pl.ds(s0, tk), :], vt_ref[0, 0, :, pl.ds(s0, tk)]

    def finish(acc):
        out = (acc[:MLA_V] * (1.0 / acc[MLA_V:MLA_V + 1])).T
        o_ref[0] = (out * ga_ref[0].astype(F32)).astype(BF16)

    s_scr[0] = scores(k_ref[0, 0, 0:tk, :])
    m_fix = jnp.max(s_scr[0], axis=0, keepdims=True)

    def weights(st):
        return jnp.exp2((st - m_fix).astype(BF16))

    acc = jnp.zeros((V_AUG, tq), F32)
    for j in range(n_chunks):
        if j + 1 < n_chunks:
            s_scr[(j + 1) % 2] = scores(k_ref[0, 0, (j + 1) * tk:(j + 2) * tk, :])
        elif has_ctx:
            st_ctx = scores(kc_ref[0, 0])
        acc = acc + _dot(vt_ref[0, 0, :, j * tk:(j + 1) * tk], weights(s_scr[j % 2]))
    if has_ctx:
        acc = acc + _dot(vtc_ref[0, 0], weights(st_ctx))
    finish(acc)

    @pl.when(jnp.logical_not(jnp.max(acc[MLA_V:MLA_V + 1]) < OVERFLOW_GUARD))
    def _():
        def step(kc, vtc, carry):
            m, acc = carry
            st = scores(kc)
            m_new = jnp.maximum(m, jnp.max(st, axis=0, keepdims=True))
            p = jnp.exp2(st - m_new).astype(BF16)
            return m_new, jnp.exp2(m - m_new) * acc + _dot(vtc, p)

        carry = (jnp.full((1, tq), -1e30, F32), jnp.zeros((V_AUG, tq), F32))
        carry = lax.fori_loop(0, n_chunks, lambda j, c: step(*chunk(j), c), carry)
        if has_ctx:
            carry = step(kc_ref[0, 0], vtc_ref[0, 0], carry)
        finish(carry[1])


def _attn_call(q, k, vt, ga, ctx, tq, tk):
    B, H, L, _ = q.shape
    has_ctx = ctx is not None
    in_specs = [pl.BlockSpec((1, 1, tq, QK_PAD), lambda b, h, i: (b, h, i, 0)),
                pl.BlockSpec((1, 1, L, QK_PAD), lambda b, h, i: (b, h, 0, 0)),
                pl.BlockSpec((1, 1, V_AUG, L), lambda b, h, i: (b, h, 0, 0))]
    args = [q, k, vt]
    if has_ctx:
        kc, vtc = ctx
        P = kc.shape[2]
        in_specs += [pl.BlockSpec((1, 1, P, QK_PAD), lambda b, h, i: (b, h, 0, 0)),
                     pl.BlockSpec((1, 1, V_AUG, P), lambda b, h, i: (b, h, 0, 0))]
        args += [kc, vtc]
    in_specs.append(pl.BlockSpec((1, tq, MLA_V), lambda b, h, i: (b, i, h)))
    args.append(ga)
    return pl.pallas_call(
        functools.partial(_attn_kernel, tk=tk, n_chunks=L // tk, has_ctx=has_ctx),
        grid=(B, H, L // tq),
        in_specs=in_specs,
        out_specs=pl.BlockSpec((1, tq, MLA_V), lambda b, h, i: (b, i, h)),
        out_shape=jax.ShapeDtypeStruct((B, L, MLA_W), BF16),
        scratch_shapes=[pltpu.VMEM((2, tk, tq), F32)],
        compiler_params=pltpu.CompilerParams(
            dimension_semantics=("parallel", "parallel", "parallel"),
            vmem_limit_bytes=VMEM_LIMIT),
        name="attn_ctx" if has_ctx else "attn",
    )(*args)


def _ret_kernel(*refs, n_chunks, has_state):
    it = iter(refs)
    lgf_ref, lgb_ref = next(it), next(it)
    q_ref, k_ref, v_ref, gb_ref, gnw_ref = next(it), next(it), next(it), next(it), next(it)
    if has_state:
        s0f_ref, s0b_ref = next(it), next(it)
    out_ref = next(it)
    if not has_state:
        sf_ref, sb_ref = next(it), next(it)
    o_scr, s_scr = next(it), next(it)

    C = RET_CHUNK
    hd = pl.program_id(1)
    lgf = lgf_ref[hd]
    lgb = lgb_ref[hd]
    ri = lax.broadcasted_iota(jnp.int32, (C, C), 0).astype(F32)
    ci = lax.broadcasted_iota(jnp.int32, (C, C), 1).astype(F32)
    diff = ri - ci
    low = diff >= 0
    mask = (jnp.where(low, jnp.exp(jnp.where(low, diff, 0.0) * lgf), 0.0)
            + jnp.where(low, 0.0, jnp.exp(jnp.where(low, 0.0, -diff) * lgb)))
    qd_f = jnp.exp((ri + 1.0) * lgf)
    kd_f = jnp.exp((C - 1.0 - ri) * lgf)
    qd_b = jnp.exp((C - ri) * lgb)
    kd_b = jnp.exp(ri * lgb)
    cd_f = jnp.exp(jnp.full((RET_DK, RET_DV), C, F32) * lgf)
    cd_b = jnp.exp(jnp.full((RET_DK, RET_DV), C, F32) * lgb)
    gnw = gnw_ref[0]

    def chunk(n):
        s0 = pl.multiple_of(n * C, C)
        sl = pl.ds(s0, C)
        return sl, q_ref[0, sl, :], k_ref[0, sl, :], v_ref[0, sl, :]

    def state_update(kc, vc, kd, cd):
        kt = (kc.astype(F32) * kd).T.astype(BF16)
        s_scr[...] = s_scr[...] * cd + _dot(kt, vc)

    if has_state:
        s_scr[...] = s0f_ref[0, 0]
    else:
        s_scr[...] = jnp.zeros((RET_DK, RET_DV), F32)

    def fwd(n, _):
        sl, qc, kc, vc = chunk(n)
        s = lax.dot_general(qc, kc, NT_DIMS, preferred_element_type=F32) * mask
        qs = (qc.astype(F32) * qd_f).astype(BF16)
        o_scr[sl, :] = _dot(s.astype(BF16), vc) + _dot(qs, s_scr[...].astype(BF16))
        state_update(kc, vc, kd_f, cd_f)
        return 0

    lax.fori_loop(0, n_chunks, fwd, 0)
    if not has_state:
        sf_ref[0, 0] = s_scr[...]

    if has_state:
        s_scr[...] = s0b_ref[0, 0]
    else:
        s_scr[...] = jnp.zeros((RET_DK, RET_DV), F32)

    def bwd(t, _):
        sl, qc, kc, vc = chunk(n_chunks - 1 - t)
        qs = (qc.astype(F32) * qd_b).astype(BF16)
        o = o_scr[sl, :] + _dot(qs, s_scr[...].astype(BF16))
        y = _rms(o, gnw, RET_DV) * gb_ref[0, sl, :].astype(F32)
        out_ref[0, sl, :] = y.astype(BF16)
        state_update(kc, vc, kd_b, cd_b)
        return 0

    lax.fori_loop(0, n_chunks, bwd, 0)
    if not has_state:
        sb_ref[0, 0] = s_scr[...]


def _ret_call(lgf, lgb, rq, rk, rv, gb, gnw, states):
    B, L, _ = rq.shape
    has_state = states is not None
    smem = pl.BlockSpec(memory_space=pltpu.SMEM)
    st_spec = pl.BlockSpec((1, 1, RET_DK, RET_DV), lambda b, h: (b, h, 0, 0))
    in_specs = [smem, smem,
                pl.BlockSpec((1, L, RET_DK), lambda b, h: (b, 0, h)),
                pl.BlockSpec((1, L, RET_DK), lambda b, h: (b, 0, h)),
                pl.BlockSpec((1, L, RET_DV), lambda b, h: (b, 0, h)),
                pl.BlockSpec((1, L, RET_DV), lambda b, h: (b, 0, h)),
                pl.BlockSpec((1, 1, RET_DV), lambda b, h: (h, 0, 0))]
    args = [lgf, lgb, rq, rk, rv, gb, gnw]
    out_specs = [pl.BlockSpec((1, L, RET_DV), lambda b, h: (b, 0, h))]
    out_shape = [jax.ShapeDtypeStruct((B, L, RET_W), BF16)]
    if has_state:
        in_specs += [st_spec, st_spec]
        args += list(states)
    else:
        out_specs += [st_spec, st_spec]
        out_shape += [jax.ShapeDtypeStruct((B, RET_HEADS, RET_DK, RET_DV), F32)] * 2
    return pl.pallas_call(
        functools.partial(_ret_kernel, n_chunks=L // RET_CHUNK, has_state=has_state),
        grid=(B, RET_HEADS),
        in_specs=in_specs, out_specs=out_specs, out_shape=out_shape,
        scratch_shapes=[pltpu.VMEM((L, RET_DV), F32), pltpu.VMEM((RET_DK, RET_DV), F32)],
        compiler_params=pltpu.CompilerParams(dimension_semantics=("parallel", "parallel"),
                                             vmem_limit_bytes=VMEM_LIMIT),
        name="ret_state" if has_state else "ret",
    )(*args)


def _out_kernel(ma_ref, mb_ref, w_ref, x_ref, mod_ref, o_ref):
    acc = _dot(ma_ref[0], w_ref[:MLA_W, :]) + _dot(mb_ref[0], w_ref[MLA_W:, :])
    o_ref[0] = x_ref[0] + mod_ref[0, 2:3, :] * acc


def _out_call(mix_a, mix_b, w_out, x, mod3, ctx_mod, tm):
    B, L, _ = x.shape
    mod_map = (lambda b, i: (MOD_ROWS - 1, 0, 0)) if ctx_mod else (lambda b, i: (b, 0, 0))
    return pl.pallas_call(
        _out_kernel,
        grid=(B, L // tm),
        in_specs=[pl.BlockSpec((1, tm, MLA_W), lambda b, i: (b, i, 0)),
                  pl.BlockSpec((1, tm, RET_W), lambda b, i: (b, i, 0)),
                  _const_spec((D_MODEL, D_MODEL)),
                  pl.BlockSpec((1, tm, D_MODEL), lambda b, i: (b, i, 0)),
                  pl.BlockSpec((1, 3, D_MODEL), mod_map)],
        out_specs=pl.BlockSpec((1, tm, D_MODEL), lambda b, i: (b, i, 0)),
        out_shape=jax.ShapeDtypeStruct((B, L, D_MODEL), F32),
        compiler_params=pltpu.CompilerParams(dimension_semantics=("parallel", "parallel"),
                                             vmem_limit_bytes=VMEM_LIMIT),
        name="out_proj",
    )(mix_a, mix_b, w_out, x, mod3)


def _rope_angles(pos, dim):
    half = dim // 2
    freqs = ROPE_BASE ** (-jnp.arange(half, dtype=F32) / half)
    return pos.astype(F32)[:, None] * freqs[None, :]


def _rope_tables(L):
    rows = L // GRID_W
    row = jnp.repeat(jnp.arange(rows), GRID_W)
    col = jnp.tile(jnp.arange(GRID_W), rows)
    ar = _rope_angles(row, ROPE_DIM // 2)
    ac = _rope_angles(col, ROPE_DIM // 2)
    cr, sr, cc, sc = jnp.cos(ar), jnp.sin(ar), jnp.cos(ac), jnp.sin(ac)
    z16 = jnp.zeros_like(cr)
    z64 = jnp.zeros((L, LANE - ROPE_DIM), F32)
    mc = jnp.concatenate([cr, cr, cc, cc, z64], axis=-1)
    msa = jnp.concatenate([-sr, z16, -sc, z16, z64], axis=-1)
    msb = jnp.concatenate([z16, sr, z16, sc, z64], axis=-1)
    at = _rope_angles(jnp.arange(L), RET_DK)
    rc = jnp.concatenate([jnp.cos(at), jnp.cos(at)], axis=-1)
    rs = jnp.concatenate([-jnp.sin(at), jnp.sin(at)], axis=-1)
    return mc, msa, msb, rc, rs


def _prep_weights(l, norm_w, w_in, q_norm_w, w_uq, kv_norm_w, w_uk, w_uv, qk_q_w, qk_k_w):
    wi = w_in[l]
    w_in_p = jnp.concatenate(
        [wi[:, :OFF_KROPE + ROPE_DIM], jnp.zeros((D_MODEL, LANE - ROPE_DIM), wi.dtype),
         wi[:, OFF_KROPE + ROPE_DIM:]], axis=1).astype(BF16)
    wq = w_uq[l].reshape(Q_LORA, MLA_HEADS, QK_HEAD)
    wq = jnp.pad(wq, ((0, 0), (0, 0), (0, QK_PAD - QK_HEAD))).reshape(Q_LORA, MLA_HEADS * QK_PAD)
    qscale = (QK_HEAD ** -0.5) * math.log2(math.e)
    qkq = jnp.pad(qk_q_w[l] * qscale, (0, QK_PAD - QK_HEAD))[None, :]
    kk = qk_k_w[l]
    return {
        "norm_w": norm_w[l][None, :],
        "w_in": w_in_p,
        "q_norm_w": q_norm_w[l][None, :],
        "w_uq": wq.astype(BF16),
        "qk_q_w": qkq,
        "kv_norm_w": kv_norm_w[l][None, :],
        "w_uk": w_uk[l].astype(BF16),
        "w_uvt": w_uv[l].T.astype(BF16),
        "qk_k_n": kk[None, :MLA_NOPE],
        "qk_k_r": jnp.pad(kk[MLA_NOPE:], (0, LANE - ROPE_DIM))[None, :],
    }


def _layer(x, mod3, ctx_mod, wts, w_out, lgf, lgb, gnw, rope_tabs, ctx, states, tm, tq, tk):
    outs = _in_call(x, mod3, ctx_mod, wts, rope_tabs, tm)
    q, k, vt, ga, rq, rk, rv, gb = outs[:8]
    mix_a = _attn_call(q, k, vt, ga, ctx, tq, tk)
    ret = _ret_call(lgf, lgb, rq, rk, rv, gb, gnw, states)
    y = _out_call(mix_a, ret[0], w_out, x, mod3, ctx_mod, tm)
    return y, outs[8:], ret[1:]


def kernel(x_prompt, x_sample, c, cache_mla_ckv, cache_mla_krope, state_ret_fwd, state_ret_bwd,
           c_ctx, norm_w, w_mod, b_mod, w_in, mla_q_norm_w, mla_w_uq, mla_kv_norm_w, mla_w_uk,
           mla_w_uv, mla_qk_q_w, mla_qk_k_w, ret_log_decay_fwd, ret_log_decay_bwd, ret_gn_w, w_out):
    depth = w_in.shape[0]
    dec_b = x_sample.shape[0]
    assert dec_b < MOD_ROWS
    cvec = jnp.zeros((MOD_ROWS, D_MODEL), F32).at[:dec_b].set(c).at[MOD_ROWS - 1].set(c_ctx)
    rope_tabs = _rope_tables(x_sample.shape[1])

    xp, xs = x_prompt, x_sample
    ckv_l, kr_l, sf_l, sb_l = [], [], [], []
    for l in range(depth):
        wts = _prep_weights(l, norm_w, w_in, mla_q_norm_w, mla_w_uq, mla_kv_norm_w, mla_w_uk,
                            mla_w_uv, mla_qk_q_w, mla_qk_k_w)
        w_out_b = w_out[l].astype(BF16)
        lgf = -jnp.exp(ret_log_decay_fwd[l].astype(F32))
        lgb = -jnp.exp(ret_log_decay_bwd[l].astype(F32))
        gnw = ret_gn_w[l][:, None, :]
        mod3 = _mod_call(cvec, w_mod[l], b_mod[l][None, :]).reshape(MOD_ROWS, 3, D_MODEL)

        xp, (ckv, kr), (sf, sb) = _layer(xp, mod3, True, wts, w_out_b, lgf, lgb, gnw,
                                         None, None, None, tm=256, tq=256, tk=256)
        ckv_l.append(ckv)
        kr_l.append(kr)
        sf_l.append(sf)
        sb_l.append(sb)

        kr_pad = jnp.pad(cache_mla_krope[:, l], ((0, 0), (0, 0), (0, LANE - ROPE_DIM)))
        ctx = _ctx_call(cache_mla_ckv[:, l], kr_pad, wts)
        xs, _, _ = _layer(xs, mod3, False, wts, w_out_b, lgf, lgb, gnw, rope_tabs, ctx,
                          (state_ret_fwd[:, l], state_ret_bwd[:, l]), tm=256, tq=1024, tk=512)

    return (xp, xs, jnp.stack(ckv_l, axis=1), jnp.stack(kr_l, axis=1),
            jnp.stack(sf_l, axis=1), jnp.stack(sb_l, axis=1))
```

```python
import functools
import math

import jax
import jax.numpy as jnp
from jax import lax
from jax.experimental import pallas as pl
from jax.experimental.pallas import tpu as pltpu

F32 = jnp.float32
BF16 = jnp.bfloat16

D_MODEL = 2048
GRID_W = 64
MLA_W = 1024
RET_W = 1024
MLA_NOPE = 128
ROPE_DIM = 64
QK_HEAD = MLA_NOPE + ROPE_DIM
MLA_V = 128
MLA_HEADS = 8
Q_LORA = 384
KV_LORA = 256
RET_DV = 256
RET_DK = 128
RET_HEADS = 4
RET_QK = RET_HEADS * RET_DK
RET_CHUNK = 128
ROPE_BASE = 10000.0
EPS = 1e-6

LANE = 128
QK_PAD = 2 * LANE
V_AUG = MLA_V + 16
OVERFLOW_GUARD = 2.0 ** 100
MOD_ROWS = 16
VMEM_LIMIT = 56 * 1024 * 1024

OFF_QLAT = 0
OFF_CKV = OFF_QLAT + Q_LORA
OFF_KROPE = OFF_CKV + KV_LORA
OFF_GA = OFF_KROPE + LANE
OFF_RQ = OFF_GA + MLA_W
OFF_RK = OFF_RQ + RET_QK
OFF_RV = OFF_RK + RET_QK
OFF_GB = OFF_RV + RET_W
IN_COLS_PAD = OFF_GB + RET_W

NT_DIMS = (((1,), (1,)), ((), ()))


def _silu(x):
    return x * (1.0 / (1.0 + jnp.exp(-x)))


def _rms(x, w, n):
    ms = jnp.sum(x * x, axis=-1, keepdims=True) * (1.0 / n)
    return x * lax.rsqrt(ms + EPS) * w


def _dot(a, b):
    return jnp.dot(a, b, preferred_element_type=F32)


def _mod_kernel(c_ref, w_ref, b_ref, o_ref):
    s = _silu(c_ref[...]).astype(BF16)
    o_ref[...] = _dot(s, w_ref[...].astype(BF16)) + b_ref[...]


def _mod_call(cvec, w_mod, b_mod):
    tn = 512
    n = w_mod.shape[1]
    return pl.pallas_call(
        _mod_kernel,
        grid=(n // tn,),
        in_specs=[pl.BlockSpec((MOD_ROWS, D_MODEL), lambda j: (0, 0)),
                  pl.BlockSpec((D_MODEL, tn), lambda j: (0, j)),
                  pl.BlockSpec((1, tn), lambda j: (0, j))],
        out_specs=pl.BlockSpec((MOD_ROWS, tn), lambda j: (0, j)),
        out_shape=jax.ShapeDtypeStruct((MOD_ROWS, n), F32),
        compiler_params=pltpu.CompilerParams(dimension_semantics=("arbitrary",),
                                             vmem_limit_bytes=VMEM_LIMIT),
        name="mod",
    )(cvec, w_mod, b_mod)


def _mla_rope(r, c, sa, sb):
    return r * c + pltpu.roll(r, LANE - 16, 1) * sa + pltpu.roll(r, 16, 1) * sb


def _keys_values(ckv, kr, w_uk_ref, w_uvt_ref, kkn, kkr, rope, k_ref, vt_ref):
    ckv_b = ckv.astype(BF16)
    kn_all = _dot(ckv_b, w_uk_ref[...])
    krw = kr * kkr
    if rope is not None:
        krw = _mla_rope(krw, *rope)
    kr_ss = jnp.sum(kr * kr, axis=-1, keepdims=True)
    for h in range(MLA_HEADS):
        kn = kn_all[:, h * MLA_NOPE:(h + 1) * MLA_NOPE]
        ss = jnp.sum(kn * kn, axis=-1, keepdims=True) + kr_ss
        inv = lax.rsqrt(ss * (1.0 / QK_HEAD) + EPS)
        k_ref[0, h, :, :LANE] = (kn * inv * kkn).astype(BF16)
        k_ref[0, h, :, LANE:] = (krw * inv).astype(BF16)
    vt = lax.dot_general(w_uvt_ref[...], ckv_b, NT_DIMS, preferred_element_type=F32)
    ones_row = jnp.where(lax.broadcasted_iota(jnp.int32, (V_AUG - MLA_V, vt.shape[1]), 0) == 0,
                         1.0, 0.0).astype(BF16)
    for h in range(MLA_HEADS):
        vt_ref[0, h, :MLA_V, :] = vt[h * MLA_V:(h + 1) * MLA_V].astype(BF16)
        vt_ref[0, h, MLA_V:, :] = ones_row


def _in_kernel(*refs, rope, emit_cache):
    it = iter(refs)
    x_ref, mod_ref, nw_ref, w_ref = next(it), next(it), next(it), next(it)
    qnw_ref, wuq_ref, qkq_ref = next(it), next(it), next(it)
    kvw_ref, wuk_ref, wuvt_ref, kkn_ref, kkr_ref = next(it), next(it), next(it), next(it), next(it)
    if rope:
        mc_ref, msa_ref, msb_ref, rc_ref, rs_ref = next(it), next(it), next(it), next(it), next(it)
    q_ref, k_ref, vt_ref, ga_ref = next(it), next(it), next(it), next(it)
    rq_ref, rk_ref, rv_ref, gb_ref = next(it), next(it), next(it), next(it)
    if emit_cache:
        ckv_ref, kr_ref = next(it), next(it)

    x = x_ref[0]
    shift = mod_ref[0, 0:1, :]
    scale = mod_ref[0, 1:2, :]
    h = (_rms(x, nw_ref[...], D_MODEL) * (1.0 + scale) + shift).astype(BF16)

    mla_rope = (mc_ref[...], msa_ref[...], msb_ref[...]) if rope else None

    a = _dot(h, w_ref[:, OFF_QLAT:OFF_GA])
    qn = _rms(a[:, OFF_QLAT:OFF_CKV], qnw_ref[...], Q_LORA).astype(BF16)
    ckv = _rms(a[:, OFF_CKV:OFF_KROPE], kvw_ref[...], KV_LORA)
    kr = a[:, OFF_KROPE:OFF_GA]
    if emit_cache:
        ckv_ref[0] = ckv
        kr_ref[0] = kr[:, :ROPE_DIM]

    q_all = _dot(qn, wuq_ref[...])
    qkq = qkq_ref[...]
    for hh in range(MLA_HEADS):
        qh = q_all[:, hh * QK_PAD:(hh + 1) * QK_PAD]
        inv = lax.rsqrt(jnp.sum(qh * qh, axis=-1, keepdims=True) * (1.0 / QK_HEAD) + EPS)
        qh = qh * inv * qkq
        r = qh[:, LANE:]
        if rope:
            r = _mla_rope(r, *mla_rope)
        q_ref[0, hh, :, :LANE] = qh[:, :LANE].astype(BF16)
        q_ref[0, hh, :, LANE:] = r.astype(BF16)

    _keys_values(ckv, kr, wuk_ref, wuvt_ref, kkn_ref[...], kkr_ref[...], mla_rope, k_ref, vt_ref)

    ga_ref[0] = _silu(_dot(h, w_ref[:, OFF_GA:OFF_RQ])).astype(BF16)
    gb_ref[0] = _silu(_dot(h, w_ref[:, OFF_GB:IN_COLS_PAD])).astype(BF16)

    rqk = _dot(h, w_ref[:, OFF_RQ:OFF_RV])
    for hh in range(2 * RET_HEADS):
        t = rqk[:, hh * RET_DK:(hh + 1) * RET_DK]
        if hh >= RET_HEADS:
            t = t * (RET_DK ** -0.5)
        if rope:
            t = t * rc_ref[...] + pltpu.roll(t, RET_DK // 2, 1) * rs_ref[...]
        if hh < RET_HEADS:
            rq_ref[0, :, hh * RET_DK:(hh + 1) * RET_DK] = t.astype(BF16)
        else:
            g = hh - RET_HEADS
            rk_ref[0, :, g * RET_DK:(g + 1) * RET_DK] = t.astype(BF16)
    rv_ref[0] = _dot(h, w_ref[:, OFF_RV:OFF_GB]).astype(BF16)


def _const_spec(shape):
    return pl.BlockSpec(shape, lambda b, i: (0,) * len(shape), pipeline_mode=pl.Buffered(1))


def _in_call(x, mod3, ctx_mod, wts, rope_tabs, tm):
    B, L, _ = x.shape
    rope = rope_tabs is not None
    emit_cache = not rope
    mod_map = (lambda b, i: (MOD_ROWS - 1, 0, 0)) if ctx_mod else (lambda b, i: (b, 0, 0))
    in_specs = [pl.BlockSpec((1, tm, D_MODEL), lambda b, i: (b, i, 0)),
                pl.BlockSpec((1, 3, D_MODEL), mod_map),
                _const_spec((1, D_MODEL)),
                _const_spec((D_MODEL, IN_COLS_PAD)),
                _const_spec((1, Q_LORA)),
                _const_spec((Q_LORA, MLA_HEADS * QK_PAD)),
                _const_spec((1, QK_PAD)),
                _const_spec((1, KV_LORA)),
                _const_spec((KV_LORA, MLA_HEADS * MLA_NOPE)),
                _const_spec((MLA_W, KV_LORA)),
                _const_spec((1, LANE)),
                _const_spec((1, LANE))]
    args = [x, mod3, wts["norm_w"], wts["w_in"], wts["q_norm_w"], wts["w_uq"], wts["qk_q_w"],
            wts["kv_norm_w"], wts["w_uk"], wts["w_uvt"], wts["qk_k_n"], wts["qk_k_r"]]
    if rope:
        in_specs += [pl.BlockSpec((tm, LANE), lambda b, i: (i, 0))] * 5
        args += list(rope_tabs)
    tok = lambda w: pl.BlockSpec((1, tm, w), lambda b, i: (b, i, 0))
    out_specs = [pl.BlockSpec((1, MLA_HEADS, tm, QK_PAD), lambda b, i: (b, 0, i, 0)),
                 pl.BlockSpec((1, MLA_HEADS, tm, QK_PAD), lambda b, i: (b, 0, i, 0)),
                 pl.BlockSpec((1, MLA_HEADS, V_AUG, tm), lambda b, i: (b, 0, 0, i)),
                 tok(MLA_W), tok(RET_QK), tok(RET_QK), tok(RET_W), tok(RET_W)]
    out_shape = [jax.ShapeDtypeStruct((B, MLA_HEADS, L, QK_PAD), BF16),
                 jax.ShapeDtypeStruct((B, MLA_HEADS, L, QK_PAD), BF16),
                 jax.ShapeDtypeStruct((B, MLA_HEADS, V_AUG, L), BF16),
                 jax.ShapeDtypeStruct((B, L, MLA_W), BF16),
                 jax.ShapeDtypeStruct((B, L, RET_QK), BF16),
                 jax.ShapeDtypeStruct((B, L, RET_QK), BF16),
                 jax.ShapeDtypeStruct((B, L, RET_W), BF16),
                 jax.ShapeDtypeStruct((B, L, RET_W), BF16)]
    if emit_cache:
        out_specs += [tok(KV_LORA), tok(ROPE_DIM)]
        out_shape += [jax.ShapeDtypeStruct((B, L, KV_LORA), F32),
                      jax.ShapeDtypeStruct((B, L, ROPE_DIM), F32)]
    return pl.pallas_call(
        functools.partial(_in_kernel, rope=rope, emit_cache=emit_cache),
        grid=(B, L // tm),
        in_specs=in_specs, out_specs=out_specs, out_shape=out_shape,
        compiler_params=pltpu.CompilerParams(dimension_semantics=("parallel", "parallel"),
                                             vmem_limit_bytes=VMEM_LIMIT),
        name="in_proj_rope" if rope else "in_proj",
    )(*args)


def _ctx_kernel(ckv_ref, kr_ref, wuk_ref, wuvt_ref, kkn_ref, kkr_ref, k_ref, vt_ref):
    _keys_values(ckv_ref[0], kr_ref[0], wuk_ref, wuvt_ref, kkn_ref[...], kkr_ref[...], None,
                 k_ref, vt_ref)


def _ctx_call(ckv, kr_pad, wts):
    B, P, _ = ckv.shape
    cs = lambda shape: pl.BlockSpec(shape, lambda b: (0,) * len(shape))
    return pl.pallas_call(
        _ctx_kernel,
        grid=(B,),
        in_specs=[pl.BlockSpec((1, P, KV_LORA), lambda b: (b, 0, 0)),
                  pl.BlockSpec((1, P, LANE), lambda b: (b, 0, 0)),
                  cs((KV_LORA, MLA_HEADS * MLA_NOPE)), cs((MLA_W, KV_LORA)),
                  cs((1, LANE)), cs((1, LANE))],
        out_specs=[pl.BlockSpec((1, MLA_HEADS, P, QK_PAD), lambda b: (b, 0, 0, 0)),
                   pl.BlockSpec((1, MLA_HEADS, V_AUG, P), lambda b: (b, 0, 0, 0))],
        out_shape=[jax.ShapeDtypeStruct((B, MLA_HEADS, P, QK_PAD), BF16),
                   jax.ShapeDtypeStruct((B, MLA_HEADS, V_AUG, P), BF16)],
        compiler_params=pltpu.CompilerParams(dimension_semantics=("parallel",),
                                             vmem_limit_bytes=VMEM_LIMIT),
        name="ctx_kv",
    )(ckv, kr_pad, wts["w_uk"], wts["w_uvt"], wts["qk_k_n"], wts["qk_k_r"])


def _attn_kernel(*refs, heads, **kw):
    sums = [_attn_head(hh, *refs, robust=False, **kw) for hh in range(heads)]
    worst = functools.reduce(jnp.maximum, sums)

    @pl.when(jnp.logical_not(jnp.max(worst) < OVERFLOW_GUARD))
    def _():
        for hh in range(heads):
            _attn_head(hh, *refs, robust=True, **kw)


def _attn_head(hh, *refs, tk, n_chunks, has_ctx, robust):
    if has_ctx:
        q_ref, k_ref, vt_ref, kc_ref, vtc_ref, ga_ref, o_ref, s_scr = refs
    else:
        q_ref, k_ref, vt_ref, ga_ref, o_ref, s_scr = refs
    q = q_ref[0, hh]
    tq = q.shape[0]
    cols = slice(hh * MLA_V, (hh + 1) * MLA_V)

    def scores(kc):
        return lax.dot_general(kc, q, NT_DIMS, preferred_element_type=F32)

    def chunk(j):
        s0 = pl.multiple_of(j * tk, tk)
        return k_ref[0, hh, pl.ds(s0, tk), :], vt_ref[0, hh, :, pl.ds(s0, tk)]

    def finish(acc):
        out = (acc[:MLA_V] * (1.0 / acc[MLA_V:MLA_V + 1])).T
        o_ref[0, :, cols] = (out * ga_ref[0, :, cols].astype(F32)).astype(BF16)

    if robust:
        def step(kc, vtc, carry):
            m, acc = carry
            st = scores(kc)
            m_new = jnp.maximum(m, jnp.max(st, axis=0, keepdims=True))
            p = jnp.exp2(st - m_new).astype(BF16)
            return m_new, jnp.exp2(m - m_new) * acc + _dot(vtc, p)

        carry = (jnp.full((1, tq), -1e30, F32), jnp.zeros((V_AUG, tq), F32))
        carry = lax.fori_loop(0, n_chunks, lambda j, c: step(*chunk(j), c), carry)
        if has_ctx:
            carry = step(kc_ref[0, hh], vtc_ref[0, hh], carry)
        finish(carry[1])
        return None

    s_scr[hh, 0] = scores(k_ref[0, hh, 0:tk, :])
    m_fix = jnp.max(s_scr[hh, 0], axis=0, keepdims=True)

    def weights(st):
        return jnp.exp2((st - m_fix).astype(BF16))

    acc = jnp.zeros((V_AUG, tq), F32)
    for j in range(n_chunks):
        if j + 1 < n_chunks:
            s_scr[hh, (j + 1) % 2] = scores(k_ref[0, hh, (j + 1) * tk:(j + 2) * tk, :])
        elif has_ctx:
            st_ctx = scores(kc_ref[0, hh])
        acc = acc + _dot(vt_ref[0, hh, :, j * tk:(j + 1) * tk], weights(s_scr[hh, j % 2]))
    if has_ctx:
        acc = acc + _dot(vtc_ref[0, hh], weights(st_ctx))
    finish(acc)
    return acc[MLA_V:MLA_V + 1]


def _attn_call(q, k, vt, ga, ctx, tq, tk, heads):
    B, H, L, _ = q.shape
    has_ctx = ctx is not None
    in_specs = [pl.BlockSpec((1, heads, tq, QK_PAD), lambda b, h, i: (b, h, i, 0)),
                pl.BlockSpec((1, heads, L, QK_PAD), lambda b, h, i: (b, h, 0, 0)),
                pl.BlockSpec((1, heads, V_AUG, L), lambda b, h, i: (b, h, 0, 0))]
    args = [q, k, vt]
    if has_ctx:
        kc, vtc = ctx
        P = kc.shape[2]
        in_specs += [pl.BlockSpec((1, heads, P, QK_PAD), lambda b, h, i: (b, h, 0, 0)),
                     pl.BlockSpec((1, heads, V_AUG, P), lambda b, h, i: (b, h, 0, 0))]
        args += [kc, vtc]
    in_specs.append(pl.BlockSpec((1, tq, heads * MLA_V), lambda b, h, i: (b, i, h)))
    args.append(ga)
    return pl.pallas_call(
        functools.partial(_attn_kernel, heads=heads, tk=tk, n_chunks=L // tk, has_ctx=has_ctx),
        grid=(B, H // heads, L // tq),
        in_specs=in_specs,
        out_specs=pl.BlockSpec((1, tq, heads * MLA_V), lambda b, h, i: (b, i, h)),
        out_shape=jax.ShapeDtypeStruct((B, L, MLA_W), BF16),
        scratch_shapes=[pltpu.VMEM((heads, 2, tk, tq), F32)],
        compiler_params=pltpu.CompilerParams(
            dimension_semantics=("parallel", "parallel", "parallel"),
            vmem_limit_bytes=VMEM_LIMIT),
        name="attn_ctx" if has_ctx else "attn",
    )(*args)


def _ret_kernel(*refs, n_chunks, has_state, unroll):
    it = iter(refs)
    lgf_ref, lgb_ref = next(it), next(it)
    q_ref, k_ref, v_ref, gb_ref, gnw_ref = next(it), next(it), next(it), next(it), next(it)
    if has_state:
        s0f_ref, s0b_ref = next(it), next(it)
    out_ref = next(it)
    if not has_state:
        sf_ref, sb_ref = next(it), next(it)
    kv_scr, st_scr = next(it), next(it)

    C = RET_CHUNK
    hd = pl.program_id(1)
    lgf = lgf_ref[hd]
    lgb = lgb_ref[hd]
    ri = lax.broadcasted_iota(jnp.int32, (C, C), 0).astype(F32)
    ci = lax.broadcasted_iota(jnp.int32, (C, C), 1).astype(F32)
    diff = ri - ci
    low = diff >= 0
    mask = (jnp.where(low, jnp.exp(jnp.where(low, diff, 0.0) * lgf), 0.0)
            + jnp.where(low, 0.0, jnp.exp(jnp.where(low, 0.0, -diff) * lgb)))
    qd_f = jnp.exp((ri + 1.0) * lgf)
    kd_f = jnp.exp((C - 1.0 - ri) * lgf)
    qd_b = jnp.exp((C - ri) * lgb)
    kd_b = jnp.exp(ri * lgb)
    cd_f = jnp.exp(jnp.full((RET_DK, RET_DV), C, F32) * lgf)
    cd_b = jnp.exp(jnp.full((RET_DK, RET_DV), C, F32) * lgb)
    gnw = gnw_ref[0]

    def rows(n):
        return pl.ds(pl.multiple_of(n * C, C), C)

    def unrolled(fn):
        def body(i, _):
            for u in range(unroll):
                fn(i * unroll + u)
            return 0
        lax.fori_loop(0, n_chunks // unroll, body, 0)

    def summarise(n):
        sl = rows(n)
        kc = k_ref[0, sl, :].astype(F32)
        a = jnp.concatenate([kc * kd_f, kc * kd_b], axis=1)
        kv_scr[n] = _dot(a.T.astype(BF16), v_ref[0, sl, :])

    unrolled(summarise)

    if has_state:
        init = (s0f_ref[0, 0], s0b_ref[0, 0])
    else:
        init = (jnp.zeros((RET_DK, RET_DV), F32),) * 2

    def scan(i, carry):
        sf, sb = carry
        nb = n_chunks - 1 - i
        st_scr[i, :RET_DK, :] = sf.astype(BF16)
        st_scr[nb, RET_DK:, :] = sb.astype(BF16)
        return (sf * cd_f + kv_scr[i, :RET_DK, :], sb * cd_b + kv_scr[nb, RET_DK:, :])

    sf, sb = lax.fori_loop(0, n_chunks, scan, init)
    if not has_state:
        sf_ref[0, 0] = sf
        sb_ref[0, 0] = sb

    def output(n):
        sl = rows(n)
        qc = q_ref[0, sl, :]
        vc = v_ref[0, sl, :]
        s = lax.dot_general(qc, k_ref[0, sl, :], NT_DIMS, preferred_element_type=F32) * mask
        qf = qc.astype(F32)
        qq = jnp.concatenate([qf * qd_f, qf * qd_b], axis=1).astype(BF16)
        o = _dot(s.astype(BF16), vc) + _dot(qq, st_scr[n])
        y = _rms(o, gnw, RET_DV) * gb_ref[0, sl, :].astype(F32)
        out_ref[0, sl, :] = y.astype(BF16)

    unrolled(output)


def _ret_call(lgf, lgb, rq, rk, rv, gb, gnw, states):
    B, L, _ = rq.shape
    n_chunks = L // RET_CHUNK
    has_state = states is not None
    smem =pl.BlockSpec(memory_space=pltpu.SMEM)
    st_spec = pl.BlockSpec((1, 1, RET_DK, RET_DV), lambda b, h: (b, h, 0, 0))
    in_specs = [smem, smem,
                pl.BlockSpec((1, L, RET_DK), lambda b, h: (b, 0, h)),
                pl.BlockSpec((1, L, RET_DK), lambda b, h: (b, 0, h)),
                pl.BlockSpec((1, L, RET_DV), lambda b, h: (b, 0, h)),
                pl.BlockSpec((1, L, RET_DV), lambda b, h: (b, 0, h)),
                pl.BlockSpec((1, 1, RET_DV), lambda b, h: (h, 0, 0))]
    args = [lgf, lgb, rq, rk, rv, gb, gnw]
    out_specs = [pl.BlockSpec((1, L, RET_DV), lambda b, h: (b, 0, h))]
    out_shape = [jax.ShapeDtypeStruct((B, L, RET_W), BF16)]
    if has_state:
        in_specs += [st_spec, st_spec]
        args += list(states)
    else:
        out_specs += [st_spec, st_spec]
        out_shape += [jax.ShapeDtypeStruct((B, RET_HEADS, RET_DK, RET_DV), F32)] * 2
    return pl.pallas_call(
        functools.partial(_ret_kernel, n_chunks=n_chunks, has_state=has_state,
                          unroll=math.gcd(n_chunks, 4)),
        grid=(B, RET_HEADS),
        in_specs=in_specs, out_specs=out_specs, out_shape=out_shape,
        scratch_shapes=[pltpu.VMEM((n_chunks, 2 * RET_DK, RET_DV), F32),
                        pltpu.VMEM((n_chunks, 2 * RET_DK, RET_DV), BF16)],
        compiler_params=pltpu.CompilerParams(dimension_semantics=("parallel", "parallel"),
                                             vmem_limit_bytes=VMEM_LIMIT),
        name="ret_state" if has_state else "ret",
    )(*args)


def _out_kernel(ma_ref, mb_ref, w_ref, x_ref, mod_ref, o_ref):
    acc = _dot(ma_ref[0], w_ref[:MLA_W, :]) + _dot(mb_ref[0], w_ref[MLA_W:, :])
    o_ref[0] = x_ref[0] + mod_ref[0, 2:3, :] * acc


def _out_call(mix_a, mix_b, w_out, x, mod3, ctx_mod, tm):
    B, L, _ = x.shape
    mod_map = (lambda b, i: (MOD_ROWS - 1, 0, 0)) if ctx_mod else (lambda b, i: (b, 0, 0))
    return pl.pallas_call(
        _out_kernel,
        grid=(B, L // tm),
        in_specs=[pl.BlockSpec((1, tm, MLA_W), lambda b, i: (b, i, 0)),
                  pl.BlockSpec((1, tm, RET_W), lambda b, i: (b, i, 0)),
                  _const_spec((D_MODEL, D_MODEL)),
                  pl.BlockSpec((1, tm, D_MODEL), lambda b, i: (b, i, 0)),
                  pl.BlockSpec((1, 3, D_MODEL), mod_map)],
        out_specs=pl.BlockSpec((1, tm, D_MODEL), lambda b, i: (b, i, 0)),
        out_shape=jax.ShapeDtypeStruct((B, L, D_MODEL), F32),
        compiler_params=pltpu.CompilerParams(dimension_semantics=("parallel", "parallel"),
                                             vmem_limit_bytes=VMEM_LIMIT),
        name="out_proj",
    )(mix_a, mix_b, w_out, x, mod3)


def _rope_angles(pos, dim):
    half = dim // 2
    freqs = ROPE_BASE ** (-jnp.arange(half, dtype=F32) / half)
    return pos.astype(F32)[:, None] * freqs[None, :]


def _rope_tables(L):
    rows = L // GRID_W
    row = jnp.repeat(jnp.arange(rows), GRID_W)
    col = jnp.tile(jnp.arange(GRID_W), rows)
    ar = _rope_angles(row, ROPE_DIM // 2)
    ac = _rope_angles(col, ROPE_DIM // 2)
    cr, sr, cc, sc = jnp.cos(ar), jnp.sin(ar), jnp.cos(ac), jnp.sin(ac)
    z16 = jnp.zeros_like(cr)
    z64 = jnp.zeros((L, LANE - ROPE_DIM), F32)
    mc = jnp.concatenate([cr, cr, cc, cc, z64], axis=-1)
    msa = jnp.concatenate([-sr, z16, -sc, z16, z64], axis=-1)
    msb = jnp.concatenate([z16, sr, z16, sc, z64], axis=-1)
    at = _rope_angles(jnp.arange(L), RET_DK)
    rc = jnp.concatenate([jnp.cos(at), jnp.cos(at)], axis=-1)
    rs = jnp.concatenate([-jnp.sin(at), jnp.sin(at)], axis=-1)
    return mc, msa, msb, rc, rs


def _prep_weights(l, norm_w, w_in, q_norm_w, w_uq, kv_norm_w, w_uk, w_uv, qk_q_w, qk_k_w):
    wi = w_in[l]
    w_in_p = jnp.concatenate(
        [wi[:, :OFF_KROPE + ROPE_DIM], jnp.zeros((D_MODEL, LANE - ROPE_DIM), wi.dtype),
         wi[:, OFF_KROPE + ROPE_DIM:]], axis=1).astype(BF16)
    wq = w_uq[l].reshape(Q_LORA, MLA_HEADS, QK_HEAD)
    wq = jnp.pad(wq, ((0, 0), (0, 0), (0, QK_PAD - QK_HEAD))).reshape(Q_LORA, MLA_HEADS * QK_PAD)
    qscale = (QK_HEAD ** -0.5) * math.log2(math.e)
    qkq = jnp.pad(qk_q_w[l] * qscale, (0, QK_PAD - QK_HEAD))[None, :]
    kk = qk_k_w[l]
    return {
        "norm_w": norm_w[l][None, :],
        "w_in": w_in_p,
        "q_norm_w": q_norm_w[l][None, :],
        "w_uq": wq.astype(BF16),
        "qk_q_w": qkq,
        "kv_norm_w": kv_norm_w[l][None, :],
        "w_uk": w_uk[l].astype(BF16),
        "w_uvt": w_uv[l].T.astype(BF16),
        "qk_k_n": kk[None, :MLA_NOPE],
        "qk_k_r": jnp.pad(kk[MLA_NOPE:], (0, LANE - ROPE_DIM))[None, :],
    }


def _tiles(L):
    tm = min(L, 256)
    tq = min(L, 1024)
    tk = min(L, 512)
    heads = MLA_HEADS if L <= 512 else 1
    return tm, tq, tk, heads


def _layer(x, mod3, ctx_mod, wts, w_out, lgf, lgb, gnw, rope_tabs, ctx, states):
    tm, tq, tk, heads = _tiles(x.shape[1])
    outs = _in_call(x, mod3, ctx_mod, wts, rope_tabs, tm)
    q, k, vt, ga, rq, rk, rv, gb = outs[:8]
    mix_a = _attn_call(q, k, vt, ga, ctx, tq, tk, heads)
    ret = _ret_call(lgf, lgb, rq, rk, rv, gb, gnw, states)
    y = _out_call(mix_a, ret[0], w_out, x, mod3, ctx_mod, tm)
    return y, outs[8:], ret[1:]


def kernel(x_prompt, x_sample, c, cache_mla_ckv, cache_mla_krope, state_ret_fwd, state_ret_bwd,
           c_ctx, norm_w, w_mod, b_mod, w_in, mla_q_norm_w, mla_w_uq, mla_kv_norm_w, mla_w_uk,
           mla_w_uv, mla_qk_q_w, mla_qk_k_w, ret_log_decay_fwd, ret_log_decay_bwd, ret_gn_w, w_out):
    depth = w_in.shape[0]
    dec_b = x_sample.shape[0]
    assert dec_b < MOD_ROWS
    cvec = jnp.zeros((MOD_ROWS, D_MODEL), F32).at[:dec_b].set(c).at[MOD_ROWS - 1].set(c_ctx)
    rope_tabs = _rope_tables(x_sample.shape[1])

    xp, xs = x_prompt, x_sample
    ckv_l, kr_l, sf_l, sb_l = [], [], [], []
    for l in range(depth):
        wts = _prep_weights(l, norm_w, w_in, mla_q_norm_w, mla_w_uq, mla_kv_norm_w, mla_w_uk,
                            mla_w_uv, mla_qk_q_w, mla_qk_k_w)
        w_out_b = w_out[l].astype(BF16)
        lgf = -jnp.exp(ret_log_decay_fwd[l].astype(F32))
        lgb = -jnp.exp(ret_log_decay_bwd[l].astype(F32))
        gnw = ret_gn_w[l][:, None, :]
        mod3 = _mod_call(cvec, w_mod[l], b_mod[l][None, :]).reshape(MOD_ROWS, 3, D_MODEL)

        xp, (ckv, kr), (sf, sb) = _layer(xp, mod3, True, wts, w_out_b, lgf, lgb, gnw,
                                         None, None, None)
        ckv_l.append(ckv)
        kr_l.append(kr)
        sf_l.append(sf)
        sb_l.append(sb)

        kr_pad = jnp.pad(cache_mla_krope[:, l], ((0, 0), (0, 0), (0, LANE - ROPE_DIM)))
        ctx = _ctx_call(cache_mla_ckv[:, l], kr_pad, wts)
        xs, _, _ = _layer(xs, mod3, False, wts, w_out_b, lgf, lgb, gnw, rope_tabs, ctx,
                          (state_ret_fwd[:, l], state_ret_bwd[:, l]))

    return (xp, xs, jnp.stack(ckv_l, axis=1), jnp.stack(kr_l, axis=1),
            jnp.stack(sf_l, axis=1), jnp.stack(sb_l, axis=1))
```

```python
import functools
import math

import jax
import jax.numpy as jnp
from jax import lax
from jax.experimental import pallas as pl
from jax.experimental.pallas import tpu as pltpu

F32 = jnp.float32
BF16 = jnp.bfloat16

D_MODEL = 2048
GRID_W = 64
MLA_W = 1024
RET_W = 1024
MLA_NOPE = 128
ROPE_DIM = 64
QK_HEAD = MLA_NOPE + ROPE_DIM
MLA_V = 128
MLA_HEADS = 8
Q_LORA = 384
KV_LORA = 256
RET_DV = 256
RET_DK = 128
RET_HEADS = 4
RET_QK = RET_HEADS * RET_DK
RET_CHUNK = 128
ROPE_BASE = 10000.0
EPS = 1e-6

LANE = 128
QK_PAD = 2 * LANE
V_AUG = MLA_V + 16
OVERFLOW_GUARD = 2.0 ** 100
MOD_ROWS = 16
VMEM_LIMIT = 56 * 1024 * 1024

OFF_QLAT = 0
OFF_CKV = OFF_QLAT + Q_LORA
OFF_KROPE = OFF_CKV + KV_LORA
LAT_COLS = OFF_KROPE + LANE
LAT_SRC_COLS = OFF_KROPE + ROPE_DIM
OFF_GA = 0
OFF_RQ = OFF_GA + MLA_W
OFF_RK = OFF_RQ + RET_QK
OFF_RV = OFF_RK + RET_QK
OFF_GB = OFF_RV + RET_W
WIDE_COLS = OFF_GB + RET_W

NT_DIMS = (((1,), (1,)), ((), ()))


def _silu(x):
    return x * (1.0 / (1.0 + jnp.exp(-x)))


def _rms(x, w, n):
    ms = jnp.sum(x * x, axis=-1, keepdims=True) * (1.0 / n)
    return x * lax.rsqrt(ms + EPS) * w


def _dot(a, b):
    return jnp.dot(a, b, preferred_element_type=F32)


def _mod_kernel(c_ref, w_ref, b_ref, o_ref):
    s = _silu(c_ref[...]).astype(BF16)
    o_ref[...] = _dot(s, w_ref[...].astype(BF16)) + b_ref[...]


def _mod_call(cvec, w_mod, b_mod):
    tn = 512
    n = w_mod.shape[1]
    return pl.pallas_call(
        _mod_kernel,
        grid=(n // tn,),
        in_specs=[pl.BlockSpec((MOD_ROWS, D_MODEL), lambda j: (0, 0)),
                  pl.BlockSpec((D_MODEL, tn), lambda j: (0, j)),
                  pl.BlockSpec((1, tn), lambda j: (0, j))],
        out_specs=pl.BlockSpec((MOD_ROWS, tn), lambda j: (0, j)),
        out_shape=jax.ShapeDtypeStruct((MOD_ROWS, n), F32),
        compiler_params=pltpu.CompilerParams(dimension_semantics=("arbitrary",),
                                             vmem_limit_bytes=VMEM_LIMIT),
        name="mod",
    )(cvec, w_mod, b_mod)


def _mla_rope(r, c, sa, sb):
    return r * c + pltpu.roll(r, LANE - 16, 1) * sa + pltpu.roll(r, 16, 1) * sb


def _keys_values(ckv, kr, w_uk_ref, w_uvt_ref, kkn, kkr, rope, k_ref, vt_ref):
    ckv_b = ckv.astype(BF16)
    kn_all = _dot(ckv_b, w_uk_ref[...])
    krw = kr * kkr
    if rope is not None:
        krw = _mla_rope(krw, *rope)
    kr_ss = jnp.sum(kr * kr, axis=-1, keepdims=True)
    for h in range(MLA_HEADS):
        kn = kn_all[:, h * MLA_NOPE:(h + 1) * MLA_NOPE]
        ss = jnp.sum(kn * kn, axis=-1, keepdims=True) + kr_ss
        inv = lax.rsqrt(ss * (1.0 / QK_HEAD) + EPS)
        k_ref[0, h, :, :LANE] = (kn * inv * kkn).astype(BF16)
        k_ref[0, h, :, LANE:] = (krw * inv).astype(BF16)
    vt = lax.dot_general(w_uvt_ref[...], ckv_b, NT_DIMS, preferred_element_type=F32)
    ones_row = jnp.where(lax.broadcasted_iota(jnp.int32, (V_AUG - MLA_V, vt.shape[1]), 0) == 0,
                         1.0, 0.0).astype(BF16)
    for h in range(MLA_HEADS):
        vt_ref[0, h, :MLA_V, :] = vt[h * MLA_V:(h + 1) * MLA_V].astype(BF16)
        vt_ref[0, h, MLA_V:, :] = ones_row


def _in_kernel(*refs, rope, emit_cache):
    it = iter(refs)
    x_ref, mod_ref, nw_ref, wl_ref, w_ref = next(it), next(it), next(it), next(it), next(it)
    qnw_ref, wuq_ref, qkq_ref = next(it), next(it), next(it)
    kvw_ref, wuk_ref, wuvt_ref, kkn_ref, kkr_ref = next(it), next(it), next(it), next(it), next(it)
    if rope:
        mc_ref, msa_ref, msb_ref, rc_ref, rs_ref = next(it), next(it), next(it), next(it), next(it)
    q_ref, k_ref, vt_ref, ga_ref = next(it), next(it), next(it), next(it)
    rq_ref, rk_ref, rv_ref, gb_ref = next(it), next(it), next(it), next(it)
    if emit_cache:
        ckv_ref, kr_ref = next(it), next(it)

    x = x_ref[0]
    shift = mod_ref[0, 0:1, :]
    scale = mod_ref[0, 1:2, :]
    h = (_rms(x, nw_ref[...], D_MODEL) * (1.0 + scale) + shift).astype(BF16)

    mla_rope = (mc_ref[...], msa_ref[...], msb_ref[...]) if rope else None

    a = _dot(h, wl_ref[...])
    qn = _rms(a[:, OFF_QLAT:OFF_CKV], qnw_ref[...], Q_LORA).astype(BF16)
    ckv = _rms(a[:, OFF_CKV:OFF_KROPE], kvw_ref[...], KV_LORA)
    kr = a[:, OFF_KROPE:LAT_COLS]
    if emit_cache:
        ckv_ref[0] = ckv
        kr_ref[0] = kr[:, :ROPE_DIM]

    q_all = _dot(qn, wuq_ref[...])
    qkq = qkq_ref[...]
    for hh in range(MLA_HEADS):
        qh = q_all[:, hh * QK_PAD:(hh + 1) * QK_PAD]
        inv = lax.rsqrt(jnp.sum(qh * qh, axis=-1, keepdims=True) * (1.0 / QK_HEAD) + EPS)
        qh = qh * inv * qkq
        r = qh[:, LANE:]
        if rope:
            r = _mla_rope(r, *mla_rope)
        q_ref[0, hh, :, :LANE] = qh[:, :LANE].astype(BF16)
        q_ref[0, hh, :, LANE:] = r.astype(BF16)

    _keys_values(ckv, kr, wuk_ref, wuvt_ref, kkn_ref[...], kkr_ref[...], mla_rope, k_ref, vt_ref)

    ga_ref[0] = _silu(_dot(h, w_ref[:, OFF_GA:OFF_RQ])).astype(BF16)
    gb_ref[0] = _silu(_dot(h, w_ref[:, OFF_GB:WIDE_COLS])).astype(BF16)

    rqk = _dot(h, w_ref[:, OFF_RQ:OFF_RV])
    for hh in range(2 * RET_HEADS):
        t = rqk[:, hh * RET_DK:(hh + 1) * RET_DK]
        if hh >= RET_HEADS:
            t = t * (RET_DK ** -0.5)
        if rope:
            t = t * rc_ref[...] + pltpu.roll(t, RET_DK // 2, 1) * rs_ref[...]
        if hh < RET_HEADS:
            rq_ref[0, :, hh * RET_DK:(hh + 1) * RET_DK] = t.astype(BF16)
        else:
            g = hh - RET_HEADS
            rk_ref[0, :, g * RET_DK:(g + 1) * RET_DK] = t.astype(BF16)
    rv_ref[0] = _dot(h, w_ref[:, OFF_RV:OFF_GB]).astype(BF16)


def _const_spec(shape):
    return pl.BlockSpec(shape, lambda b, i: (0,) * len(shape), pipeline_mode=pl.Buffered(1))


def _in_call(x, mod3, ctx_mod, wts, rope_tabs, tm):
    B, L, _ = x.shape
    rope = rope_tabs is not None
    emit_cache = not rope
    mod_map = (lambda b, i: (MOD_ROWS - 1, 0, 0)) if ctx_mod else (lambda b, i: (b, 0, 0))
    in_specs = [pl.BlockSpec((1, tm, D_MODEL), lambda b, i: (b, i, 0)),
                pl.BlockSpec((1, 3, D_MODEL), mod_map),
                _const_spec((1, D_MODEL)),
                _const_spec((D_MODEL, LAT_COLS)),
                _const_spec((D_MODEL, WIDE_COLS)),
                _const_spec((1, Q_LORA)),
                _const_spec((Q_LORA, MLA_HEADS * QK_PAD)),
                _const_spec((1, QK_PAD)),
                _const_spec((1, KV_LORA)),
                _const_spec((KV_LORA, MLA_HEADS * MLA_NOPE)),
                _const_spec((MLA_W, KV_LORA)),
                _const_spec((1, LANE)),
                _const_spec((1, LANE))]
    args = [x, mod3, wts["norm_w"], wts["w_lat"], wts["w_wide"], wts["q_norm_w"], wts["w_uq"], wts["qk_q_w"],
            wts["kv_norm_w"], wts["w_uk"], wts["w_uvt"], wts["qk_k_n"], wts["qk_k_r"]]
    if rope:
        in_specs += [pl.BlockSpec((tm, LANE), lambda b, i: (i, 0))] * 5
        args += list(rope_tabs)
    tok = lambda w: pl.BlockSpec((1, tm, w), lambda b, i: (b, i, 0))
    out_specs = [pl.BlockSpec((1, MLA_HEADS, tm, QK_PAD), lambda b, i: (b, 0, i, 0)),
                 pl.BlockSpec((1, MLA_HEADS, tm, QK_PAD), lambda b, i: (b, 0, i, 0)),
                 pl.BlockSpec((1, MLA_HEADS, V_AUG, tm), lambda b, i: (b, 0, 0, i)),
                 tok(MLA_W), tok(RET_QK), tok(RET_QK), tok(RET_W), tok(RET_W)]
    out_shape = [jax.ShapeDtypeStruct((B, MLA_HEADS, L, QK_PAD), BF16),
                 jax.ShapeDtypeStruct((B, MLA_HEADS, L, QK_PAD), BF16),
                 jax.ShapeDtypeStruct((B, MLA_HEADS, V_AUG, L), BF16),
                 jax.ShapeDtypeStruct((B, L, MLA_W), BF16),
                 jax.ShapeDtypeStruct((B, L, RET_QK), BF16),
                 jax.ShapeDtypeStruct((B, L, RET_QK), BF16),
                 jax.ShapeDtypeStruct((B, L, RET_W), BF16),
                 jax.ShapeDtypeStruct((B, L, RET_W), BF16)]
    if emit_cache:
        out_specs += [tok(KV_LORA), tok(ROPE_DIM)]
        out_shape += [jax.ShapeDtypeStruct((B, L, KV_LORA), F32),
                      jax.ShapeDtypeStruct((B, L, ROPE_DIM), F32)]
    return pl.pallas_call(
        functools.partial(_in_kernel, rope=rope, emit_cache=emit_cache),
        grid=(B, L // tm),
        in_specs=in_specs, out_specs=out_specs, out_shape=out_shape,
        compiler_params=pltpu.CompilerParams(dimension_semantics=("parallel", "parallel"),
                                             vmem_limit_bytes=VMEM_LIMIT),
        name="in_proj_rope" if rope else "in_proj",
    )(*args)


def _ctx_kernel(ckv_ref, kr_ref, wuk_ref, wuvt_ref, kkn_ref, kkr_ref, k_ref, vt_ref):
    _keys_values(ckv_ref[0], kr_ref[0], wuk_ref, wuvt_ref, kkn_ref[...], kkr_ref[...], None,
                 k_ref, vt_ref)


def _ctx_call(ckv, kr_pad, wts):
    B, P, _ = ckv.shape
    cs = lambda shape: pl.BlockSpec(shape, lambda b: (0,) * len(shape))
    return pl.pallas_call(
        _ctx_kernel,
        grid=(B,),
        in_specs=[pl.BlockSpec((1, P, KV_LORA), lambda b: (b, 0, 0)),
                  pl.BlockSpec((1, P, LANE), lambda b: (b, 0, 0)),
                  cs((KV_LORA, MLA_HEADS * MLA_NOPE)), cs((MLA_W, KV_LORA)),
                  cs((1, LANE)), cs((1, LANE))],
        out_specs=[pl.BlockSpec((1, MLA_HEADS, P, QK_PAD), lambda b: (b, 0, 0, 0)),
                   pl.BlockSpec((1, MLA_HEADS, V_AUG, P), lambda b: (b, 0, 0, 0))],
        out_shape=[jax.ShapeDtypeStruct((B, MLA_HEADS, P, QK_PAD), BF16),
                   jax.ShapeDtypeStruct((B, MLA_HEADS, V_AUG, P), BF16)],
        compiler_params=pltpu.CompilerParams(dimension_semantics=("parallel",),
                                             vmem_limit_bytes=VMEM_LIMIT),
        name="ctx_kv",
    )(ckv, kr_pad, wts["w_uk"], wts["w_uvt"], wts["qk_k_n"], wts["qk_k_r"])


def _attn_kernel(*refs, heads, **kw):
    def first_scores(hh):
        if hh < heads:
            _attn_head(hh, *refs, phase="first", **kw)

    first_scores(0)
    sums = [_attn_head(hh, *refs, phase="fast", hook=functools.partial(first_scores, hh + 1),
                       **kw) for hh in range(heads)]
    worst = functools.reduce(jnp.maximum, sums)

    @pl.when(jnp.logical_not(jnp.max(worst) < OVERFLOW_GUARD))
    def _():
        for hh in range(heads):
            _attn_head(hh, *refs, phase="robust", **kw)


def _attn_head(hh, *refs, tk, n_chunks, has_ctx, phase, hook=None):
    if has_ctx:
        q_ref, k_ref, vt_ref, kc_ref, vtc_ref, ga_ref, o_ref, s_scr = refs
    else:
        q_ref, k_ref, vt_ref, ga_ref, o_ref, s_scr = refs
    q = q_ref[0, hh]
    tq = q.shape[0]
    cols = slice(hh * MLA_V, (hh + 1) * MLA_V)

    def scores(kc):
        return lax.dot_general(kc, q, NT_DIMS, preferred_element_type=F32)

    def chunk(j):
        s0 = pl.multiple_of(j * tk, tk)
        return k_ref[0, hh, pl.ds(s0, tk), :], vt_ref[0, hh, :, pl.ds(s0, tk)]

    def finish(acc):
        out = (acc[:MLA_V] * (1.0 / acc[MLA_V:MLA_V + 1])).T
        o_ref[0, :, cols] = (out * ga_ref[0, :, cols].astype(F32)).astype(BF16)

    if phase == "first":
        s_scr[hh, 0] = scores(k_ref[0, hh, 0:tk, :])
        return None

    if phase == "robust":
        def step(kc, vtc, carry):
            m, acc = carry
            st = scores(kc)
            m_new = jnp.maximum(m, jnp.max(st, axis=0, keepdims=True))
            p = jnp.exp2(st - m_new).astype(BF16)
            return m_new, jnp.exp2(m - m_new) * acc + _dot(vtc, p)

        carry = (jnp.full((1, tq), -1e30, F32), jnp.zeros((V_AUG, tq), F32))
        carry = lax.fori_loop(0, n_chunks, lambda j, c: step(*chunk(j), c), carry)
        if has_ctx:
            carry = step(kc_ref[0, hh], vtc_ref[0, hh], carry)
        finish(carry[1])
        return None

    m_fix = jnp.max(s_scr[hh, 0], axis=0, keepdims=True)

    def weights(st):
        return jnp.exp2((st - m_fix).astype(BF16))

    acc = jnp.zeros((V_AUG, tq), F32)
    for j in range(n_chunks):
        if j + 1 < n_chunks:
            s_scr[hh, (j + 1) % 2] = scores(k_ref[0, hh, (j + 1) * tk:(j + 2) * tk, :])
        elif has_ctx:
            st_ctx = scores(kc_ref[0, hh])
        if j + 1 == n_chunks and not has_ctx:
            hook()
        acc = acc + _dot(vt_ref[0, hh, :, j * tk:(j + 1) * tk], weights(s_scr[hh, j % 2]))
    if has_ctx:
        hook()
        acc = acc + _dot(vtc_ref[0, hh], weights(st_ctx))
    finish(acc)
    return acc[MLA_V:MLA_V + 1]


def _attn_call(q, k, vt, ga, ctx, tq, tk, heads):
    B, H, L, _ = q.shape
    has_ctx = ctx is not None
    in_specs = [pl.BlockSpec((1, heads, tq, QK_PAD), lambda b, h, i: (b, h, i, 0)),
                pl.BlockSpec((1, heads, L, QK_PAD), lambda b, h, i: (b, h, 0, 0)),
                pl.BlockSpec((1, heads, V_AUG, L), lambda b, h, i: (b, h, 0, 0))]
    args = [q, k, vt]
    if has_ctx:
        kc, vtc = ctx
        P = kc.shape[2]
        in_specs += [pl.BlockSpec((1, heads, P, QK_PAD), lambda b, h, i: (b, h, 0, 0)),
                     pl.BlockSpec((1, heads, V_AUG, P), lambda b, h, i: (b, h, 0, 0))]
        args += [kc, vtc]
    in_specs.append(pl.BlockSpec((1, tq, heads * MLA_V), lambda b, h, i: (b, i, h)))
    args.append(ga)
    return pl.pallas_call(
        functools.partial(_attn_kernel, heads=heads, tk=tk, n_chunks=L // tk, has_ctx=has_ctx),
        grid=(B, H // heads, L // tq),
        in_specs=in_specs,
        out_specs=pl.BlockSpec((1, tq, heads * MLA_V), lambda b, h, i: (b, i, h)),
        out_shape=jax.ShapeDtypeStruct((B, L, MLA_W), BF16),
        scratch_shapes=[pltpu.VMEM((heads, 2, tk, tq), F32)],
        compiler_params=pltpu.CompilerParams(
            dimension_semantics=("parallel", "parallel", "parallel"),
            vmem_limit_bytes=VMEM_LIMIT),
        name="attn_ctx" if has_ctx else "attn",
    )(*args)


def _ret_kernel(*refs, n_chunks, has_state, unroll):
    it = iter(refs)
    lgf_ref, lgb_ref = next(it), next(it)
    q_ref, k_ref, v_ref, gb_ref, gnw_ref = next(it), next(it), next(it), next(it), next(it)
    if has_state:
        s0f_ref, s0b_ref = next(it), next(it)
    out_ref = next(it)
    if not has_state:
        sf_ref, sb_ref = next(it), next(it)
    kv_scr, st_scr, o_scr = next(it), next(it), next(it)

    C = RET_CHUNK
    hd = pl.program_id(1)
    lgf = lgf_ref[hd]
    lgb = lgb_ref[hd]
    ri = lax.broadcasted_iota(jnp.int32, (C, C), 0).astype(F32)
    ci = lax.broadcasted_iota(jnp.int32, (C, C), 1).astype(F32)
    diff = ri - ci
    low = diff >= 0
    mask = (jnp.where(low, jnp.exp(jnp.where(low, diff, 0.0) * lgf), 0.0)
            + jnp.where(low, 0.0, jnp.exp(jnp.where(low, 0.0, -diff) * lgb)))
    qd_f = jnp.exp((ri + 1.0) * lgf)
    kd_f = jnp.exp((C - 1.0 - ri) * lgf)
    qd_b = jnp.exp((C - ri) * lgb)
    kd_b = jnp.exp(ri * lgb)
    cd_f = jnp.exp(jnp.full((RET_DK, RET_DV), C, F32) * lgf)
    cd_b = jnp.exp(jnp.full((RET_DK, RET_DV), C, F32) * lgb)
    gnw = gnw_ref[0]

    def rows(n):
        return pl.ds(n * C if isinstance(n, int) else pl.multiple_of(n * C, C), C)

    def unrolled(fn):
        def body(i, _):
            for u in range(unroll):
                fn(i * unroll + u)
            return 0
        lax.fori_loop(0, n_chunks // unroll, body, 0)

    def summarise(n):
        sl = rows(n)
        kc = k_ref[0, sl, :].astype(F32)
        a = jnp.concatenate([kc * kd_f, kc * kd_b], axis=1)
        kv_scr[n] = _dot(a.T.astype(BF16), v_ref[0, sl, :])

    unrolled(summarise)

    if has_state:
        init = (s0f_ref[0, 0], s0b_ref[0, 0])
    else:
        init = (jnp.zeros((RET_DK, RET_DV), F32),) * 2

    def scan(i, carry):
        sf, sb = carry
        nb = n_chunks - 1 - i
        st_scr[i, :RET_DK, :] = sf.astype(BF16)
        st_scr[nb, RET_DK:, :] = sb.astype(BF16)
        return (sf * cd_f + kv_scr[i, :RET_DK, :], sb * cd_b + kv_scr[nb, RET_DK:, :])

    sf, sb = lax.fori_loop(0, n_chunks, scan, init)
    if not has_state:
        sf_ref[0, 0] = sf
        sb_ref[0, 0] = sb

    def products(i):
        ns = [i * unroll + u for u in range(unroll)]
        sls = [rows(n) for n in ns]
        qs = [q_ref[0, sl, :] for sl in sls]
        ss = [lax.dot_general(q, k_ref[0, sl, :], NT_DIMS, preferred_element_type=F32)
              for q, sl in zip(qs, sls)]
        for n, sl, q, s in zip(ns, sls, qs, ss):
            qf = q.astype(F32)
            qq = jnp.concatenate([qf * qd_f, qf * qd_b], axis=1).astype(BF16)
            o_scr[sl, :] = _dot((s * mask).astype(BF16), v_ref[0, sl, :]) + _dot(qq, st_scr[n])

    def normalise(i):
        for u in range(unroll):
            sl = rows(i * unroll + u)
            y = _rms(o_scr[sl, :], gnw, RET_DV) * gb_ref[0, sl, :].astype(F32)
            out_ref[0, sl, :] = y.astype(BF16)

    def both(i, _):
        normalise(i - 1)
        products(i)
        return 0

    trips = n_chunks // unroll
    products(0)
    lax.fori_loop(1, trips, both, 0)
    normalise(trips - 1)


def _ret_call(lgf, lgb, rq, rk, rv, gb, gnw, states):
    B, L, _ = rq.shape
    n_chunks = L // RET_CHUNK
    has_state = states is not None
    smem =pl.BlockSpec(memory_space=pltpu.SMEM)
    st_spec = pl.BlockSpec((1, 1, RET_DK, RET_DV), lambda b, h: (b, h, 0, 0))
    in_specs = [smem, smem,
                pl.BlockSpec((1, L, RET_DK), lambda b, h: (b, 0, h)),
                pl.BlockSpec((1, L, RET_DK), lambda b, h: (b, 0, h)),
                pl.BlockSpec((1, L, RET_DV), lambda b, h: (b, 0, h)),
                pl.BlockSpec((1, L, RET_DV), lambda b, h: (b, 0, h)),
                pl.BlockSpec((1, 1, RET_DV), lambda b, h: (h, 0, 0))]
    args = [lgf, lgb, rq, rk, rv, gb, gnw]
    out_specs = [pl.BlockSpec((1, L, RET_DV), lambda b, h: (b, 0, h))]
    out_shape = [jax.ShapeDtypeStruct((B, L, RET_W), BF16)]
    if has_state:
        in_specs += [st_spec, st_spec]
        args += list(states)
    else:
        out_specs += [st_spec, st_spec]
        out_shape += [jax.ShapeDtypeStruct((B, RET_HEADS, RET_DK, RET_DV), F32)] * 2
    return pl.pallas_call(
        functools.partial(_ret_kernel, n_chunks=n_chunks, has_state=has_state,
                          unroll=math.gcd(n_chunks, 4)),
        grid=(B, RET_HEADS),
        in_specs=in_specs, out_specs=out_specs, out_shape=out_shape,
        scratch_shapes=[pltpu.VMEM((n_chunks, 2 * RET_DK, RET_DV), F32),
                        pltpu.VMEM((n_chunks, 2 * RET_DK, RET_DV), BF16),
                        pltpu.VMEM((L, RET_DV), F32)],
        compiler_params=pltpu.CompilerParams(dimension_semantics=("parallel", "parallel"),
                                             vmem_limit_bytes=VMEM_LIMIT),
        name="ret_state" if has_state else "ret",
    )(*args)


def _out_kernel(ma_ref, mb_ref, w_ref, x_ref, mod_ref, o_ref):
    acc = _dot(ma_ref[0], w_ref[:MLA_W, :]) + _dot(mb_ref[0], w_ref[MLA_W:, :])
    o_ref[0] = x_ref[0] + mod_ref[0, 2:3, :] * acc


def _out_call(mix_a, mix_b, w_out, x, mod3, ctx_mod, tm):
    B, L, _ = x.shape
    mod_map = (lambda b, i: (MOD_ROWS - 1, 0, 0)) if ctx_mod else (lambda b, i: (b, 0, 0))
    return pl.pallas_call(
        _out_kernel,
        grid=(B, L // tm),
        in_specs=[pl.BlockSpec((1, tm, MLA_W), lambda b, i: (b, i, 0)),
                  pl.BlockSpec((1, tm, RET_W), lambda b, i: (b, i, 0)),
                  _const_spec((D_MODEL, D_MODEL)),
                  pl.BlockSpec((1, tm, D_MODEL), lambda b, i: (b, i, 0)),
                  pl.BlockSpec((1, 3, D_MODEL), mod_map)],
        out_specs=pl.BlockSpec((1, tm, D_MODEL), lambda b, i: (b, i, 0)),
        out_shape=jax.ShapeDtypeStruct((B, L, D_MODEL), F32),
        compiler_params=pltpu.CompilerParams(dimension_semantics=("parallel", "parallel"),
                                             vmem_limit_bytes=VMEM_LIMIT),
        name="out_proj",
    )(mix_a, mix_b, w_out, x, mod3)


def _rope_angles(pos, dim):
    half = dim // 2
    freqs = ROPE_BASE ** (-jnp.arange(half, dtype=F32) / half)
    return pos.astype(F32)[:, None] * freqs[None, :]


def _rope_tables(L):
    rows = L // GRID_W
    row = jnp.repeat(jnp.arange(rows), GRID_W)
    col = jnp.tile(jnp.arange(GRID_W), rows)
    ar = _rope_angles(row, ROPE_DIM // 2)
    ac = _rope_angles(col, ROPE_DIM // 2)
    cr, sr, cc, sc = jnp.cos(ar), jnp.sin(ar), jnp.cos(ac), jnp.sin(ac)
    z16 = jnp.zeros_like(cr)
    z64 = jnp.zeros((L, LANE - ROPE_DIM), F32)
    mc = jnp.concatenate([cr, cr, cc, cc, z64], axis=-1)
    msa = jnp.concatenate([-sr, z16, -sc, z16, z64], axis=-1)
    msb = jnp.concatenate([z16, sr, z16, sc, z64], axis=-1)
    at = _rope_angles(jnp.arange(L), RET_DK)
    rc = jnp.concatenate([jnp.cos(at), jnp.cos(at)], axis=-1)
    rs = jnp.concatenate([-jnp.sin(at), jnp.sin(at)], axis=-1)
    return mc, msa, msb, rc, rs


def _prep_weights(l, norm_w, w_in, q_norm_w, w_uq, kv_norm_w, w_uk, w_uv, qk_q_w, qk_k_w):
    wi = w_in[l]
    w_lat = jnp.pad(wi[:, :LAT_SRC_COLS], ((0, 0), (0, LAT_COLS - LAT_SRC_COLS))).astype(BF16)
    w_wide = wi[:, LAT_SRC_COLS:].astype(BF16)
    wq = w_uq[l].reshape(Q_LORA, MLA_HEADS, QK_HEAD)
    wq = jnp.pad(wq, ((0, 0), (0, 0), (0, QK_PAD - QK_HEAD))).reshape(Q_LORA, MLA_HEADS * QK_PAD)
    qscale = (QK_HEAD ** -0.5) * math.log2(math.e)
    qkq = jnp.pad(qk_q_w[l] * qscale, (0, QK_PAD - QK_HEAD))[None, :]
    kk = qk_k_w[l]
    return {
        "norm_w": norm_w[l][None, :],
        "w_lat": w_lat,
        "w_wide": w_wide,
        "q_norm_w": q_norm_w[l][None, :],
        "w_uq": wq.astype(BF16),
        "qk_q_w": qkq,
        "kv_norm_w": kv_norm_w[l][None, :],
        "w_uk": w_uk[l].astype(BF16),
        "w_uvt": w_uv[l].T.astype(BF16),
        "qk_k_n": kk[None, :MLA_NOPE],
        "qk_k_r": jnp.pad(kk[MLA_NOPE:], (0, LANE - ROPE_DIM))[None, :],
    }


def _tiles(L):
    tm = min(L, 256)
    tq = min(L, 1024)
    tk = min(L, 256)
    heads = MLA_HEADS if L <= 512 else 2
    return tm, tq, tk, heads


def _layer(x, mod3, ctx_mod, wts, w_out, lgf, lgb, gnw, rope_tabs, ctx, states):
    tm, tq, tk, heads = _tiles(x.shape[1])
    outs = _in_call(x, mod3, ctx_mod, wts, rope_tabs, tm)
    q, k, vt, ga, rq, rk, rv, gb = outs[:8]
    mix_a = _attn_call(q, k, vt, ga, ctx, tq, tk, heads)
    ret = _ret_call(lgf, lgb, rq, rk, rv, gb, gnw, states)
    y = _out_call(mix_a, ret[0], w_out, x, mod3, ctx_mod, tm)
    return y, outs[8:], ret[1:]


def kernel(x_prompt, x_sample, c, cache_mla_ckv, cache_mla_krope, state_ret_fwd, state_ret_bwd,
           c_ctx, norm_w, w_mod, b_mod, w_in, mla_q_norm_w, mla_w_uq, mla_kv_norm_w, mla_w_uk,
           mla_w_uv, mla_qk_q_w, mla_qk_k_w, ret_log_decay_fwd, ret_log_decay_bwd, ret_gn_w, w_out):
    depth = w_in.shape[0]
    dec_b = x_sample.shape[0]
    assert dec_b < MOD_ROWS
    cvec = jnp.zeros((MOD_ROWS, D_MODEL), F32).at[:dec_b].set(c).at[MOD_ROWS - 1].set(c_ctx)
    rope_tabs = _rope_tables(x_sample.shape[1])

    xp, xs = x_prompt, x_sample
    ckv_l, kr_l, sf_l, sb_l = [], [], [], []
    for l in range(depth):
        wts = _prep_weights(l, norm_w, w_in, mla_q_norm_w, mla_w_uq, mla_kv_norm_w, mla_w_uk,
                            mla_w_uv, mla_qk_q_w, mla_qk_k_w)
        w_out_b = w_out[l].astype(BF16)
        lgf = -jnp.exp(ret_log_decay_fwd[l].astype(F32))
        lgb = -jnp.exp(ret_log_decay_bwd[l].astype(F32))
        gnw = ret_gn_w[l][:, None, :]
        mod3 = _mod_call(cvec, w_mod[l], b_mod[l][None, :]).reshape(MOD_ROWS, 3, D_MODEL)

        xp, (ckv, kr), (sf, sb) = _layer(xp, mod3, True, wts, w_out_b, lgf, lgb, gnw,
                                         None, None, None)
        ckv_l.append(ckv)
        kr_l.append(kr)
        sf_l.append(sf)
        sb_l.append(sb)

        kr_pad = jnp.pad(cache_mla_krope[:, l], ((0, 0), (0, 0), (0, LANE - ROPE_DIM)))
        ctx = _ctx_call(cache_mla_ckv[:, l], kr_pad, wts)
        xs, _, _ = _layer(xs, mod3, False, wts, w_out_b, lgf, lgb, gnw, rope_tabs, ctx,
                          (state_ret_fwd[:, l], state_ret_bwd[:, l]))

    return (xp, xs, jnp.stack(ckv_l, axis=1), jnp.stack(kr_l, axis=1),
            jnp.stack(sf_l, axis=1), jnp.stack(sb_l, axis=1))
```

```python
import functools
import math

import jax
import jax.numpy as jnp
from jax import lax
from jax.experimental import pallas as pl
from jax.experimental.pallas import tpu as pltpu

F32 = jnp.float32
BF16 = jnp.bfloat16

D_MODEL = 2048
GRID_W = 64
MLA_W = 1024
RET_W = 1024
MLA_NOPE = 128
ROPE_DIM = 64
QK_HEAD = MLA_NOPE + ROPE_DIM
MLA_V = 128
MLA_HEADS = 8
Q_LORA = 384
KV_LORA = 256
RET_DV = 256
RET_DK = 128
RET_HEADS = 4
RET_QK = RET_HEADS * RET_DK
RET_CHUNK = 128
ROPE_BASE = 10000.0
EPS = 1e-6

LANE = 128
QK_PAD = 2 * LANE
V_AUG = MLA_V + 16
OVERFLOW_GUARD = 2.0 ** 100
MOD_ROWS = 16
VMEM_LIMIT = 56 * 1024 * 1024

OFF_QLAT = 0
OFF_CKV = OFF_QLAT + Q_LORA
OFF_KROPE = OFF_CKV + KV_LORA
LAT_COLS = OFF_KROPE + LANE
LAT_SRC_COLS = OFF_KROPE + ROPE_DIM
OFF_GA = 0
OFF_RQ = OFF_GA + MLA_W
OFF_RK = OFF_RQ + RET_QK
OFF_RV = OFF_RK + RET_QK
OFF_GB = OFF_RV + RET_W
WIDE_COLS = OFF_GB + RET_W

NT_DIMS = (((1,), (1,)), ((), ()))


def _silu(x):
    return x * (1.0 / (1.0 + jnp.exp(-x)))


def _rms(x, w, n):
    ms = jnp.sum(x * x, axis=-1, keepdims=True) * (1.0 / n)
    return x * lax.rsqrt(ms + EPS) * w


def _dot(a, b):
    return jnp.dot(a, b, preferred_element_type=F32)


def _mod_kernel(c_ref, w_ref, b_ref, o_ref):
    s = _silu(c_ref[...]).astype(BF16)
    o_ref[...] = _dot(s, w_ref[...].astype(BF16)) + b_ref[...]


def _mod_call(cvec, w_mod, b_mod):
    tn = 512
    n = w_mod.shape[1]
    return pl.pallas_call(
        _mod_kernel,
        grid=(n // tn,),
        in_specs=[pl.BlockSpec((MOD_ROWS, D_MODEL), lambda j: (0, 0)),
                  pl.BlockSpec((D_MODEL, tn), lambda j: (0, j)),
                  pl.BlockSpec((1, tn), lambda j: (0, j))],
        out_specs=pl.BlockSpec((MOD_ROWS, tn), lambda j: (0, j)),
        out_shape=jax.ShapeDtypeStruct((MOD_ROWS, n), F32),
        compiler_params=pltpu.CompilerParams(dimension_semantics=("arbitrary",),
                                             vmem_limit_bytes=VMEM_LIMIT),
        name="mod",
    )(cvec, w_mod, b_mod)


def _mla_rope(r, c, sa, sb):
    return r * c + pltpu.roll(r, LANE - 16, 1) * sa + pltpu.roll(r, 16, 1) * sb


def _keys_values(ckv, kr, w_uk_ref, w_uvt_ref, kkn, kkr, rope, k_ref, vt_ref):
    ckv_b = ckv.astype(BF16)
    kn_all = _dot(ckv_b, w_uk_ref[...])
    krw = kr * kkr
    if rope is not None:
        krw = _mla_rope(krw, *rope)
    kr_ss = jnp.sum(kr * kr, axis=-1, keepdims=True)
    for h in range(MLA_HEADS):
        kn = kn_all[:, h * MLA_NOPE:(h + 1) * MLA_NOPE]
        ss = jnp.sum(kn * kn, axis=-1, keepdims=True) + kr_ss
        inv = lax.rsqrt(ss * (1.0 / QK_HEAD) + EPS)
        k_ref[0, h, :, :LANE] = (kn * inv * kkn).astype(BF16)
        k_ref[0, h, :, LANE:] = (krw * inv).astype(BF16)
    vt = lax.dot_general(w_uvt_ref[...], ckv_b, NT_DIMS, preferred_element_type=F32)
    ones_row = jnp.where(lax.broadcasted_iota(jnp.int32, (V_AUG - MLA_V, vt.shape[1]), 0) == 0,
                         1.0, 0.0).astype(BF16)
    for h in range(MLA_HEADS):
        vt_ref[0, h, :MLA_V, :] = vt[h * MLA_V:(h + 1) * MLA_V].astype(BF16)
        vt_ref[0, h, MLA_V:, :] = ones_row


def _in_kernel(*refs, rope, emit_cache):
    it = iter(refs)
    x_ref, mod_ref, nw_ref, wl_ref, w_ref = next(it), next(it), next(it), next(it), next(it)
    qnw_ref, wuq_ref, qkq_ref = next(it), next(it), next(it)
    kvw_ref, wuk_ref, wuvt_ref, kkn_ref, kkr_ref = next(it), next(it), next(it), next(it), next(it)
    if rope:
        mc_ref, msa_ref, msb_ref, rc_ref, rs_ref = next(it), next(it), next(it), next(it), next(it)
    q_ref, k_ref, vt_ref, ga_ref = next(it), next(it), next(it), next(it)
    rq_ref, rk_ref, rv_ref, gb_ref = next(it), next(it), next(it), next(it)
    if emit_cache:
        ckv_ref, kr_ref = next(it), next(it)

    x = x_ref[0]
    shift = mod_ref[0, 0:1, :]
    scale = mod_ref[0, 1:2, :]
    h = (_rms(x, nw_ref[...], D_MODEL) * (1.0 + scale) + shift).astype(BF16)

    mla_rope = (mc_ref[...], msa_ref[...], msb_ref[...]) if rope else None

    a = _dot(h, wl_ref[...])
    qn = _rms(a[:, OFF_QLAT:OFF_CKV], qnw_ref[...], Q_LORA).astype(BF16)
    ckv = _rms(a[:, OFF_CKV:OFF_KROPE], kvw_ref[...], KV_LORA)
    kr = a[:, OFF_KROPE:LAT_COLS]
    if emit_cache:
        ckv_ref[0] = ckv
        kr_ref[0] = kr[:, :ROPE_DIM]

    q_all = _dot(qn, wuq_ref[...])
    qkq = qkq_ref[...]
    for hh in range(MLA_HEADS):
        qh = q_all[:, hh * QK_PAD:(hh + 1) * QK_PAD]
        inv = lax.rsqrt(jnp.sum(qh * qh, axis=-1, keepdims=True) * (1.0 / QK_HEAD) + EPS)
        qh = qh * inv * qkq
        r = qh[:, LANE:]
        if rope:
            r = _mla_rope(r, *mla_rope)
        q_ref[0, hh, :, :LANE] = qh[:, :LANE].astype(BF16)
        q_ref[0, hh, :, LANE:] = r.astype(BF16)

    _keys_values(ckv, kr, wuk_ref, wuvt_ref, kkn_ref[...], kkr_ref[...], mla_rope, k_ref, vt_ref)

    ga_ref[0] = _silu(_dot(h, w_ref[:, OFF_GA:OFF_RQ])).astype(BF16)
    gb_ref[0] = _silu(_dot(h, w_ref[:, OFF_GB:WIDE_COLS])).astype(BF16)

    rqk = _dot(h, w_ref[:, OFF_RQ:OFF_RV])
    for hh in range(2 * RET_HEADS):
        t = rqk[:, hh * RET_DK:(hh + 1) * RET_DK]
        if hh >= RET_HEADS:
            t = t * (RET_DK ** -0.5)
        if rope:
            t = t * rc_ref[...] + pltpu.roll(t, RET_DK // 2, 1) * rs_ref[...]
        if hh < RET_HEADS:
            rq_ref[0, :, hh * RET_DK:(hh + 1) * RET_DK] = t.astype(BF16)
        else:
            g = hh - RET_HEADS
            rk_ref[0, :, g * RET_DK:(g + 1) * RET_DK] = t.astype(BF16)
    rv_ref[0] = _dot(h, w_ref[:, OFF_RV:OFF_GB]).astype(BF16)


def _const_spec(shape):
    return pl.BlockSpec(shape, lambda b, i: (0,) * len(shape), pipeline_mode=pl.Buffered(1))


def _in_call(x, mod3, ctx_mod, wts, rope_tabs, tm):
    B, L, _ = x.shape
    rope = rope_tabs is not None
    emit_cache = not rope
    mod_map = (lambda b, i: (MOD_ROWS - 1, 0, 0)) if ctx_mod else (lambda b, i: (b, 0, 0))
    in_specs = [pl.BlockSpec((1, tm, D_MODEL), lambda b, i: (b, i, 0)),
                pl.BlockSpec((1, 3, D_MODEL), mod_map),
                _const_spec((1, D_MODEL)),
                _const_spec((D_MODEL, LAT_COLS)),
                _const_spec((D_MODEL, WIDE_COLS)),
                _const_spec((1, Q_LORA)),
                _const_spec((Q_LORA, MLA_HEADS * QK_PAD)),
                _const_spec((1, QK_PAD)),
                _const_spec((1, KV_LORA)),
                _const_spec((KV_LORA, MLA_HEADS * MLA_NOPE)),
                _const_spec((MLA_W, KV_LORA)),
                _const_spec((1, LANE)),
                _const_spec((1, LANE))]
    args = [x, mod3, wts["norm_w"], wts["w_lat"], wts["w_wide"], wts["q_norm_w"], wts["w_uq"], wts["qk_q_w"],
            wts["kv_norm_w"], wts["w_uk"], wts["w_uvt"], wts["qk_k_n"], wts["qk_k_r"]]
    if rope:
        in_specs += [pl.BlockSpec((tm, LANE), lambda b, i: (i, 0))] * 5
        args += list(rope_tabs)
    tok = lambda w: pl.BlockSpec((1, tm, w), lambda b, i: (b, i, 0))
    out_specs = [pl.BlockSpec((1, MLA_HEADS, tm, QK_PAD), lambda b, i: (b, 0, i, 0)),
                 pl.BlockSpec((1, MLA_HEADS, tm, QK_PAD), lambda b, i: (b, 0, i, 0)),
                 pl.BlockSpec((1, MLA_HEADS, V_AUG, tm), lambda b, i: (b, 0, 0, i)),
                 tok(MLA_W), tok(RET_QK), tok(RET_QK), tok(RET_W), tok(RET_W)]
    out_shape = [jax.ShapeDtypeStruct((B, MLA_HEADS, L, QK_PAD), BF16),
                 jax.ShapeDtypeStruct((B, MLA_HEADS, L, QK_PAD), BF16),
                 jax.ShapeDtypeStruct((B, MLA_HEADS, V_AUG, L), BF16),
                 jax.ShapeDtypeStruct((B, L, MLA_W), BF16),
                 jax.ShapeDtypeStruct((B, L, RET_QK), BF16),
                 jax.ShapeDtypeStruct((B, L, RET_QK), BF16),
                 jax.ShapeDtypeStruct((B, L, RET_W), BF16),
                 jax.ShapeDtypeStruct((B, L, RET_W), BF16)]
    if emit_cache:
        out_specs += [tok(KV_LORA), tok(ROPE_DIM)]
        out_shape += [jax.ShapeDtypeStruct((B, L, KV_LORA), F32),
                      jax.ShapeDtypeStruct((B, L, ROPE_DIM), F32)]
    return pl.pallas_call(
        functools.partial(_in_kernel, rope=rope, emit_cache=emit_cache),
        grid=(B, L // tm),
        in_specs=in_specs, out_specs=out_specs, out_shape=out_shape,
        compiler_params=pltpu.CompilerParams(dimension_semantics=("parallel", "parallel"),
                                             vmem_limit_bytes=VMEM_LIMIT),
        name="in_proj_rope" if rope else "in_proj",
    )(*args)


def _ctx_kernel(ckv_ref, kr_ref, wuk_ref, wuvt_ref, kkn_ref, kkr_ref, k_ref, vt_ref):
    _keys_values(ckv_ref[0], kr_ref[0], wuk_ref, wuvt_ref, kkn_ref[...], kkr_ref[...], None,
                 k_ref, vt_ref)


def _ctx_call(ckv, kr_pad, wts):
    B, P, _ = ckv.shape
    cs = lambda shape: pl.BlockSpec(shape, lambda b: (0,) * len(shape))
    return pl.pallas_call(
        _ctx_kernel,
        grid=(B,),
        in_specs=[pl.BlockSpec((1, P, KV_LORA), lambda b: (b, 0, 0)),
                  pl.BlockSpec((1, P, LANE), lambda b: (b, 0, 0)),
                  cs((KV_LORA, MLA_HEADS * MLA_NOPE)), cs((MLA_W, KV_LORA)),
                  cs((1, LANE)), cs((1, LANE))],
        out_specs=[pl.BlockSpec((1, MLA_HEADS, P, QK_PAD), lambda b: (b, 0, 0, 0)),
                   pl.BlockSpec((1, MLA_HEADS, V_AUG, P), lambda b: (b, 0, 0, 0))],
        out_shape=[jax.ShapeDtypeStruct((B, MLA_HEADS, P, QK_PAD), BF16),
                   jax.ShapeDtypeStruct((B, MLA_HEADS, V_AUG, P), BF16)],
        compiler_params=pltpu.CompilerParams(dimension_semantics=("parallel",),
                                             vmem_limit_bytes=VMEM_LIMIT),
        name="ctx_kv",
    )(ckv, kr_pad, wts["w_uk"], wts["w_uvt"], wts["qk_k_n"], wts["qk_k_r"])


def _attn_kernel(*refs, heads, **kw):
    def first_scores(hh):
        if hh < heads:
            _attn_head(hh, *refs, phase="first", **kw)

    first_scores(0)
    sums = [_attn_head(hh, *refs, phase="fast", hook=functools.partial(first_scores, hh + 1),
                       **kw) for hh in range(heads)]
    worst = functools.reduce(jnp.maximum, sums)

    @pl.when(jnp.logical_not(jnp.max(worst) < OVERFLOW_GUARD))
    def _():
        for hh in range(heads):
            _attn_head(hh, *refs, phase="robust", **kw)


def _attn_head(hh, *refs, tk, n_chunks, has_ctx, phase, hook=None):
    if has_ctx:
        q_ref, k_ref, vt_ref, kc_ref, vtc_ref, ga_ref, o_ref, s_scr = refs
    else:
        q_ref, k_ref, vt_ref, ga_ref, o_ref, s_scr = refs
    q = q_ref[0, hh]
    tq = q.shape[0]
    cols = slice(hh * MLA_V, (hh + 1) * MLA_V)

    def scores(kc):
        return lax.dot_general(kc, q, NT_DIMS, preferred_element_type=F32)

    def chunk(j):
        s0 = pl.multiple_of(j * tk, tk)
        return k_ref[0, hh, pl.ds(s0, tk), :], vt_ref[0, hh, :, pl.ds(s0, tk)]

    def finish(acc):
        out = (acc[:MLA_V] * (1.0 / acc[MLA_V:MLA_V + 1])).T
        o_ref[0, :, cols] = (out * ga_ref[0, :, cols].astype(F32)).astype(BF16)

    if phase == "first":
        s_scr[hh, 0] = scores(k_ref[0, hh, 0:tk, :])
        return None

    if phase == "robust":
        def step(kc, vtc, carry):
            m, acc = carry
            st = scores(kc)
            m_new = jnp.maximum(m, jnp.max(st, axis=0, keepdims=True))
            p = jnp.exp2(st - m_new).astype(BF16)
            return m_new, jnp.exp2(m - m_new) * acc + _dot(vtc, p)

        carry = (jnp.full((1, tq), -1e30, F32), jnp.zeros((V_AUG, tq), F32))
        carry = lax.fori_loop(0, n_chunks, lambda j, c: step(*chunk(j), c), carry)
        if has_ctx:
            carry = step(kc_ref[0, hh], vtc_ref[0, hh], carry)
        finish(carry[1])
        return None

    m_fix = jnp.max(s_scr[hh, 0], axis=0, keepdims=True)

    def weights(st):
        return jnp.exp2((st - m_fix).astype(BF16))

    acc = jnp.zeros((V_AUG, tq), F32)
    for j in range(n_chunks):
        if j + 1 < n_chunks:
            s_scr[hh, (j + 1) % 2] = scores(k_ref[0, hh, (j + 1) * tk:(j + 2) * tk, :])
        elif has_ctx:
            st_ctx = scores(kc_ref[0, hh])
        if j + 1 == n_chunks and not has_ctx:
            hook()
        acc = acc + _dot(vt_ref[0, hh, :, j * tk:(j + 1) * tk], weights(s_scr[hh, j % 2]))
    if has_ctx:
        hook()
        acc = acc + _dot(vtc_ref[0, hh], weights(st_ctx))
    finish(acc)
    return acc[MLA_V:MLA_V + 1]


def _attn_call(q, k, vt, ga, ctx, tq, tk, heads):
    B, H, L, _ = q.shape
    has_ctx = ctx is not None
    in_specs = [pl.BlockSpec((1, heads, tq, QK_PAD), lambda b, h, i: (b, h, i, 0)),
                pl.BlockSpec((1, heads, L, QK_PAD), lambda b, h, i: (b, h, 0, 0)),
                pl.BlockSpec((1, heads, V_AUG, L), lambda b, h, i: (b, h, 0, 0))]
    args = [q, k, vt]
    if has_ctx:
        kc, vtc = ctx
        P = kc.shape[2]
        in_specs += [pl.BlockSpec((1, heads, P, QK_PAD), lambda b, h, i: (b, h, 0, 0)),
                     pl.BlockSpec((1, heads, V_AUG, P), lambda b, h, i: (b, h, 0, 0))]
        args += [kc, vtc]
    in_specs.append(pl.BlockSpec((1, tq, heads * MLA_V), lambda b, h, i: (b, i, h)))
    args.append(ga)
    return pl.pallas_call(
        functools.partial(_attn_kernel, heads=heads, tk=tk, n_chunks=L // tk, has_ctx=has_ctx),
        grid=(B, H // heads, L // tq),
        in_specs=in_specs,
        out_specs=pl.BlockSpec((1, tq, heads * MLA_V), lambda b, h, i: (b, i, h)),
        out_shape=jax.ShapeDtypeStruct((B, L, MLA_W), BF16),
        scratch_shapes=[pltpu.VMEM((heads, 2, tk, tq), F32)],
        compiler_params=pltpu.CompilerParams(
            dimension_semantics=("parallel", "parallel", "parallel"),
            vmem_limit_bytes=VMEM_LIMIT),
        name="attn_ctx" if has_ctx else "attn",
    )(*args)


def _ret_kernel(*refs, heads, **kw):
    for hh in range(heads):
        _ret_head(hh, *refs, heads=heads, **kw)


def _ret_head(hh, *refs, heads, n_chunks, has_state, unroll):
    it = iter(refs)
    lgf_ref, lgb_ref = next(it), next(it)
    q_ref, k_ref, v_ref, gb_ref, gnw_ref = next(it), next(it), next(it), next(it), next(it)
    if has_state:
        s0f_ref, s0b_ref = next(it), next(it)
    out_ref = next(it)
    if not has_state:
        sf_ref, sb_ref = next(it), next(it)
    kv_scr, st_scr, o_scr = next(it), next(it), next(it)

    C = RET_CHUNK
    kcols = slice(hh * RET_DK, (hh + 1) * RET_DK)
    vcols = slice(hh * RET_DV, (hh + 1) * RET_DV)
    hd = pl.program_id(1) * heads + hh
    lgf = lgf_ref[hd]
    lgb = lgb_ref[hd]
    ri = lax.broadcasted_iota(jnp.int32, (C, C), 0).astype(F32)
    ci = lax.broadcasted_iota(jnp.int32, (C, C), 1).astype(F32)
    diff = ri - ci
    low = diff >= 0
    mask = (jnp.where(low, jnp.exp(jnp.where(low, diff, 0.0) * lgf), 0.0)
            + jnp.where(low, 0.0, jnp.exp(jnp.where(low, 0.0, -diff) * lgb)))
    qd_f = jnp.exp((ri + 1.0) * lgf)
    kd_f = jnp.exp((C - 1.0 - ri) * lgf)
    qd_b = jnp.exp((C - ri) * lgb)
    kd_b = jnp.exp(ri * lgb)
    cd_f = jnp.exp(jnp.full((RET_DK, RET_DV), C, F32) * lgf)
    cd_b = jnp.exp(jnp.full((RET_DK, RET_DV), C, F32) * lgb)
    gnw = gnw_ref[hh]

    def rows(n):
        return pl.ds(n * C if isinstance(n, int) else pl.multiple_of(n * C, C), C)

    def loop(lo, hi, body, carry):
        if hi - lo <= 2:
            for i in range(lo, hi):
                carry = body(i, carry)
            return carry
        return lax.fori_loop(lo, hi, body, carry)

    trips = n_chunks // unroll

    def summarise(i, _):
        for u in range(unroll):
            n = i * unroll + u
            sl = rows(n)
            kc = k_ref[0, sl, kcols].astype(F32)
            a = jnp.concatenate([kc * kd_f, kc * kd_b], axis=1)
            kv_scr[hh, n] = _dot(a.T.astype(BF16), v_ref[0, sl, vcols])
        return 0

    loop(0, trips, summarise, 0)

    if has_state:
        init = (s0f_ref[0, hh], s0b_ref[0, hh])
    else:
        init = (jnp.zeros((RET_DK, RET_DV), F32),) * 2

    def scan(i, carry):
        sf, sb = carry
        nb = n_chunks - 1 - i
        st_scr[hh, i, :RET_DK, :] = sf.astype(BF16)
        st_scr[hh, nb, RET_DK:, :] = sb.astype(BF16)
        return (sf * cd_f + kv_scr[hh, i, :RET_DK, :], sb * cd_b + kv_scr[hh, nb, RET_DK:, :])

    sf, sb = loop(0, n_chunks, scan, init)
    if not has_state:
        sf_ref[0, hh] = sf
        sb_ref[0, hh] = sb

    def products(i):
        ns = [i * unroll + u for u in range(unroll)]
        sls = [rows(n) for n in ns]
        qs = [q_ref[0, sl, kcols] for sl in sls]
        ss = [lax.dot_general(q, k_ref[0, sl, kcols], NT_DIMS, preferred_element_type=F32)
              for q, sl in zip(qs, sls)]
        for n, sl, q, s in zip(ns, sls, qs, ss):
            qf = q.astype(F32)
            qq = jnp.concatenate([qf * qd_f, qf * qd_b], axis=1).astype(BF16)
            o_scr[hh, sl, :] = (_dot((s * mask).astype(BF16), v_ref[0, sl, vcols])
                                + _dot(qq, st_scr[hh, n]))

    def normalise(i):
        for u in range(unroll):
            sl = rows(i * unroll + u)
            y = _rms(o_scr[hh, sl, :], gnw, RET_DV) * gb_ref[0, sl, vcols].astype(F32)
            out_ref[0, sl, vcols] = y.astype(BF16)

    def both(i, _):
        normalise(i - 1)
        products(i)
        return 0

    products(0)
    loop(1, trips, both, 0)
    normalise(trips - 1)


def _ret_call(lgf, lgb, rq, rk, rv, gb, gnw, states, heads):
    B, L, _ = rq.shape
    n_chunks = L // RET_CHUNK
    has_state = states is not None
    smem = pl.BlockSpec(memory_space=pltpu.SMEM)
    st_spec = pl.BlockSpec((1, heads, RET_DK, RET_DV), lambda b, h: (b, h, 0, 0))
    in_specs = [smem, smem,
                pl.BlockSpec((1, L, heads * RET_DK), lambda b, h: (b, 0, h)),
                pl.BlockSpec((1, L, heads * RET_DK), lambda b, h: (b, 0, h)),
                pl.BlockSpec((1, L, heads * RET_DV), lambda b, h: (b, 0, h)),
                pl.BlockSpec((1, L, heads * RET_DV), lambda b, h: (b, 0, h)),
                pl.BlockSpec((heads, 1, RET_DV), lambda b, h: (h, 0, 0))]
    args = [lgf, lgb, rq, rk, rv, gb, gnw]
    out_specs = [pl.BlockSpec((1, L, heads * RET_DV), lambda b, h: (b, 0, h))]
    out_shape = [jax.ShapeDtypeStruct((B, L, RET_W), BF16)]
    if has_state:
        in_specs += [st_spec, st_spec]
        args += list(states)
    else:
        out_specs += [st_spec, st_spec]
        out_shape += [jax.ShapeDtypeStruct((B, RET_HEADS, RET_DK, RET_DV), F32)] * 2
    return pl.pallas_call(
        functools.partial(_ret_kernel, heads=heads, n_chunks=n_chunks, has_state=has_state,
                          unroll=math.gcd(n_chunks, 4)),
        grid=(B, RET_HEADS // heads),
        in_specs=in_specs, out_specs=out_specs, out_shape=out_shape,
        scratch_shapes=[pltpu.VMEM((heads, n_chunks, 2 * RET_DK, RET_DV), F32),
                        pltpu.VMEM((heads, n_chunks, 2 * RET_DK, RET_DV), BF16),
                        pltpu.VMEM((heads, L, RET_DV), F32)],
        compiler_params=pltpu.CompilerParams(dimension_semantics=("parallel", "parallel"),
                                             vmem_limit_bytes=VMEM_LIMIT),
        name="ret_state" if has_state else "ret",
    )(*args)


def _out_kernel(ma_ref, mb_ref, w_ref, x_ref, mod_ref, o_ref):
    acc = _dot(ma_ref[0], w_ref[:MLA_W, :]) + _dot(mb_ref[0], w_ref[MLA_W:, :])
    o_ref[0] = x_ref[0] + mod_ref[0, 2:3, :] * acc


def _out_call(mix_a, mix_b, w_out, x, mod3, ctx_mod, tm):
    B, L, _ = x.shape
    mod_map = (lambda b, i: (MOD_ROWS - 1, 0, 0)) if ctx_mod else (lambda b, i: (b, 0, 0))
    return pl.pallas_call(
        _out_kernel,
        grid=(B, L // tm),
        in_specs=[pl.BlockSpec((1, tm, MLA_W), lambda b, i: (b, i, 0)),
                  pl.BlockSpec((1, tm, RET_W), lambda b, i: (b, i, 0)),
                  _const_spec((D_MODEL, D_MODEL)),
                  pl.BlockSpec((1, tm, D_MODEL), lambda b, i: (b, i, 0)),
                  pl.BlockSpec((1, 3, D_MODEL), mod_map)],
        out_specs=pl.BlockSpec((1, tm, D_MODEL), lambda b, i: (b, i, 0)),
        out_shape=jax.ShapeDtypeStruct((B, L, D_MODEL), F32),
        compiler_params=pltpu.CompilerParams(dimension_semantics=("parallel", "parallel"),
                                             vmem_limit_bytes=VMEM_LIMIT),
        name="out_proj",
    )(mix_a, mix_b, w_out, x, mod3)


def _rope_angles(pos, dim):
    half = dim // 2
    freqs = ROPE_BASE ** (-jnp.arange(half, dtype=F32) / half)
    return pos.astype(F32)[:, None] * freqs[None, :]


def _rope_tables(L):
    rows = L // GRID_W
    ar = _rope_angles(jnp.arange(rows), ROPE_DIM // 2)
    ac = _rope_angles(jnp.arange(GRID_W), ROPE_DIM // 2)
    cr, sr, cc, sc = jnp.cos(ar), jnp.sin(ar), jnp.cos(ac), jnp.sin(ac)
    zr, zc = jnp.zeros_like(sr), jnp.zeros_like(sc)

    def grid_table(per_row, per_col):
        w = per_row.shape[-1]
        pad = jnp.zeros((rows, GRID_W, LANE - 2 * w), F32)
        t = jnp.concatenate([jnp.broadcast_to(per_row[:, None, :], (rows, GRID_W, w)),
                             jnp.broadcast_to(per_col[None, :, :], (rows, GRID_W, w)), pad],
                            axis=-1)
        return t.reshape(L, LANE)

    cat = lambda a, b: jnp.concatenate([a, b], axis=-1)
    mc = grid_table(cat(cr, cr), cat(cc, cc))
    msa = grid_table(cat(-sr, zr), cat(-sc, zc))
    msb = grid_table(cat(zr, sr), cat(zc, sc))

    hi = _rope_angles(jnp.arange(rows) * GRID_W, RET_DK)[:, None, :]
    lo = _rope_angles(jnp.arange(GRID_W), RET_DK)[None, :, :]
    cos_t = jnp.cos(hi) * jnp.cos(lo) - jnp.sin(hi) * jnp.sin(lo)
    sin_t = jnp.sin(hi) * jnp.cos(lo) + jnp.cos(hi) * jnp.sin(lo)
    rc = cat(cos_t, cos_t).reshape(L, LANE)
    rs = cat(-sin_t, sin_t).reshape(L, LANE)
    return mc, msa, msb, rc, rs


def _prep_weights(l, norm_w, w_in, q_norm_w, w_uq, kv_norm_w, w_uk, w_uv, qk_q_w, qk_k_w):
    wi = w_in[l]
    w_lat = jnp.pad(wi[:, :LAT_SRC_COLS], ((0, 0), (0, LAT_COLS - LAT_SRC_COLS))).astype(BF16)
    w_wide = wi[:, LAT_SRC_COLS:].astype(BF16)
    wq = w_uq[l].reshape(Q_LORA, MLA_HEADS, QK_HEAD)
    wq = jnp.pad(wq, ((0, 0), (0, 0), (0, QK_PAD - QK_HEAD))).reshape(Q_LORA, MLA_HEADS * QK_PAD)
    qscale = (QK_HEAD ** -0.5) * math.log2(math.e)
    qkq = jnp.pad(qk_q_w[l] * qscale, (0, QK_PAD - QK_HEAD))[None, :]
    kk = qk_k_w[l]
    return {
        "norm_w": norm_w[l][None, :],
        "w_lat": w_lat,
        "w_wide": w_wide,
        "q_norm_w": q_norm_w[l][None, :],
        "w_uq": wq.astype(BF16),
        "qk_q_w": qkq,
        "kv_norm_w": kv_norm_w[l][None, :],
        "w_uk": w_uk[l].astype(BF16),
        "w_uvt": w_uv[l].T.astype(BF16),
        "qk_k_n": kk[None, :MLA_NOPE],
        "qk_k_r": jnp.pad(kk[MLA_NOPE:], (0, LANE - ROPE_DIM))[None, :],
    }


def _tiles(L):
    tm = min(L, 256)
    tq = min(L, 1024)
    tk = min(L, 256)
    heads = MLA_HEADS if L <= 512 else 2
    ret_heads = RET_HEADS if L <= 512 else 1
    return tm, tq, tk, heads, ret_heads


def _layer(x, mod3, ctx_mod, wts, w_out, lgf, lgb, gnw, rope_tabs, ctx, states):
    tm, tq, tk, heads, ret_heads = _tiles(x.shape[1])
    outs = _in_call(x, mod3, ctx_mod, wts, rope_tabs, tm)
    q, k, vt, ga, rq, rk, rv, gb = outs[:8]
    mix_a = _attn_call(q, k, vt, ga, ctx, tq, tk, heads)
    ret = _ret_call(lgf, lgb, rq, rk, rv, gb, gnw, states, ret_heads)
    y = _out_call(mix_a, ret[0], w_out, x, mod3, ctx_mod, tm)
    return y, outs[8:], ret[1:]


def kernel(x_prompt, x_sample, c, cache_mla_ckv, cache_mla_krope, state_ret_fwd, state_ret_bwd,
           c_ctx, norm_w, w_mod, b_mod, w_in, mla_q_norm_w, mla_w_uq, mla_kv_norm_w, mla_w_uk,
           mla_w_uv, mla_qk_q_w, mla_qk_k_w, ret_log_decay_fwd, ret_log_decay_bwd, ret_gn_w, w_out):
    depth = w_in.shape[0]
    dec_b = x_sample.shape[0]
    assert dec_b < MOD_ROWS
    cvec = jnp.zeros((MOD_ROWS, D_MODEL), F32).at[:dec_b].set(c).at[MOD_ROWS - 1].set(c_ctx)
    rope_tabs = _rope_tables(x_sample.shape[1])

    xp, xs = x_prompt, x_sample
    ckv_l, kr_l, sf_l, sb_l = [], [], [], []
    for l in range(depth):
        wts = _prep_weights(l, norm_w, w_in, mla_q_norm_w, mla_w_uq, mla_kv_norm_w, mla_w_uk,
                            mla_w_uv, mla_qk_q_w, mla_qk_k_w)
        w_out_b = w_out[l].astype(BF16)
        lgf = -jnp.exp(ret_log_decay_fwd[l].astype(F32))
        lgb = -jnp.exp(ret_log_decay_bwd[l].astype(F32))
        gnw = ret_gn_w[l][:, None, :]
        mod3 = _mod_call(cvec, w_mod[l], b_mod[l][None, :]).reshape(MOD_ROWS, 3, D_MODEL)

        xp, (ckv, kr), (sf, sb) = _layer(xp, mod3, True, wts, w_out_b, lgf, lgb, gnw,
                                         None, None, None)
        ckv_l.append(ckv)
        kr_l.append(kr)
        sf_l.append(sf)
        sb_l.append(sb)

        kr_pad = jnp.pad(cache_mla_krope[:, l], ((0, 0), (0, 0), (0, LANE - ROPE_DIM)))
        ctx = _ctx_call(cache_mla_ckv[:, l], kr_pad, wts)
        xs, _, _ = _layer(xs, mod3, False, wts, w_out_b, lgf, lgb, gnw, rope_tabs, ctx,
                          (state_ret_fwd[:, l], state_ret_bwd[:, l]))

    return (xp, xs, jnp.stack(ckv_l, axis=1), jnp.stack(kr_l, axis=1),
            jnp.stack(sf_l, axis=1), jnp.stack(sb_l, axis=1))
```

```python
import functools
import math

import jax
import jax.numpy as jnp
from jax import lax
from jax.experimental import pallas as pl
from jax.experimental.pallas import tpu as pltpu

F32 = jnp.float32
BF16 = jnp.bfloat16

D_MODEL = 2048
GRID_W = 64
MLA_W = 1024
RET_W = 1024
MLA_NOPE = 128
ROPE_DIM = 64
QK_HEAD = MLA_NOPE + ROPE_DIM
MLA_V = 128
MLA_HEADS = 8
Q_LORA = 384
KV_LORA = 256
RET_DV = 256
RET_DK = 128
RET_HEADS = 4
RET_QK = RET_HEADS * RET_DK
RET_CHUNK = 128
ROPE_BASE = 10000.0
EPS = 1e-6

LANE = 128
QK_PAD = 2 * LANE
V_AUG = MLA_V + 16
OVERFLOW_GUARD = 2.0 ** 100
MOD_ROWS = 16
VMEM_LIMIT = 56 * 1024 * 1024

OFF_QLAT = 0
OFF_CKV = OFF_QLAT + Q_LORA
OFF_KROPE = OFF_CKV + KV_LORA
LAT_COLS = OFF_KROPE + LANE
LAT_SRC_COLS = OFF_KROPE + ROPE_DIM
OFF_GA = 0
OFF_RQ = OFF_GA + MLA_W
OFF_RK = OFF_RQ + RET_QK
OFF_RV = OFF_RK + RET_QK
OFF_GB = OFF_RV + RET_W
WIDE_COLS = OFF_GB + RET_W

NT_DIMS = (((1,), (1,)), ((), ()))


def _silu(x):
    return x * (1.0 / (1.0 + jnp.exp(-x)))


def _rms(x, w, n):
    ms = jnp.sum(x * x, axis=-1, keepdims=True) * (1.0 / n)
    return x * lax.rsqrt(ms + EPS) * w


def _dot(a, b):
    return jnp.dot(a, b, preferred_element_type=F32)


def _mod_kernel(c_ref, w_ref, b_ref, o_ref):
    s = _silu(c_ref[...]).astype(BF16)
    o_ref[...] = _dot(s, w_ref[...].astype(BF16)) + b_ref[...]


def _mod_call(cvec, w_mod, b_mod):
    tn = 512
    n = w_mod.shape[1]
    return pl.pallas_call(
        _mod_kernel,
        grid=(n // tn,),
        in_specs=[pl.BlockSpec((MOD_ROWS, D_MODEL), lambda j: (0, 0)),
                  pl.BlockSpec((D_MODEL, tn), lambda j: (0, j)),
                  pl.BlockSpec((1, tn), lambda j: (0, j))],
        out_specs=pl.BlockSpec((MOD_ROWS, tn), lambda j: (0, j)),
        out_shape=jax.ShapeDtypeStruct((MOD_ROWS, n), F32),
        compiler_params=pltpu.CompilerParams(dimension_semantics=("arbitrary",),
                                             vmem_limit_bytes=VMEM_LIMIT),
        name="mod",
    )(cvec, w_mod, b_mod)


def _mla_rope(r, c, sa, sb):
    return r * c + pltpu.roll(r, LANE - 16, 1) * sa + pltpu.roll(r, 16, 1) * sb


def _keys_values(ckv, kr, w_uk_ref, w_uvt_ref, kkn, kkr, rope, k_ref, vt_ref):
    ckv_b = ckv.astype(BF16)
    kn_all = _dot(ckv_b, w_uk_ref[...])
    krw = kr * kkr
    if rope is not None:
        krw = _mla_rope(krw, *rope)
    kr_ss = jnp.sum(kr * kr, axis=-1, keepdims=True)
    for h in range(MLA_HEADS):
        kn = kn_all[:, h * MLA_NOPE:(h + 1) * MLA_NOPE]
        ss = jnp.sum(kn * kn, axis=-1, keepdims=True) + kr_ss
        inv = lax.rsqrt(ss * (1.0 / QK_HEAD) + EPS)
        k_ref[0, h, :, :LANE] = (kn * inv * kkn).astype(BF16)
        k_ref[0, h, :, LANE:] = (krw * inv).astype(BF16)
    vt = lax.dot_general(w_uvt_ref[...], ckv_b, NT_DIMS, preferred_element_type=F32)
    ones_row = jnp.where(lax.broadcasted_iota(jnp.int32, (V_AUG - MLA_V, vt.shape[1]), 0) == 0,
                         1.0, 0.0).astype(BF16)
    for h in range(MLA_HEADS):
        vt_ref[0, h, :MLA_V, :] = vt[h * MLA_V:(h + 1) * MLA_V].astype(BF16)
        vt_ref[0, h, MLA_V:, :] = ones_row


def _in_kernel(*refs, rope, emit_cache):
    it = iter(refs)
    x_ref, mod_ref, nw_ref, wl_ref, w_ref = next(it), next(it), next(it), next(it), next(it)
    qnw_ref, wuq_ref, qkq_ref = next(it), next(it), next(it)
    kvw_ref, wuk_ref, wuvt_ref, kkn_ref, kkr_ref = next(it), next(it), next(it), next(it), next(it)
    if rope:
        mc_ref, msa_ref, msb_ref, rc_ref, rs_ref = next(it), next(it), next(it), next(it), next(it)
    q_ref, k_ref, vt_ref, ga_ref = next(it), next(it), next(it), next(it)
    rq_ref, rk_ref, rv_ref, gb_ref = next(it), next(it), next(it), next(it)
    if emit_cache:
        ckv_ref, kr_ref = next(it), next(it)

    x = x_ref[0]
    shift = mod_ref[0, 0:1, :]
    scale = mod_ref[0, 1:2, :]
    h = (_rms(x, nw_ref[...], D_MODEL) * (1.0 + scale) + shift).astype(BF16)

    mla_rope = (mc_ref[...], msa_ref[...], msb_ref[...]) if rope else None

    a = _dot(h, wl_ref[...])
    qn = _rms(a[:, OFF_QLAT:OFF_CKV], qnw_ref[...], Q_LORA).astype(BF16)
    ckv = _rms(a[:, OFF_CKV:OFF_KROPE], kvw_ref[...], KV_LORA)
    kr = a[:, OFF_KROPE:LAT_COLS]
    if emit_cache:
        ckv_ref[0] = ckv
        kr_ref[0] = kr[:, :ROPE_DIM]

    q_all = _dot(qn, wuq_ref[...])
    qkq = qkq_ref[...]
    for hh in range(MLA_HEADS):
        qh = q_all[:, hh * QK_PAD:(hh + 1) * QK_PAD]
        inv = lax.rsqrt(jnp.sum(qh * qh, axis=-1, keepdims=True) * (1.0 / QK_HEAD) + EPS)
        qh = qh * inv * qkq
        r = qh[:, LANE:]
        if rope:
            r = _mla_rope(r, *mla_rope)
        q_ref[0, hh, :, :LANE] = qh[:, :LANE].astype(BF16)
        q_ref[0, hh, :, LANE:] = r.astype(BF16)

    _keys_values(ckv, kr, wuk_ref, wuvt_ref, kkn_ref[...], kkr_ref[...], mla_rope, k_ref, vt_ref)

    ga_ref[0] = _silu(_dot(h, w_ref[:, OFF_GA:OFF_RQ])).astype(BF16)
    gb_ref[0] = _silu(_dot(h, w_ref[:, OFF_GB:WIDE_COLS])).astype(BF16)

    rqk = _dot(h, w_ref[:, OFF_RQ:OFF_RV])
    for hh in range(2 * RET_HEADS):
        t = rqk[:, hh * RET_DK:(hh + 1) * RET_DK]
        if hh >= RET_HEADS:
            t = t * (RET_DK ** -0.5)
        if rope:
            t = t * rc_ref[...] + pltpu.roll(t, RET_DK // 2, 1) * rs_ref[...]
        if hh < RET_HEADS:
            rq_ref[0, :, hh * RET_DK:(hh + 1) * RET_DK] = t.astype(BF16)
        else:
            g = hh - RET_HEADS
            rk_ref[0, :, g * RET_DK:(g + 1) * RET_DK] = t.astype(BF16)
    rv_ref[0] = _dot(h, w_ref[:, OFF_RV:OFF_GB]).astype(BF16)


def _const_spec(shape):
    return pl.BlockSpec(shape, lambda b, i: (0,) * len(shape), pipeline_mode=pl.Buffered(1))


def _in_call(x, mod3, ctx_mod, wts, rope_tabs, tm):
    B, L, _ = x.shape
    rope = rope_tabs is not None
    emit_cache = not rope
    mod_map = (lambda b, i: (MOD_ROWS - 1, 0, 0)) if ctx_mod else (lambda b, i: (b, 0, 0))
    in_specs = [pl.BlockSpec((1, tm, D_MODEL), lambda b, i: (b, i, 0)),
                pl.BlockSpec((1, 3, D_MODEL), mod_map),
                _const_spec((1, D_MODEL)),
                _const_spec((D_MODEL, LAT_COLS)),
                _const_spec((D_MODEL, WIDE_COLS)),
                _const_spec((1, Q_LORA)),
                _const_spec((Q_LORA, MLA_HEADS * QK_PAD)),
                _const_spec((1, QK_PAD)),
                _const_spec((1, KV_LORA)),
                _const_spec((KV_LORA, MLA_HEADS * MLA_NOPE)),
                _const_spec((MLA_W, KV_LORA)),
                _const_spec((1, LANE)),
                _const_spec((1, LANE))]
    args = [x, mod3, wts["norm_w"], wts["w_lat"], wts["w_wide"], wts["q_norm_w"], wts["w_uq"], wts["qk_q_w"],
            wts["kv_norm_w"], wts["w_uk"], wts["w_uvt"], wts["qk_k_n"], wts["qk_k_r"]]
    if rope:
        in_specs += [pl.BlockSpec((tm, LANE), lambda b, i: (i, 0))] * 5
        args += list(rope_tabs)
    tok = lambda w: pl.BlockSpec((1, tm, w), lambda b, i: (b, i, 0))
    out_specs = [pl.BlockSpec((1, MLA_HEADS, tm, QK_PAD), lambda b, i: (b, 0, i, 0)),
                 pl.BlockSpec((1, MLA_HEADS, tm, QK_PAD), lambda b, i: (b, 0, i, 0)),
                 pl.BlockSpec((1, MLA_HEADS, V_AUG, tm), lambda b, i: (b, 0, 0, i)),
                 tok(MLA_W), tok(RET_QK), tok(RET_QK), tok(RET_W), tok(RET_W)]
    out_shape = [jax.ShapeDtypeStruct((B, MLA_HEADS, L, QK_PAD), BF16),
                 jax.ShapeDtypeStruct((B, MLA_HEADS, L, QK_PAD), BF16),
                 jax.ShapeDtypeStruct((B, MLA_HEADS, V_AUG, L), BF16),
                 jax.ShapeDtypeStruct((B, L, MLA_W), BF16),
                 jax.ShapeDtypeStruct((B, L, RET_QK), BF16),
                 jax.ShapeDtypeStruct((B, L, RET_QK), BF16),
                 jax.ShapeDtypeStruct((B, L, RET_W), BF16),
                 jax.ShapeDtypeStruct((B, L, RET_W), BF16)]
    if emit_cache:
        out_specs += [tok(KV_LORA), tok(ROPE_DIM)]
        out_shape += [jax.ShapeDtypeStruct((B, L, KV_LORA), F32),
                      jax.ShapeDtypeStruct((B, L, ROPE_DIM), F32)]
    return pl.pallas_call(
        functools.partial(_in_kernel, rope=rope, emit_cache=emit_cache),
        grid=(B, L // tm),
        in_specs=in_specs, out_specs=out_specs, out_shape=out_shape,
        compiler_params=pltpu.CompilerParams(dimension_semantics=("parallel", "parallel"),
                                             vmem_limit_bytes=VMEM_LIMIT),
        name="in_proj_rope" if rope else "in_proj",
    )(*args)


def _ctx_kernel(ckv_ref, kr_ref, wuk_ref, wuvt_ref, kkn_ref, kkr_ref, k_ref, vt_ref):
    _keys_values(ckv_ref[0], kr_ref[0], wuk_ref, wuvt_ref, kkn_ref[...], kkr_ref[...], None,
                 k_ref, vt_ref)


def _ctx_call(ckv, kr_pad, wts):
    B, P, _ = ckv.shape
    cs = lambda shape: pl.BlockSpec(shape, lambda b: (0,) * len(shape))
    return pl.pallas_call(
        _ctx_kernel,
        grid=(B,),
        in_specs=[pl.BlockSpec((1, P, KV_LORA), lambda b: (b, 0, 0)),
                  pl.BlockSpec((1, P, LANE), lambda b: (b, 0, 0)),
                  cs((KV_LORA, MLA_HEADS * MLA_NOPE)), cs((MLA_W, KV_LORA)),
                  cs((1, LANE)), cs((1, LANE))],
        out_specs=[pl.BlockSpec((1, MLA_HEADS, P, QK_PAD), lambda b: (b, 0, 0, 0)),
                   pl.BlockSpec((1, MLA_HEADS, V_AUG, P), lambda b: (b, 0, 0, 0))],
        out_shape=[jax.ShapeDtypeStruct((B, MLA_HEADS, P, QK_PAD), BF16),
                   jax.ShapeDtypeStruct((B, MLA_HEADS, V_AUG, P), BF16)],
        compiler_params=pltpu.CompilerParams(dimension_semantics=("parallel",),
                                             vmem_limit_bytes=VMEM_LIMIT),
        name="ctx_kv",
    )(ckv, kr_pad, wts["w_uk"], wts["w_uvt"], wts["qk_k_n"], wts["qk_k_r"])


def _attn_kernel(*refs, heads, **kw):
    def first_scores(hh):
        if hh < heads:
            _attn_head(hh, *refs, phase="first", **kw)

    first_scores(0)
    sums = [_attn_head(hh, *refs, phase="fast", hook=functools.partial(first_scores, hh + 1),
                       **kw) for hh in range(heads)]
    worst = functools.reduce(jnp.maximum, sums)

    @pl.when(jnp.logical_not(jnp.max(worst) < OVERFLOW_GUARD))
    def _():
        for hh in range(heads):
            _attn_head(hh, *refs, phase="robust", **kw)


def _attn_head(hh, *refs, tk, n_chunks, has_ctx, phase, hook=None):
    if has_ctx:
        q_ref, k_ref, vt_ref, kc_ref, vtc_ref, ga_ref, o_ref, s_scr = refs
    else:
        q_ref, k_ref, vt_ref, ga_ref, o_ref, s_scr = refs
    q = q_ref[0, hh]
    tq = q.shape[0]
    cols = slice(hh * MLA_V, (hh + 1) * MLA_V)

    def scores(kc):
        return lax.dot_general(kc, q, NT_DIMS, preferred_element_type=F32)

    def chunk(j):
        s0 = pl.multiple_of(j * tk, tk)
        return k_ref[0, hh, pl.ds(s0, tk), :], vt_ref[0, hh, :, pl.ds(s0, tk)]

    def finish(acc):
        out = (acc[:MLA_V] * (1.0 / acc[MLA_V:MLA_V + 1])).T
        o_ref[0, :, cols] = (out * ga_ref[0, :, cols].astype(F32)).astype(BF16)

    if phase == "first":
        s_scr[hh, 0] = scores(k_ref[0, hh, 0:tk, :])
        return None

    if phase == "robust":
        def step(kc, vtc, carry):
            m, acc = carry
            st = scores(kc)
            m_new = jnp.maximum(m, jnp.max(st, axis=0, keepdims=True))
            p = jnp.exp2(st - m_new).astype(BF16)
            return m_new, jnp.exp2(m - m_new) * acc + _dot(vtc, p)

        carry = (jnp.full((1, tq), -1e30, F32), jnp.zeros((V_AUG, tq), F32))
        carry = lax.fori_loop(0, n_chunks, lambda j, c: step(*chunk(j), c), carry)
        if has_ctx:
            carry = step(kc_ref[0, hh], vtc_ref[0, hh], carry)
        finish(carry[1])
        return None

    m_fix = jnp.max(s_scr[hh, 0], axis=0, keepdims=True)

    def weights(st):
        return jnp.exp2((st - m_fix).astype(BF16))

    acc = jnp.zeros((V_AUG, tq), F32)
    for j in range(n_chunks):
        if j + 1 < n_chunks:
            s_scr[hh, (j + 1) % 2] = scores(k_ref[0, hh, (j + 1) * tk:(j + 2) * tk, :])
        elif has_ctx:
            st_ctx = scores(kc_ref[0, hh])
        if j + 1 == n_chunks and not has_ctx:
            hook()
        acc = acc + _dot(vt_ref[0, hh, :, j * tk:(j + 1) * tk], weights(s_scr[hh, j % 2]))
    if has_ctx:
        hook()
        acc = acc + _dot(vtc_ref[0, hh], weights(st_ctx))
    finish(acc)
    return acc[MLA_V:MLA_V + 1]


def _attn_call(q, k, vt, ga, ctx, tq, tk, heads):
    B, H, L, _ = q.shape
    has_ctx = ctx is not None
    in_specs = [pl.BlockSpec((1, heads, tq, QK_PAD), lambda b, h, i: (b, h, i, 0)),
                pl.BlockSpec((1, heads, L, QK_PAD), lambda b, h, i: (b, h, 0, 0)),
                pl.BlockSpec((1, heads, V_AUG, L), lambda b, h, i: (b, h, 0, 0))]
    args = [q, k, vt]
    if has_ctx:
        kc, vtc = ctx
        P = kc.shape[2]
        in_specs += [pl.BlockSpec((1, heads, P, QK_PAD), lambda b, h, i: (b, h, 0, 0)),
                     pl.BlockSpec((1, heads, V_AUG, P), lambda b, h, i: (b, h, 0, 0))]
        args += [kc, vtc]
    in_specs.append(pl.BlockSpec((1, tq, heads * MLA_V), lambda b, h, i: (b, i, h)))
    args.append(ga)
    return pl.pallas_call(
        functools.partial(_attn_kernel, heads=heads, tk=tk, n_chunks=L // tk, has_ctx=has_ctx),
        grid=(B, H // heads, L // tq),
        in_specs=in_specs,
        out_specs=pl.BlockSpec((1, tq, heads * MLA_V), lambda b, h, i: (b, i, h)),
        out_shape=jax.ShapeDtypeStruct((B, L, MLA_W), BF16),
        scratch_shapes=[pltpu.VMEM((heads, 2, tk, tq), F32)],
        compiler_params=pltpu.CompilerParams(
            dimension_semantics=("parallel", "parallel", "parallel"),
            vmem_limit_bytes=VMEM_LIMIT),
        name="attn_ctx" if has_ctx else "attn",
    )(*args)


def _ret_kernel(*refs, heads, **kw):
    for hh in range(heads):
        _ret_head(hh, *refs, heads=heads, **kw)


def _ret_head(hh, *refs, heads, n_chunks, has_state, unroll):
    it = iter(refs)
    lgf_ref, lgb_ref = next(it), next(it)
    q_ref, k_ref, v_ref, gb_ref, gnw_ref = next(it), next(it), next(it), next(it), next(it)
    if has_state:
        s0f_ref, s0b_ref = next(it), next(it)
    out_ref = next(it)
    if not has_state:
        sf_ref, sb_ref = next(it), next(it)
    kv_scr, st_scr, o_scr, at_scr = next(it), next(it), next(it), next(it)

    C = RET_CHUNK
    kcols = slice(hh * RET_DK, (hh + 1) * RET_DK)
    vcols = slice(hh * RET_DV, (hh + 1) * RET_DV)
    hd = pl.program_id(1) * heads + hh
    lgf = lgf_ref[hd]
    lgb = lgb_ref[hd]
    ri = lax.broadcasted_iota(jnp.int32, (C, C), 0).astype(F32)
    ci = lax.broadcasted_iota(jnp.int32, (C, C), 1).astype(F32)
    diff = ri - ci
    low = diff >= 0
    mask = (jnp.where(low, jnp.exp(jnp.where(low, diff, 0.0) * lgf), 0.0)
            + jnp.where(low, 0.0, jnp.exp(jnp.where(low, 0.0, -diff) * lgb)))
    qd_f = jnp.exp((ri + 1.0) * lgf)
    kd_f = jnp.exp((C - 1.0 - ri) * lgf)
    qd_b = jnp.exp((C - ri) * lgb)
    kd_b = jnp.exp(ri * lgb)
    cd_f = jnp.exp(jnp.full((RET_DK, RET_DV), C, F32) * lgf)
    cd_b = jnp.exp(jnp.full((RET_DK, RET_DV), C, F32) * lgb)
    gnw = gnw_ref[hh]

    def rows(n):
        return pl.ds(n * C if isinstance(n, int) else pl.multiple_of(n * C, C), C)

    def loop(lo, hi, body, carry):
        if hi - lo <= 2:
            for i in range(lo, hi):
                carry = body(i, carry)
            return carry
        return lax.fori_loop(lo, hi, body, carry)

    trips = n_chunks // unroll

    def decayed_keys(i):
        for u in range(unroll):
            n = i * unroll + u
            kc = k_ref[0, rows(n), kcols].astype(F32)
            a = jnp.concatenate([kc * kd_f, kc * kd_b], axis=1)
            at_scr[hh, n] = a.T.astype(BF16)

    def increments(i):
        for u in range(unroll):
            n = i * unroll + u
            kv_scr[hh, n] = _dot(at_scr[hh, n], v_ref[0, rows(n), vcols])

    def summarise(i, _):
        increments(i - 1)
        decayed_keys(i)
        return 0

    decayed_keys(0)
    loop(1, trips, summarise, 0)
    increments(trips - 1)

    if has_state:
        init = (s0f_ref[0, hh], s0b_ref[0, hh])
    else:
        init = (jnp.zeros((RET_DK, RET_DV), F32),) * 2

    def scan(i, carry):
        sf, sb = carry
        nb = n_chunks - 1 - i
        st_scr[hh, i, :RET_DK, :] = sf.astype(BF16)
        st_scr[hh, nb, RET_DK:, :] = sb.astype(BF16)
        return (sf * cd_f + kv_scr[hh, i, :RET_DK, :], sb * cd_b + kv_scr[hh, nb, RET_DK:, :])

    sf, sb = loop(0, n_chunks, scan, init)
    if not has_state:
        sf_ref[0, hh] = sf
        sb_ref[0, hh] = sb

    def products(i):
        ns = [i * unroll + u for u in range(unroll)]
        sls = [rows(n) for n in ns]
        qs = [q_ref[0, sl, kcols] for sl in sls]
        ss = [lax.dot_general(q, k_ref[0, sl, kcols], NT_DIMS, preferred_element_type=F32)
              for q, sl in zip(qs, sls)]
        for n, sl, q, s in zip(ns, sls, qs, ss):
            qf = q.astype(F32)
            qq = jnp.concatenate([qf * qd_f, qf * qd_b], axis=1).astype(BF16)
            o_scr[hh, sl, :] = (_dot((s * mask).astype(BF16), v_ref[0, sl, vcols])
                                + _dot(qq, st_scr[hh, n]))

    def normalise(i):
        for u in range(unroll):
            sl = rows(i * unroll + u)
            y = _rms(o_scr[hh, sl, :], gnw, RET_DV) * gb_ref[0, sl, vcols].astype(F32)
            out_ref[0, sl, vcols] = y.astype(BF16)

    def both(i, _):
        normalise(i - 1)
        products(i)
        return 0

    products(0)
    loop(1, trips, both, 0)
    normalise(trips - 1)


def _ret_call(lgf, lgb, rq, rk, rv, gb, gnw, states, heads):
    B, L, _ = rq.shape
    n_chunks = L // RET_CHUNK
    has_state = states is not None
    smem = pl.BlockSpec(memory_space=pltpu.SMEM)
    st_spec = pl.BlockSpec((1, heads, RET_DK, RET_DV), lambda b, h: (b, h, 0, 0))
    in_specs = [smem, smem,
                pl.BlockSpec((1, L, heads * RET_DK), lambda b, h: (b, 0, h)),
                pl.BlockSpec((1, L, heads * RET_DK), lambda b, h: (b, 0, h)),
                pl.BlockSpec((1, L, heads * RET_DV), lambda b, h: (b, 0, h)),
                pl.BlockSpec((1, L, heads * RET_DV), lambda b, h: (b, 0, h)),
                pl.BlockSpec((heads, 1, RET_DV), lambda b, h: (h, 0, 0))]
    args = [lgf, lgb, rq, rk, rv, gb, gnw]
    out_specs = [pl.BlockSpec((1, L, heads * RET_DV), lambda b, h: (b, 0, h))]
    out_shape = [jax.ShapeDtypeStruct((B, L, RET_W), BF16)]
    if has_state:
        in_specs += [st_spec, st_spec]
        args += list(states)
    else:
        out_specs += [st_spec, st_spec]
        out_shape += [jax.ShapeDtypeStruct((B, RET_HEADS, RET_DK, RET_DV), F32)] * 2
    return pl.pallas_call(
        functools.partial(_ret_kernel, heads=heads, n_chunks=n_chunks, has_state=has_state,
                          unroll=math.gcd(n_chunks, 4)),
        grid=(B, RET_HEADS // heads),
        in_specs=in_specs, out_specs=out_specs, out_shape=out_shape,
        scratch_shapes=[pltpu.VMEM((heads, n_chunks, 2 * RET_DK, RET_DV), F32),
                        pltpu.VMEM((heads, n_chunks, 2 * RET_DK, RET_DV), BF16),
                        pltpu.VMEM((heads, L, RET_DV), F32),
                        pltpu.VMEM((heads, n_chunks, 2 * RET_DK, RET_CHUNK), BF16)],
        compiler_params=pltpu.CompilerParams(dimension_semantics=("parallel", "parallel"),
                                             vmem_limit_bytes=VMEM_LIMIT),
        name="ret_state" if has_state else "ret",
    )(*args)


def _out_kernel(ma_ref, mb_ref, w_ref, x_ref, mod_ref, o_ref):
    acc = _dot(ma_ref[0], w_ref[:MLA_W, :]) + _dot(mb_ref[0], w_ref[MLA_W:, :])
    o_ref[0] = x_ref[0] + mod_ref[0, 2:3, :] * acc


def _out_call(mix_a, mix_b, w_out, x, mod3, ctx_mod, tm):
    B, L, _ = x.shape
    mod_map = (lambda b, i: (MOD_ROWS - 1, 0, 0)) if ctx_mod else (lambda b, i: (b, 0, 0))
    return pl.pallas_call(
        _out_kernel,
        grid=(B, L // tm),
        in_specs=[pl.BlockSpec((1, tm, MLA_W), lambda b, i: (b, i, 0)),
                  pl.BlockSpec((1, tm, RET_W), lambda b, i: (b, i, 0)),
                  _const_spec((D_MODEL, D_MODEL)),
                  pl.BlockSpec((1, tm, D_MODEL), lambda b, i: (b, i, 0)),
                  pl.BlockSpec((1, 3, D_MODEL), mod_map)],
        out_specs=pl.BlockSpec((1, tm, D_MODEL), lambda b, i: (b, i, 0)),
        out_shape=jax.ShapeDtypeStruct((B, L, D_MODEL), F32),
        compiler_params=pltpu.CompilerParams(dimension_semantics=("parallel", "parallel"),
                                             vmem_limit_bytes=VMEM_LIMIT),
        name="out_proj",
    )(mix_a, mix_b, w_out, x, mod3)


def _rope_angles(pos, dim):
    half = dim // 2
    freqs = ROPE_BASE ** (-jnp.arange(half, dtype=F32) / half)
    return pos.astype(F32)[:, None] * freqs[None, :]


def _rope_tables(L):
    rows = L // GRID_W
    ar = _rope_angles(jnp.arange(rows), ROPE_DIM // 2)
    ac = _rope_angles(jnp.arange(GRID_W), ROPE_DIM // 2)
    cr, sr, cc, sc = jnp.cos(ar), jnp.sin(ar), jnp.cos(ac), jnp.sin(ac)
    zr, zc = jnp.zeros_like(sr), jnp.zeros_like(sc)

    def grid_table(per_row, per_col):
        w = per_row.shape[-1]
        pad = jnp.zeros((rows, GRID_W, LANE - 2 * w), F32)
        t = jnp.concatenate([jnp.broadcast_to(per_row[:, None, :], (rows, GRID_W, w)),
                             jnp.broadcast_to(per_col[None, :, :], (rows, GRID_W, w)), pad],
                            axis=-1)
        return t.reshape(L, LANE)

    cat = lambda a, b: jnp.concatenate([a, b], axis=-1)
    mc = grid_table(cat(cr, cr), cat(cc, cc))
    msa = grid_table(cat(-sr, zr), cat(-sc, zc))
    msb = grid_table(cat(zr, sr), cat(zc, sc))

    hi = _rope_angles(jnp.arange(rows) * GRID_W, RET_DK)[:, None, :]
    lo = _rope_angles(jnp.arange(GRID_W), RET_DK)[None, :, :]
    cos_t = jnp.cos(hi) * jnp.cos(lo) - jnp.sin(hi) * jnp.sin(lo)
    sin_t = jnp.sin(hi) * jnp.cos(lo) + jnp.cos(hi) * jnp.sin(lo)
    rc = cat(cos_t, cos_t).reshape(L, LANE)
    rs = cat(-sin_t, sin_t).reshape(L, LANE)
    return mc, msa, msb, rc, rs


def _split_kernel(w_ref, lat_ref, wide_ref):
    w = w_ref[0]
    rows = w.shape[0]
    lat_ref[:, :LAT_SRC_COLS] = w[:, :LAT_SRC_COLS].astype(BF16)
    lat_ref[:, LAT_SRC_COLS:] = jnp.zeros((rows, LAT_COLS - LAT_SRC_COLS), BF16)
    wide_ref[...] = w[:, LAT_SRC_COLS:].astype(BF16)


def _split_w_in(w_in, l):
    tr = 256
    n_cols = w_in.shape[2]
    return pl.pallas_call(
        _split_kernel,
        grid=(D_MODEL // tr,),
        in_specs=[pl.BlockSpec((1, tr, n_cols), lambda i: (l, i, 0))],
        out_specs=[pl.BlockSpec((tr, LAT_COLS), lambda i: (i, 0)),
                   pl.BlockSpec((tr, WIDE_COLS), lambda i: (i, 0))],
        out_shape=[jax.ShapeDtypeStruct((D_MODEL, LAT_COLS), BF16),
                   jax.ShapeDtypeStruct((D_MODEL, WIDE_COLS), BF16)],
        compiler_params=pltpu.CompilerParams(dimension_semantics=("parallel",),
                                             vmem_limit_bytes=VMEM_LIMIT),
        name="split_w_in",
    )(w_in)


def _prep_weights(l, norm_w, w_in, q_norm_w, w_uq, kv_norm_w, w_uk, w_uv, qk_q_w, qk_k_w):
    w_lat, w_wide = _split_w_in(w_in, l)
    wq = w_uq[l].reshape(Q_LORA, MLA_HEADS, QK_HEAD)
    wq = jnp.pad(wq, ((0, 0), (0, 0), (0, QK_PAD - QK_HEAD))).reshape(Q_LORA, MLA_HEADS * QK_PAD)
    qscale = (QK_HEAD ** -0.5) * math.log2(math.e)
    qkq = jnp.pad(qk_q_w[l] * qscale, (0, QK_PAD - QK_HEAD))[None, :]
    kk = qk_k_w[l]
    return {
        "norm_w": norm_w[l][None, :],
        "w_lat": w_lat,
        "w_wide": w_wide,
        "q_norm_w": q_norm_w[l][None, :],
        "w_uq": wq.astype(BF16),
        "qk_q_w": qkq,
        "kv_norm_w": kv_norm_w[l][None, :],
        "w_uk": w_uk[l].astype(BF16),
        "w_uvt": w_uv[l].T.astype(BF16),
        "qk_k_n": kk[None, :MLA_NOPE],
        "qk_k_r": jnp.pad(kk[MLA_NOPE:], (0, LANE - ROPE_DIM))[None, :],
    }


def _tiles(L):
    tm = min(L, 256)
    tm_out = min(L, 512)
    tq = min(L, 1024)
    tk = min(L, 256)
    heads = MLA_HEADS if L <= 512 else 2
    ret_heads = RET_HEADS if L <= 512 else 1
    return tm, tm_out, tq, tk, heads, ret_heads


def _layer(x, mod3, ctx_mod, wts, w_out, lgf, lgb, gnw, rope_tabs, ctx, states):
    tm, tm_out, tq, tk, heads, ret_heads = _tiles(x.shape[1])
    outs = _in_call(x, mod3, ctx_mod, wts, rope_tabs, tm)
    q, k, vt, ga, rq, rk, rv, gb = outs[:8]
    mix_a = _attn_call(q, k, vt, ga, ctx, tq, tk, heads)
    ret = _ret_call(lgf, lgb, rq, rk, rv, gb, gnw, states, ret_heads)
    y = _out_call(mix_a, ret[0], w_out, x, mod3, ctx_mod, tm_out)
    return y, outs[8:], ret[1:]


def kernel(x_prompt, x_sample, c, cache_mla_ckv, cache_mla_krope, state_ret_fwd, state_ret_bwd,
           c_ctx, norm_w, w_mod, b_mod, w_in, mla_q_norm_w, mla_w_uq, mla_kv_norm_w, mla_w_uk,
           mla_w_uv, mla_qk_q_w, mla_qk_k_w, ret_log_decay_fwd, ret_log_decay_bwd, ret_gn_w, w_out):
    depth = w_in.shape[0]
    dec_b = x_sample.shape[0]
    assert dec_b < MOD_ROWS
    cvec = jnp.zeros((MOD_ROWS, D_MODEL), F32).at[:dec_b].set(c).at[MOD_ROWS - 1].set(c_ctx)
    rope_tabs = _rope_tables(x_sample.shape[1])

    xp, xs = x_prompt, x_sample
    ckv_l, kr_l, sf_l, sb_l = [], [], [], []
    for l in range(depth):
        wts = _prep_weights(l, norm_w, w_in, mla_q_norm_w, mla_w_uq, mla_kv_norm_w, mla_w_uk,
                            mla_w_uv, mla_qk_q_w, mla_qk_k_w)
        w_out_b = w_out[l].astype(BF16)
        lgf = -jnp.exp(ret_log_decay_fwd[l].astype(F32))
        lgb = -jnp.exp(ret_log_decay_bwd[l].astype(F32))
        gnw = ret_gn_w[l][:, None, :]
        mod3 = _mod_call(cvec, w_mod[l], b_mod[l][None, :]).reshape(MOD_ROWS, 3, D_MODEL)

        xp, (ckv, kr), (sf, sb) = _layer(xp, mod3, True, wts, w_out_b, lgf, lgb, gnw,
                                         None, None, None)
        ckv_l.append(ckv)
        kr_l.append(kr)
        sf_l.append(sf)
        sb_l.append(sb)

        kr_pad = jnp.pad(cache_mla_krope[:, l], ((0, 0), (0, 0), (0, LANE - ROPE_DIM)))
        ctx = _ctx_call(cache_mla_ckv[:, l], kr_pad, wts)
        xs, _, _ = _layer(xs, mod3, False, wts, w_out_b, lgf, lgb, gnw, rope_tabs, ctx,
                          (state_ret_fwd[:, l], state_ret_bwd[:, l]))

    return (xp, xs, jnp.stack(ckv_l, axis=1), jnp.stack(kr_l, axis=1),
            jnp.stack(sf_l, axis=1), jnp.stack(sb_l, axis=1))
```

```python
import functools
import math

import jax
import jax.numpy as jnp
from jax import lax
from jax.experimental import pallas as pl
from jax.experimental.pallas import tpu as pltpu

F32 = jnp.float32
BF16 = jnp.bfloat16

D_MODEL = 2048
GRID_W = 64
MLA_W = 1024
RET_W = 1024
MLA_NOPE = 128
ROPE_DIM = 64
QK_HEAD = MLA_NOPE + ROPE_DIM
MLA_V = 128
MLA_HEADS = 8
Q_LORA = 384
KV_LORA = 256
RET_DV = 256
RET_DK = 128
RET_HEADS = 4
RET_QK = RET_HEADS * RET_DK
RET_CHUNK = 128
ROPE_BASE = 10000.0
EPS = 1e-6

LANE = 128
QK_PAD = 2 * LANE
V_AUG = MLA_V + 16
OVERFLOW_GUARD = 2.0 ** 100
MOD_ROWS = 16
VMEM_LIMIT = 56 * 1024 * 1024

OFF_QLAT = 0
OFF_CKV = OFF_QLAT + Q_LORA
OFF_KROPE = OFF_CKV + KV_LORA
LAT_COLS = OFF_KROPE + LANE
LAT_SRC_COLS = OFF_KROPE + ROPE_DIM
OFF_GA = 0
OFF_RQ = OFF_GA + MLA_W
OFF_RK = OFF_RQ + RET_QK
OFF_RV = OFF_RK + RET_QK
OFF_GB = OFF_RV + RET_W
WIDE_COLS = OFF_GB + RET_W

NT_DIMS = (((1,), (1,)), ((), ()))


def _silu(x):
    return x * (1.0 / (1.0 + jnp.exp(-x)))


def _rms(x, w, n):
    ms = jnp.sum(x * x, axis=-1, keepdims=True) * (1.0 / n)
    return x * lax.rsqrt(ms + EPS) * w


def _dot(a, b):
    return jnp.dot(a, b, preferred_element_type=F32)


def _mod_kernel(c_ref, w_ref, b_ref, o_ref):
    s = _silu(c_ref[...]).astype(BF16)
    o_ref[...] = _dot(s, w_ref[...].astype(BF16)) + b_ref[...]


def _mod_call(cvec, w_mod, b_mod):
    tn = 512
    n = w_mod.shape[1]
    return pl.pallas_call(
        _mod_kernel,
        grid=(n // tn,),
        in_specs=[pl.BlockSpec((MOD_ROWS, D_MODEL), lambda j: (0, 0)),
                  pl.BlockSpec((D_MODEL, tn), lambda j: (0, j)),
                  pl.BlockSpec((1, tn), lambda j: (0, j))],
        out_specs=pl.BlockSpec((MOD_ROWS, tn), lambda j: (0, j)),
        out_shape=jax.ShapeDtypeStruct((MOD_ROWS, n), F32),
        compiler_params=pltpu.CompilerParams(dimension_semantics=("arbitrary",),
                                             vmem_limit_bytes=VMEM_LIMIT),
        name="mod",
    )(cvec, w_mod, b_mod)


def _mla_rope(r, c, sa, sb):
    return r * c + pltpu.roll(r, LANE - 16, 1) * sa + pltpu.roll(r, 16, 1) * sb


def _keys_values(ckv, kr, w_uk_ref, w_uvt_ref, kkn, kkr, rope, k_ref, vt_ref):
    ckv_b = ckv.astype(BF16)
    kn_all = _dot(ckv_b, w_uk_ref[...])
    krw = kr * kkr
    if rope is not None:
        krw = _mla_rope(krw, *rope)
    kr_ss = jnp.sum(kr * kr, axis=-1, keepdims=True)
    for h in range(MLA_HEADS):
        kn = kn_all[:, h * MLA_NOPE:(h + 1) * MLA_NOPE]
        ss = jnp.sum(kn * kn, axis=-1, keepdims=True) + kr_ss
        inv = lax.rsqrt(ss * (1.0 / QK_HEAD) + EPS)
        k_ref[0, h, :, :LANE] = (kn * inv * kkn).astype(BF16)
        k_ref[0, h, :, LANE:] = (krw * inv).astype(BF16)
    vt = lax.dot_general(w_uvt_ref[...], ckv_b, NT_DIMS, preferred_element_type=F32)
    ones_row = jnp.where(lax.broadcasted_iota(jnp.int32, (V_AUG - MLA_V, vt.shape[1]), 0) == 0,
                         1.0, 0.0).astype(BF16)
    for h in range(MLA_HEADS):
        vt_ref[0, h, :MLA_V, :] = vt[h * MLA_V:(h + 1) * MLA_V].astype(BF16)
        vt_ref[0, h, MLA_V:, :] = ones_row


def _in_kernel(*refs, rope, emit_cache):
    it = iter(refs)
    x_ref, mod_ref, nw_ref, wl_ref, w_ref = next(it), next(it), next(it), next(it), next(it)
    qnw_ref, wuq_ref, qkq_ref = next(it), next(it), next(it)
    kvw_ref, wuk_ref, wuvt_ref, kkn_ref, kkr_ref = next(it), next(it), next(it), next(it), next(it)
    if rope:
        mc_ref, msa_ref, msb_ref, rc_ref, rs_ref = next(it), next(it), next(it), next(it), next(it)
    q_ref, k_ref, vt_ref, ga_ref = next(it), next(it), next(it), next(it)
    rq_ref, rk_ref, rv_ref, gb_ref = next(it), next(it), next(it), next(it)
    if emit_cache:
        ckv_ref, kr_ref = next(it), next(it)

    x = x_ref[0]
    shift = mod_ref[0, 0:1, :]
    scale = mod_ref[0, 1:2, :]
    h = (_rms(x, nw_ref[...], D_MODEL) * (1.0 + scale) + shift).astype(BF16)

    mla_rope = (mc_ref[...], msa_ref[...], msb_ref[...]) if rope else None

    a = _dot(h, wl_ref[...])
    qn = _rms(a[:, OFF_QLAT:OFF_CKV], qnw_ref[...], Q_LORA).astype(BF16)
    ckv = _rms(a[:, OFF_CKV:OFF_KROPE], kvw_ref[...], KV_LORA)
    kr = a[:, OFF_KROPE:LAT_COLS]
    if emit_cache:
        ckv_ref[0] = ckv
        kr_ref[0] = kr[:, :ROPE_DIM]

    q_all = _dot(qn, wuq_ref[...])
    qkq = qkq_ref[...]
    for hh in range(MLA_HEADS):
        qh = q_all[:, hh * QK_PAD:(hh + 1) * QK_PAD]
        inv = lax.rsqrt(jnp.sum(qh * qh, axis=-1, keepdims=True) * (1.0 / QK_HEAD) + EPS)
        qh = qh * inv * qkq
        r = qh[:, LANE:]
        if rope:
            r = _mla_rope(r, *mla_rope)
        q_ref[0, hh, :, :LANE] = qh[:, :LANE].astype(BF16)
        q_ref[0, hh, :, LANE:] = r.astype(BF16)

    _keys_values(ckv, kr, wuk_ref, wuvt_ref, kkn_ref[...], kkr_ref[...], mla_rope, k_ref, vt_ref)

    ga_ref[0] = _silu(_dot(h, w_ref[:, OFF_GA:OFF_RQ])).astype(BF16)
    gb_ref[0] = _silu(_dot(h, w_ref[:, OFF_GB:WIDE_COLS])).astype(BF16)

    rqk = _dot(h, w_ref[:, OFF_RQ:OFF_RV])
    for hh in range(2 * RET_HEADS):
        t = rqk[:, hh * RET_DK:(hh + 1) * RET_DK]
        if hh >= RET_HEADS:
            t = t * (RET_DK ** -0.5)
        if rope:
            t = t * rc_ref[...] + pltpu.roll(t, RET_DK // 2, 1) * rs_ref[...]
        if hh < RET_HEADS:
            rq_ref[0, :, hh * RET_DK:(hh + 1) * RET_DK] = t.astype(BF16)
        else:
            g = hh - RET_HEADS
            rk_ref[0, :, g * RET_DK:(g + 1) * RET_DK] = t.astype(BF16)
    rv_ref[0] = _dot(h, w_ref[:, OFF_RV:OFF_GB]).astype(BF16)


def _const_spec(shape):
    return pl.BlockSpec(shape, lambda b, i: (0,) * len(shape), pipeline_mode=pl.Buffered(1))


def _in_call(x, mod3, ctx_mod, wts, rope_tabs, tm):
    B, L, _ = x.shape
    rope = rope_tabs is not None
    emit_cache = not rope
    mod_map = (lambda b, i: (MOD_ROWS - 1, 0, 0)) if ctx_mod else (lambda b, i: (b, 0, 0))
    in_specs = [pl.BlockSpec((1, tm, D_MODEL), lambda b, i: (b, i, 0)),
                pl.BlockSpec((1, 3, D_MODEL), mod_map),
                _const_spec((1, D_MODEL)),
                _const_spec((D_MODEL, LAT_COLS)),
                _const_spec((D_MODEL, WIDE_COLS)),
                _const_spec((1, Q_LORA)),
                _const_spec((Q_LORA, MLA_HEADS * QK_PAD)),
                _const_spec((1, QK_PAD)),
                _const_spec((1, KV_LORA)),
                _const_spec((KV_LORA, MLA_HEADS * MLA_NOPE)),
                _const_spec((MLA_W, KV_LORA)),
                _const_spec((1, LANE)),
                _const_spec((1, LANE))]
    args = [x, mod3, wts["norm_w"], wts["w_lat"], wts["w_wide"], wts["q_norm_w"], wts["w_uq"], wts["qk_q_w"],
            wts["kv_norm_w"], wts["w_uk"], wts["w_uvt"], wts["qk_k_n"], wts["qk_k_r"]]
    if rope:
        in_specs += [pl.BlockSpec((tm, LANE), lambda b, i: (i, 0))] * 5
        args += list(rope_tabs)
    tok = lambda w: pl.BlockSpec((1, tm, w), lambda b, i: (b, i, 0))
    out_specs = [pl.BlockSpec((1, MLA_HEADS, tm, QK_PAD), lambda b, i: (b, 0, i, 0)),
                 pl.BlockSpec((1, MLA_HEADS, tm, QK_PAD), lambda b, i: (b, 0, i, 0)),
                 pl.BlockSpec((1, MLA_HEADS, V_AUG, tm), lambda b, i: (b, 0, 0, i)),
                 tok(MLA_W), tok(RET_QK), tok(RET_QK), tok(RET_W), tok(RET_W)]
    out_shape = [jax.ShapeDtypeStruct((B, MLA_HEADS, L, QK_PAD), BF16),
                 jax.ShapeDtypeStruct((B, MLA_HEADS, L, QK_PAD), BF16),
                 jax.ShapeDtypeStruct((B, MLA_HEADS, V_AUG, L), BF16),
                 jax.ShapeDtypeStruct((B, L, MLA_W), BF16),
                 jax.ShapeDtypeStruct((B, L, RET_QK), BF16),
                 jax.ShapeDtypeStruct((B, L, RET_QK), BF16),
                 jax.ShapeDtypeStruct((B, L, RET_W), BF16),
                 jax.ShapeDtypeStruct((B, L, RET_W), BF16)]
    if emit_cache:
        out_specs += [tok(KV_LORA), tok(ROPE_DIM)]
        out_shape += [jax.ShapeDtypeStruct((B, L, KV_LORA), F32),
                      jax.ShapeDtypeStruct((B, L, ROPE_DIM), F32)]
    return pl.pallas_call(
        functools.partial(_in_kernel, rope=rope, emit_cache=emit_cache),
        grid=(B, L // tm),
        in_specs=in_specs, out_specs=out_specs, out_shape=out_shape,
        compiler_params=pltpu.CompilerParams(dimension_semantics=("parallel", "parallel"),
                                             vmem_limit_bytes=VMEM_LIMIT),
        name="in_proj_rope" if rope else "in_proj",
    )(*args)


def _ctx_kernel(ckv_ref, kr_ref, wuk_ref, wuvt_ref, kkn_ref, kkr_ref, k_ref, vt_ref):
    _keys_values(ckv_ref[0], kr_ref[0], wuk_ref, wuvt_ref, kkn_ref[...], kkr_ref[...], None,
                 k_ref, vt_ref)


def _ctx_call(ckv, kr_pad, wts):
    B, P, _ = ckv.shape
    cs = lambda shape: pl.BlockSpec(shape, lambda b: (0,) * len(shape))
    return pl.pallas_call(
        _ctx_kernel,
        grid=(B,),
        in_specs=[pl.BlockSpec((1, P, KV_LORA), lambda b: (b, 0, 0)),
                  pl.BlockSpec((1, P, LANE), lambda b: (b, 0, 0)),
                  cs((KV_LORA, MLA_HEADS * MLA_NOPE)), cs((MLA_W, KV_LORA)),
                  cs((1, LANE)), cs((1, LANE))],
        out_specs=[pl.BlockSpec((1, MLA_HEADS, P, QK_PAD), lambda b: (b, 0, 0, 0)),
                   pl.BlockSpec((1, MLA_HEADS, V_AUG, P), lambda b: (b, 0, 0, 0))],
        out_shape=[jax.ShapeDtypeStruct((B, MLA_HEADS, P, QK_PAD), BF16),
                   jax.ShapeDtypeStruct((B, MLA_HEADS, V_AUG, P), BF16)],
        compiler_params=pltpu.CompilerParams(dimension_semantics=("parallel",),
                                             vmem_limit_bytes=VMEM_LIMIT),
        name="ctx_kv",
    )(ckv, kr_pad, wts["w_uk"], wts["w_uvt"], wts["qk_k_n"], wts["qk_k_r"])


def _attn_kernel(*refs, heads, **kw):
    def first_scores(hh):
        if hh < heads:
            _attn_head(hh, *refs, phase="first", **kw)

    first_scores(0)
    sums = [_attn_head(hh, *refs, phase="fast", hook=functools.partial(first_scores, hh + 1),
                       **kw) for hh in range(heads)]
    worst = functools.reduce(jnp.maximum, sums)

    @pl.when(jnp.logical_not(jnp.max(worst) < OVERFLOW_GUARD))
    def _():
        for hh in range(heads):
            _attn_head(hh, *refs, phase="robust", **kw)


def _attn_head(hh, *refs, tk, n_chunks, has_ctx, phase, hook=None):
    if has_ctx:
        q_ref, k_ref, vt_ref, kc_ref, vtc_ref, ga_ref, o_ref, s_scr = refs
    else:
        q_ref, k_ref, vt_ref, ga_ref, o_ref, s_scr = refs
    q = q_ref[0, hh]
    tq = q.shape[0]
    cols = slice(hh * MLA_V, (hh + 1) * MLA_V)

    def scores(kc):
        return lax.dot_general(kc, q, NT_DIMS, preferred_element_type=F32)

    def chunk(j):
        s0 = pl.multiple_of(j * tk, tk)
        return k_ref[0, hh, pl.ds(s0, tk), :], vt_ref[0, hh, :, pl.ds(s0, tk)]

    def finish(acc):
        out = (acc[:MLA_V] * (1.0 / acc[MLA_V:MLA_V + 1])).T
        o_ref[0, :, cols] = (out * ga_ref[0, :, cols].astype(F32)).astype(BF16)

    if phase == "first":
        s_scr[hh, 0] = scores(k_ref[0, hh, 0:tk, :])
        return None

    if phase == "robust":
        def step(kc, vtc, carry):
            m, acc = carry
            st = scores(kc)
            m_new = jnp.maximum(m, jnp.max(st, axis=0, keepdims=True))
            p = jnp.exp2(st - m_new).astype(BF16)
            return m_new, jnp.exp2(m - m_new) * acc + _dot(vtc, p)

        carry = (jnp.full((1, tq), -1e30, F32), jnp.zeros((V_AUG, tq), F32))
        carry = lax.fori_loop(0, n_chunks, lambda j, c: step(*chunk(j), c), carry)
        if has_ctx:
            carry = step(kc_ref[0, hh], vtc_ref[0, hh], carry)
        finish(carry[1])
        return None

    m_fix = jnp.max(s_scr[hh, 0], axis=0, keepdims=True)

    def weights(st):
        return jnp.exp2((st - m_fix).astype(BF16))

    acc = jnp.zeros((V_AUG, tq), F32)
    for j in range(n_chunks):
        if j + 1 < n_chunks:
            s_scr[hh, (j + 1) % 2] = scores(k_ref[0, hh, (j + 1) * tk:(j + 2) * tk, :])
        elif has_ctx:
            st_ctx = scores(kc_ref[0, hh])
        if j + 1 == n_chunks and not has_ctx:
            hook()
        acc = acc + _dot(vt_ref[0, hh, :, j * tk:(j + 1) * tk], weights(s_scr[hh, j % 2]))
    if has_ctx:
        hook()
        acc = acc + _dot(vtc_ref[0, hh], weights(st_ctx))
    finish(acc)
    return acc[MLA_V:MLA_V + 1]


def _attn_call(q, k, vt, ga, ctx, tq, tk, heads):
    B, H, L, _ = q.shape
    has_ctx = ctx is not None
    in_specs = [pl.BlockSpec((1, heads, tq, QK_PAD), lambda b, h, i: (b, h, i, 0)),
                pl.BlockSpec((1, heads, L, QK_PAD), lambda b, h, i: (b, h, 0, 0)),
                pl.BlockSpec((1, heads, V_AUG, L), lambda b, h, i: (b, h, 0, 0))]
    args = [q, k, vt]
    if has_ctx:
        kc, vtc = ctx
        P = kc.shape[2]
        in_specs += [pl.BlockSpec((1, heads, P, QK_PAD), lambda b, h, i: (b, h, 0, 0)),
                     pl.BlockSpec((1, heads, V_AUG, P), lambda b, h, i: (b, h, 0, 0))]
        args += [kc, vtc]
    in_specs.append(pl.BlockSpec((1, tq, heads * MLA_V), lambda b, h, i: (b, i, h)))
    args.append(ga)
    return pl.pallas_call(
        functools.partial(_attn_kernel, heads=heads, tk=tk, n_chunks=L // tk, has_ctx=has_ctx),
        grid=(B, H // heads, L // tq),
        in_specs=in_specs,
        out_specs=pl.BlockSpec((1, tq, heads * MLA_V), lambda b, h, i: (b, i, h)),
        out_shape=jax.ShapeDtypeStruct((B, L, MLA_W), BF16),
        scratch_shapes=[pltpu.VMEM((heads, 2, tk, tq), F32)],
        compiler_params=pltpu.CompilerParams(
            dimension_semantics=("parallel", "parallel", "parallel"),
            vmem_limit_bytes=VMEM_LIMIT),
        name="attn_ctx" if has_ctx else "attn",
    )(*args)


def _ret_kernel(*refs, heads, **kw):
    for hh in range(heads):
        _ret_head(hh, *refs, heads=heads, **kw)


def _ret_head(hh, *refs, heads, n_chunks, has_state, unroll):
    it = iter(refs)
    lgf_ref, lgb_ref = next(it), next(it)
    q_ref, k_ref, v_ref, gb_ref, gnw_ref = next(it), next(it), next(it), next(it), next(it)
    if has_state:
        s0f_ref, s0b_ref = next(it), next(it)
    out_ref = next(it)
    if not has_state:
        sf_ref, sb_ref = next(it), next(it)
    kv_scr, st_scr, o_scr, at_scr = next(it), next(it), next(it), next(it)

    C = RET_CHUNK
    kcols = slice(hh * RET_DK, (hh + 1) * RET_DK)
    vcols = slice(hh * RET_DV, (hh + 1) * RET_DV)
    hd = pl.program_id(1) * heads + hh
    lgf = lgf_ref[hd]
    lgb = lgb_ref[hd]
    ri = lax.broadcasted_iota(jnp.int32, (C, C), 0).astype(F32)
    ci = lax.broadcasted_iota(jnp.int32, (C, C), 1).astype(F32)
    diff = ri - ci
    low = diff >= 0
    mask = (jnp.where(low, jnp.exp(jnp.where(low, diff, 0.0) * lgf), 0.0)
            + jnp.where(low, 0.0, jnp.exp(jnp.where(low, 0.0, -diff) * lgb)))
    qd_f = jnp.exp((ri + 1.0) * lgf)
    kd_f = jnp.exp((C - 1.0 - ri) * lgf)
    qd_b = jnp.exp((C - ri) * lgb)
    kd_b = jnp.exp(ri * lgb)
    cd_f = jnp.exp(jnp.full((RET_DK, RET_DV), C, F32) * lgf)
    cd_b = jnp.exp(jnp.full((RET_DK, RET_DV), C, F32) * lgb)
    gnw = gnw_ref[hh]

    def rows(n):
        return pl.ds(n * C if isinstance(n, int) else pl.multiple_of(n * C, C), C)

    def loop(lo, hi, body, carry):
        if hi - lo <= 2:
            for i in range(lo, hi):
                carry = body(i, carry)
            return carry
        return lax.fori_loop(lo, hi, body, carry)

    trips = n_chunks // unroll

    def decayed_keys(i):
        for u in range(unroll):
            n = i * unroll + u
            kc = k_ref[0, rows(n), kcols].astype(F32)
            a = jnp.concatenate([kc * kd_f, kc * kd_b], axis=1)
            at_scr[hh, n] = a.T.astype(BF16)

    def increments(i):
        for u in range(unroll):
            n = i * unroll + u
            kv_scr[hh, n] = _dot(at_scr[hh, n], v_ref[0, rows(n), vcols])

    def summarise(i, _):
        increments(i - 1)
        decayed_keys(i)
        return 0

    decayed_keys(0)
    loop(1, trips, summarise, 0)
    increments(trips - 1)

    if has_state:
        init = (s0f_ref[0, hh], s0b_ref[0, hh])
    else:
        init = (jnp.zeros((RET_DK, RET_DV), F32),) * 2

    def scan(i, carry):
        sf, sb = carry
        nb = n_chunks - 1 - i
        st_scr[hh, i, :RET_DK, :] = sf.astype(BF16)
        st_scr[hh, nb, RET_DK:, :] = sb.astype(BF16)
        return (sf * cd_f + kv_scr[hh, i, :RET_DK, :], sb * cd_b + kv_scr[hh, nb, RET_DK:, :])

    sf, sb = loop(0, n_chunks, scan, init)
    if not has_state:
        sf_ref[0, hh] = sf
        sb_ref[0, hh] = sb

    def products(i):
        ns = [i * unroll + u for u in range(unroll)]
        sls = [rows(n) for n in ns]
        qs = [q_ref[0, sl, kcols] for sl in sls]
        ss = [lax.dot_general(q, k_ref[0, sl, kcols], NT_DIMS, preferred_element_type=F32)
              for q, sl in zip(qs, sls)]
        for n, sl, q, s in zip(ns, sls, qs, ss):
            qf = q.astype(F32)
            qq = jnp.concatenate([qf * qd_f, qf * qd_b], axis=1).astype(BF16)
            o_scr[hh, sl, :] = (_dot((s * mask).astype(BF16), v_ref[0, sl, vcols])
                                + _dot(qq, st_scr[hh, n]))

    def normalise(i):
        for u in range(unroll):
            sl = rows(i * unroll + u)
            y = _rms(o_scr[hh, sl, :], gnw, RET_DV) * gb_ref[0, sl, vcols].astype(F32)
            out_ref[0, sl, vcols] = y.astype(BF16)

    def both(i, _):
        normalise(i - 1)
        products(i)
        return 0

    products(0)
    loop(1, trips, both, 0)
    normalise(trips - 1)


def _ret_call(lgf, lgb, rq, rk, rv, gb, gnw, states, heads):
    B, L, _ = rq.shape
    n_chunks = L // RET_CHUNK
    has_state = states is not None
    smem = pl.BlockSpec(memory_space=pltpu.SMEM)
    st_spec = pl.BlockSpec((1, heads, RET_DK, RET_DV), lambda b, h: (b, h, 0, 0))
    in_specs = [smem, smem,
                pl.BlockSpec((1, L, heads * RET_DK), lambda b, h: (b, 0, h)),
                pl.BlockSpec((1, L, heads * RET_DK), lambda b, h: (b, 0, h)),
                pl.BlockSpec((1, L, heads * RET_DV), lambda b, h: (b, 0, h)),
                pl.BlockSpec((1, L, heads * RET_DV), lambda b, h: (b, 0, h)),
                pl.BlockSpec((heads, 1, RET_DV), lambda b, h: (h, 0, 0))]
    args = [lgf, lgb, rq, rk, rv, gb, gnw]
    out_specs = [pl.BlockSpec((1, L, heads * RET_DV), lambda b, h: (b, 0, h))]
    out_shape = [jax.ShapeDtypeStruct((B, L, RET_W), BF16)]
    if has_state:
        in_specs += [st_spec, st_spec]
        args += list(states)
    else:
        out_specs += [st_spec, st_spec]
        out_shape += [jax.ShapeDtypeStruct((B, RET_HEADS, RET_DK, RET_DV), F32)] * 2
    return pl.pallas_call(
        functools.partial(_ret_kernel, heads=heads, n_chunks=n_chunks, has_state=has_state,
                          unroll=math.gcd(n_chunks, 8)),
        grid=(B, RET_HEADS // heads),
        in_specs=in_specs, out_specs=out_specs, out_shape=out_shape,
        scratch_shapes=[pltpu.VMEM((heads, n_chunks, 2 * RET_DK, RET_DV), F32),
                        pltpu.VMEM((heads, n_chunks, 2 * RET_DK, RET_DV), BF16),
                        pltpu.VMEM((heads, L, RET_DV), F32),
                        pltpu.VMEM((heads, n_chunks, 2 * RET_DK, RET_CHUNK), BF16)],
        compiler_params=pltpu.CompilerParams(dimension_semantics=("parallel", "parallel"),
                                             vmem_limit_bytes=VMEM_LIMIT),
        name="ret_state" if has_state else "ret",
    )(*args)


def _out_kernel(ma_ref, mb_ref, w_ref, x_ref, mod_ref, o_ref):
    acc = _dot(ma_ref[0], w_ref[:MLA_W, :]) + _dot(mb_ref[0], w_ref[MLA_W:, :])
    o_ref[0] = x_ref[0] + mod_ref[0, 2:3, :] * acc


def _out_call(mix_a, mix_b, w_out, x, mod3, ctx_mod, tm):
    B, L, _ = x.shape
    mod_map = (lambda b, i: (MOD_ROWS - 1, 0, 0)) if ctx_mod else (lambda b, i: (b, 0, 0))
    return pl.pallas_call(
        _out_kernel,
        grid=(B, L // tm),
        in_specs=[pl.BlockSpec((1, tm, MLA_W), lambda b, i: (b, i, 0)),
                  pl.BlockSpec((1, tm, RET_W), lambda b, i: (b, i, 0)),
                  _const_spec((D_MODEL, D_MODEL)),
                  pl.BlockSpec((1, tm, D_MODEL), lambda b, i: (b, i, 0)),
                  pl.BlockSpec((1, 3, D_MODEL), mod_map)],
        out_specs=pl.BlockSpec((1, tm, D_MODEL), lambda b, i: (b, i, 0)),
        out_shape=jax.ShapeDtypeStruct((B, L, D_MODEL), F32),
        compiler_params=pltpu.CompilerParams(dimension_semantics=("parallel", "parallel"),
                                             vmem_limit_bytes=VMEM_LIMIT),
        name="out_proj",
    )(mix_a, mix_b, w_out, x, mod3)


def _rope_angles(pos, dim):
    half = dim // 2
    freqs = ROPE_BASE ** (-jnp.arange(half, dtype=F32) / half)
    return pos.astype(F32)[:, None] * freqs[None, :]


def _rope_tables(L):
    rows = L // GRID_W
    ar = _rope_angles(jnp.arange(rows), ROPE_DIM // 2)
    ac = _rope_angles(jnp.arange(GRID_W), ROPE_DIM // 2)
    cr, sr, cc, sc = jnp.cos(ar), jnp.sin(ar), jnp.cos(ac), jnp.sin(ac)
    zr, zc = jnp.zeros_like(sr), jnp.zeros_like(sc)

    def grid_table(per_row, per_col):
        w = per_row.shape[-1]
        pad = jnp.zeros((rows, GRID_W, LANE - 2 * w), F32)
        t = jnp.concatenate([jnp.broadcast_to(per_row[:, None, :], (rows, GRID_W, w)),
                             jnp.broadcast_to(per_col[None, :, :], (rows, GRID_W, w)), pad],
                            axis=-1)
        return t.reshape(L, LANE)

    cat = lambda a, b: jnp.concatenate([a, b], axis=-1)
    mc = grid_table(cat(cr, cr), cat(cc, cc))
    msa = grid_table(cat(-sr, zr), cat(-sc, zc))
    msb = grid_table(cat(zr, sr), cat(zc, sc))

    hi = _rope_angles(jnp.arange(rows) * GRID_W, RET_DK)[:, None, :]
    lo = _rope_angles(jnp.arange(GRID_W), RET_DK)[None, :, :]
    cos_t = jnp.cos(hi) * jnp.cos(lo) - jnp.sin(hi) * jnp.sin(lo)
    sin_t = jnp.sin(hi) * jnp.cos(lo) + jnp.cos(hi) * jnp.sin(lo)
    rc = cat(cos_t, cos_t).reshape(L, LANE)
    rs = cat(-sin_t, sin_t).reshape(L, LANE)
    return mc, msa, msb, rc, rs


def _prep_weights(l, norm_w, w_in, q_norm_w, w_uq, kv_norm_w, w_uk, w_uv, qk_q_w, qk_k_w):
    wi = w_in[l]
    w_lat = jnp.pad(wi[:, :LAT_SRC_COLS], ((0, 0), (0, LAT_COLS - LAT_SRC_COLS))).astype(BF16)
    w_wide = wi[:, LAT_SRC_COLS:].astype(BF16)
    wq = w_uq[l].reshape(Q_LORA, MLA_HEADS, QK_HEAD)
    wq = jnp.pad(wq, ((0, 0), (0, 0), (0, QK_PAD - QK_HEAD))).reshape(Q_LORA, MLA_HEADS * QK_PAD)
    qscale = (QK_HEAD ** -0.5) * math.log2(math.e)
    qkq = jnp.pad(qk_q_w[l] * qscale, (0, QK_PAD - QK_HEAD))[None, :]
    kk = qk_k_w[l]
    return {
        "norm_w": norm_w[l][None, :],
        "w_lat": w_lat,
        "w_wide": w_wide,
        "q_norm_w": q_norm_w[l][None, :],
        "w_uq": wq.astype(BF16),
        "qk_q_w": qkq,
        "kv_norm_w": kv_norm_w[l][None, :],
        "w_uk": w_uk[l].astype(BF16),
        "w_uvt": w_uv[l].T.astype(BF16),
        "qk_k_n": kk[None, :MLA_NOPE],
        "qk_k_r": jnp.pad(kk[MLA_NOPE:], (0, LANE - ROPE_DIM))[None, :],
    }


def _tiles(L):
    tm = min(L, 256)
    tm_out = min(L, 512)
    tq = min(L, 1024)
    tk = min(L, 256)
    heads = MLA_HEADS if L <= 512 else 4
    ret_heads = RET_HEADS if L <= 512 else 1
    return tm, tm_out, tq, tk, heads, ret_heads


def _layer(x, mod3, ctx_mod, wts, w_out, lgf, lgb, gnw, rope_tabs, ctx, states):
    tm, tm_out, tq, tk, heads, ret_heads = _tiles(x.shape[1])
    outs = _in_call(x, mod3, ctx_mod, wts, rope_tabs, tm)
    q, k, vt, ga, rq, rk, rv, gb = outs[:8]
    mix_a = _attn_call(q, k, vt, ga, ctx, tq, tk, heads)
    ret = _ret_call(lgf, lgb, rq, rk, rv, gb, gnw, states, ret_heads)
    y = _out_call(mix_a, ret[0], w_out, x, mod3, ctx_mod, tm_out)
    return y, outs[8:], ret[1:]


def kernel(x_prompt, x_sample, c, cache_mla_ckv, cache_mla_krope, state_ret_fwd, state_ret_bwd,
           c_ctx, norm_w, w_mod, b_mod, w_in, mla_q_norm_w, mla_w_uq, mla_kv_norm_w, mla_w_uk,
           mla_w_uv, mla_qk_q_w, mla_qk_k_w, ret_log_decay_fwd, ret_log_decay_bwd, ret_gn_w, w_out):
    depth = w_in.shape[0]
    dec_b = x_sample.shape[0]
    assert dec_b < MOD_ROWS
    cvec = jnp.zeros((MOD_ROWS, D_MODEL), F32).at[:dec_b].set(c).at[MOD_ROWS - 1].set(c_ctx)
    rope_tabs = _rope_tables(x_sample.shape[1])

    xp, xs = x_prompt, x_sample
    ckv_l, kr_l, sf_l, sb_l = [], [], [], []
    for l in range(depth):
        wts = _prep_weights(l, norm_w, w_in, mla_q_norm_w, mla_w_uq, mla_kv_norm_w, mla_w_uk,
                            mla_w_uv, mla_qk_q_w, mla_qk_k_w)
        w_out_b = w_out[l].astype(BF16)
        lgf = -jnp.exp(ret_log_decay_fwd[l].astype(F32))
        lgb = -jnp.exp(ret_log_decay_bwd[l].astype(F32))
        gnw = ret_gn_w[l][:, None, :]
        mod3 = _mod_call(cvec, w_mod[l], b_mod[l][None, :]).reshape(MOD_ROWS, 3, D_MODEL)

        xp, (ckv, kr), (sf, sb) = _layer(xp, mod3, True, wts, w_out_b, lgf, lgb, gnw,
                                         None, None, None)
        ckv_l.append(ckv)
        kr_l.append(kr)
        sf_l.append(sf)
        sb_l.append(sb)

        kr_pad = jnp.pad(cache_mla_krope[:, l], ((0, 0), (0, 0), (0, LANE - ROPE_DIM)))
        ctx = _ctx_call(cache_mla_ckv[:, l], kr_pad, wts)
        xs, _, _ = _layer(xs, mod3, False, wts, w_out_b, lgf, lgb, gnw, rope_tabs, ctx,
                          (state_ret_fwd[:, l], state_ret_bwd[:, l]))

    return (xp, xs, jnp.stack(ckv_l, axis=1), jnp.stack(kr_l, axis=1),
            jnp.stack(sf_l, axis=1), jnp.stack(sb_l, axis=1))
```

```python
import functools
import math

import jax
import jax.numpy as jnp
from jax import lax
from jax.experimental import pallas as pl
from jax.experimental.pallas import tpu as pltpu

F32 = jnp.float32
BF16 = jnp.bfloat16

D_MODEL = 2048
GRID_W = 64
MLA_W = 1024
RET_W = 1024
MLA_NOPE = 128
ROPE_DIM = 64
QK_HEAD = MLA_NOPE + ROPE_DIM
MLA_V = 128
MLA_HEADS = 8
Q_LORA = 384
KV_LORA = 256
RET_DV = 256
RET_DK = 128
RET_HEADS = 4
RET_QK = RET_HEADS * RET_DK
RET_CHUNK = 128
ROPE_BASE = 10000.0
EPS = 1e-6

LANE = 128
QK_PAD = 2 * LANE
V_AUG = MLA_V + 16
OVERFLOW_GUARD = 2.0 ** 100
MOD_ROWS = 16
VMEM_LIMIT = 56 * 1024 * 1024

OFF_QLAT = 0
OFF_CKV = OFF_QLAT + Q_LORA
OFF_KROPE = OFF_CKV + KV_LORA
LAT_COLS = OFF_KROPE + LANE
LAT_SRC_COLS = OFF_KROPE + ROPE_DIM
OFF_GA = 0
OFF_RQ = OFF_GA + MLA_W
OFF_RK = OFF_RQ + RET_QK
OFF_RV = OFF_RK + RET_QK
OFF_GB = OFF_RV + RET_W
WIDE_COLS = OFF_GB + RET_W

NT_DIMS = (((1,), (1,)), ((), ()))


def _silu(x):
    return x * (1.0 / (1.0 + jnp.exp(-x)))


def _rms(x, w, n):
    ms = jnp.sum(x * x, axis=-1, keepdims=True) * (1.0 / n)
    return x * lax.rsqrt(ms + EPS) * w


def _dot(a, b):
    return jnp.dot(a, b, preferred_element_type=F32)


def _mod_kernel(c_ref, w_ref, b_ref, o_ref):
    s = _silu(c_ref[...]).astype(BF16)
    o_ref[...] = _dot(s, w_ref[...].astype(BF16)) + b_ref[...]


def _mod_call(cvec, w_mod, b_mod):
    tn = 512
    n = w_mod.shape[1]
    return pl.pallas_call(
        _mod_kernel,
        grid=(n // tn,),
        in_specs=[pl.BlockSpec((MOD_ROWS, D_MODEL), lambda j: (0, 0)),
                  pl.BlockSpec((D_MODEL, tn), lambda j: (0, j)),
                  pl.BlockSpec((1, tn), lambda j: (0, j))],
        out_specs=pl.BlockSpec((MOD_ROWS, tn), lambda j: (0, j)),
        out_shape=jax.ShapeDtypeStruct((MOD_ROWS, n), F32),
        compiler_params=pltpu.CompilerParams(dimension_semantics=("arbitrary",),
                                             vmem_limit_bytes=VMEM_LIMIT),
        name="mod",
    )(cvec, w_mod, b_mod)


def _mla_rope(r, c, sa, sb):
    return r * c + pltpu.roll(r, LANE - 16, 1) * sa + pltpu.roll(r, 16, 1) * sb


def _keys_values(ckv, kr, w_uk_ref, w_uvt_ref, kkn, kkr, rope, k_ref, vt_ref):
    ckv_b = ckv.astype(BF16)
    kn_all = _dot(ckv_b, w_uk_ref[...])
    krw = kr * kkr
    if rope is not None:
        krw = _mla_rope(krw, *rope)
    kr_ss = jnp.sum(kr * kr, axis=-1, keepdims=True)
    for h in range(MLA_HEADS):
        kn = kn_all[:, h * MLA_NOPE:(h + 1) * MLA_NOPE]
        ss = jnp.sum(kn * kn, axis=-1, keepdims=True) + kr_ss
        inv = lax.rsqrt(ss * (1.0 / QK_HEAD) + EPS)
        k_ref[0, h, :, :LANE] = (kn * inv * kkn).astype(BF16)
        k_ref[0, h, :, LANE:] = (krw * inv).astype(BF16)
    vt = lax.dot_general(w_uvt_ref[...], ckv_b, NT_DIMS, preferred_element_type=F32)
    ones_row = jnp.where(lax.broadcasted_iota(jnp.int32, (V_AUG - MLA_V, vt.shape[1]), 0) == 0,
                         1.0, 0.0).astype(BF16)
    for h in range(MLA_HEADS):
        vt_ref[0, h, :MLA_V, :] = vt[h * MLA_V:(h + 1) * MLA_V].astype(BF16)
        vt_ref[0, h, MLA_V:, :] = ones_row


def _in_kernel(*refs, rope, emit_cache):
    it = iter(refs)
    x_ref, mod_ref, nw_ref, wl_ref, w_ref = next(it), next(it), next(it), next(it), next(it)
    qnw_ref, wuq_ref, qkq_ref = next(it), next(it), next(it)
    kvw_ref, wuk_ref, wuvt_ref, kkn_ref, kkr_ref = next(it), next(it), next(it), next(it), next(it)
    if rope:
        mc_ref, msa_ref, msb_ref, rc_ref, rs_ref = next(it), next(it), next(it), next(it), next(it)
    q_ref, k_ref, vt_ref, ga_ref = next(it), next(it), next(it), next(it)
    rq_ref, rk_ref, rv_ref, gb_ref = next(it), next(it), next(it), next(it)
    if emit_cache:
        ckv_ref, kr_ref = next(it), next(it)

    x = x_ref[0]
    shift = mod_ref[0, 0:1, :]
    scale = mod_ref[0, 1:2, :]
    h = (_rms(x, nw_ref[...], D_MODEL) * (1.0 + scale) + shift).astype(BF16)

    mla_rope = (mc_ref[...], msa_ref[...], msb_ref[...]) if rope else None

    a = _dot(h, wl_ref[...])
    qn = _rms(a[:, OFF_QLAT:OFF_CKV], qnw_ref[...], Q_LORA).astype(BF16)
    ckv = _rms(a[:, OFF_CKV:OFF_KROPE], kvw_ref[...], KV_LORA)
    kr = a[:, OFF_KROPE:LAT_COLS]
    if emit_cache:
        ckv_ref[0] = ckv
        kr_ref[0] = kr[:, :ROPE_DIM]

    q_all = _dot(qn, wuq_ref[...])
    qkq = qkq_ref[...]
    for hh in range(MLA_HEADS):
        qh = q_all[:, hh * QK_PAD:(hh + 1) * QK_PAD]
        inv = lax.rsqrt(jnp.sum(qh * qh, axis=-1, keepdims=True) * (1.0 / QK_HEAD) + EPS)
        qh = qh * inv * qkq
        r = qh[:, LANE:]
        if rope:
            r = _mla_rope(r, *mla_rope)
        q_ref[0, hh, :, :LANE] = qh[:, :LANE].astype(BF16)
        q_ref[0, hh, :, LANE:] = r.astype(BF16)

    _keys_values(ckv, kr, wuk_ref, wuvt_ref, kkn_ref[...], kkr_ref[...], mla_rope, k_ref, vt_ref)

    ga_ref[0] = _silu(_dot(h, w_ref[:, OFF_GA:OFF_RQ])).astype(BF16)
    gb_ref[0] = _silu(_dot(h, w_ref[:, OFF_GB:WIDE_COLS])).astype(BF16)

    rqk = _dot(h, w_ref[:, OFF_RQ:OFF_RV])
    for hh in range(2 * RET_HEADS):
        t = rqk[:, hh * RET_DK:(hh + 1) * RET_DK]
        if hh >= RET_HEADS:
            t = t * (RET_DK ** -0.5)
        if rope:
            t = t * rc_ref[...] + pltpu.roll(t, RET_DK // 2, 1) * rs_ref[...]
        if hh < RET_HEADS:
            rq_ref[0, :, hh * RET_DK:(hh + 1) * RET_DK] = t.astype(BF16)
        else:
            g = hh - RET_HEADS
            rk_ref[0, :, g * RET_DK:(g + 1) * RET_DK] = t.astype(BF16)
    rv_ref[0] = _dot(h, w_ref[:, OFF_RV:OFF_GB]).astype(BF16)


def _const_spec(shape):
    return pl.BlockSpec(shape, lambda b, i: (0,) * len(shape), pipeline_mode=pl.Buffered(1))


def _in_call(x, mod3, ctx_mod, wts, rope_tabs, tm):
    B, L, _ = x.shape
    rope = rope_tabs is not None
    emit_cache = not rope
    mod_map = (lambda b, i: (MOD_ROWS - 1, 0, 0)) if ctx_mod else (lambda b, i: (b, 0, 0))
    in_specs = [pl.BlockSpec((1, tm, D_MODEL), lambda b, i: (b, i, 0)),
                pl.BlockSpec((1, 3, D_MODEL), mod_map),
                _const_spec((1, D_MODEL)),
                _const_spec((D_MODEL, LAT_COLS)),
                _const_spec((D_MODEL, WIDE_COLS)),
                _const_spec((1, Q_LORA)),
                _const_spec((Q_LORA, MLA_HEADS * QK_PAD)),
                _const_spec((1, QK_PAD)),
                _const_spec((1, KV_LORA)),
                _const_spec((KV_LORA, MLA_HEADS * MLA_NOPE)),
                _const_spec((MLA_W, KV_LORA)),
                _const_spec((1, LANE)),
                _const_spec((1, LANE))]
    args = [x, mod3, wts["norm_w"], wts["w_lat"], wts["w_wide"], wts["q_norm_w"], wts["w_uq"], wts["qk_q_w"],
            wts["kv_norm_w"], wts["w_uk"], wts["w_uvt"], wts["qk_k_n"], wts["qk_k_r"]]
    if rope:
        in_specs += [pl.BlockSpec((tm, LANE), lambda b, i: (i, 0))] * 5
        args += list(rope_tabs)
    tok = lambda w: pl.BlockSpec((1, tm, w), lambda b, i: (b, i, 0))
    out_specs = [pl.BlockSpec((1, MLA_HEADS, tm, QK_PAD), lambda b, i: (b, 0, i, 0)),
                 pl.BlockSpec((1, MLA_HEADS, tm, QK_PAD), lambda b, i: (b, 0, i, 0)),
                 pl.BlockSpec((1, MLA_HEADS, V_AUG, tm), lambda b, i: (b, 0, 0, i)),
                 tok(MLA_W), tok(RET_QK), tok(RET_QK), tok(RET_W), tok(RET_W)]
    out_shape = [jax.ShapeDtypeStruct((B, MLA_HEADS, L, QK_PAD), BF16),
                 jax.ShapeDtypeStruct((B, MLA_HEADS, L, QK_PAD), BF16),
                 jax.ShapeDtypeStruct((B, MLA_HEADS, V_AUG, L), BF16),
                 jax.ShapeDtypeStruct((B, L, MLA_W), BF16),
                 jax.ShapeDtypeStruct((B, L, RET_QK), BF16),
                 jax.ShapeDtypeStruct((B, L, RET_QK), BF16),
                 jax.ShapeDtypeStruct((B, L, RET_W), BF16),
                 jax.ShapeDtypeStruct((B, L, RET_W), BF16)]
    if emit_cache:
        out_specs += [tok(KV_LORA), tok(ROPE_DIM)]
        out_shape += [jax.ShapeDtypeStruct((B, L, KV_LORA), F32),
                      jax.ShapeDtypeStruct((B, L, ROPE_DIM), F32)]
    return pl.pallas_call(
        functools.partial(_in_kernel, rope=rope, emit_cache=emit_cache),
        grid=(B, L // tm),
        in_specs=in_specs, out_specs=out_specs, out_shape=out_shape,
        compiler_params=pltpu.CompilerParams(dimension_semantics=("parallel", "parallel"),
                                             vmem_limit_bytes=VMEM_LIMIT),
        name="in_proj_rope" if rope else "in_proj",
    )(*args)


def _ctx_kernel(ckv_ref, kr_ref, wuk_ref, wuvt_ref, kkn_ref, kkr_ref, k_ref, vt_ref):
    _keys_values(ckv_ref[0], kr_ref[0], wuk_ref, wuvt_ref, kkn_ref[...], kkr_ref[...], None,
                 k_ref, vt_ref)


def _ctx_call(ckv, kr_pad, wts):
    B, P, _ = ckv.shape
    cs = lambda shape: pl.BlockSpec(shape, lambda b: (0,) * len(shape))
    return pl.pallas_call(
        _ctx_kernel,
        grid=(B,),
        in_specs=[pl.BlockSpec((1, P, KV_LORA), lambda b: (b, 0, 0)),
                  pl.BlockSpec((1, P, LANE), lambda b: (b, 0, 0)),
                  cs((KV_LORA, MLA_HEADS * MLA_NOPE)), cs((MLA_W, KV_LORA)),
                  cs((1, LANE)), cs((1, LANE))],
        out_specs=[pl.BlockSpec((1, MLA_HEADS, P, QK_PAD), lambda b: (b, 0, 0, 0)),
                   pl.BlockSpec((1, MLA_HEADS, V_AUG, P), lambda b: (b, 0, 0, 0))],
        out_shape=[jax.ShapeDtypeStruct((B, MLA_HEADS, P, QK_PAD), BF16),
                   jax.ShapeDtypeStruct((B, MLA_HEADS, V_AUG, P), BF16)],
        compiler_params=pltpu.CompilerParams(dimension_semantics=("parallel",),
                                             vmem_limit_bytes=VMEM_LIMIT),
        name="ctx_kv",
    )(ckv, kr_pad, wts["w_uk"], wts["w_uvt"], wts["qk_k_n"], wts["qk_k_r"])


def _attn_kernel(*refs, heads, **kw):
    def first_scores(hh):
        if hh < heads:
            _attn_head(hh, *refs, phase="first", **kw)

    first_scores(0)
    sums = [_attn_head(hh, *refs, phase="fast", hook=functools.partial(first_scores, hh + 1),
                       **kw) for hh in range(heads)]
    worst = functools.reduce(jnp.maximum, sums)

    @pl.when(jnp.logical_not(jnp.max(worst) < OVERFLOW_GUARD))
    def _():
        for hh in range(heads):
            _attn_head(hh, *refs, phase="robust", **kw)


def _attn_head(hh, *refs, tk, n_chunks, has_ctx, phase, hook=None):
    if has_ctx:
        q_ref, k_ref, vt_ref, kc_ref, vtc_ref, ga_ref, o_ref, s_scr = refs
    else:
        q_ref, k_ref, vt_ref, ga_ref, o_ref, s_scr = refs
    q = q_ref[0, hh]
    tq = q.shape[0]
    cols = slice(hh * MLA_V, (hh + 1) * MLA_V)

    def scores(kc):
        return lax.dot_general(kc, q, NT_DIMS, preferred_element_type=F32)

    def chunk(j):
        s0 = pl.multiple_of(j * tk, tk)
        return k_ref[0, hh, pl.ds(s0, tk), :], vt_ref[0, hh, :, pl.ds(s0, tk)]

    def finish(acc):
        out = (acc[:MLA_V] * (1.0 / acc[MLA_V:MLA_V + 1])).T
        o_ref[0, :, cols] = (out * ga_ref[0, :, cols].astype(F32)).astype(BF16)

    if phase == "first":
        s_scr[hh, 0] = scores(k_ref[0, hh, 0:tk, :])
        return None

    if phase == "robust":
        def step(kc, vtc, carry):
            m, acc = carry
            st = scores(kc)
            m_new = jnp.maximum(m, jnp.max(st, axis=0, keepdims=True))
            p = jnp.exp2(st - m_new).astype(BF16)
            return m_new, jnp.exp2(m - m_new) * acc + _dot(vtc, p)

        carry = (jnp.full((1, tq), -1e30, F32), jnp.zeros((V_AUG, tq), F32))
        carry = lax.fori_loop(0, n_chunks, lambda j, c: step(*chunk(j), c), carry)
        if has_ctx:
            carry = step(kc_ref[0, hh], vtc_ref[0, hh], carry)
        finish(carry[1])
        return None

    m_fix = jnp.max(s_scr[hh, 0], axis=0, keepdims=True)

    def weights(st):
        return jnp.exp2(st - m_fix).astype(BF16)

    acc = jnp.zeros((V_AUG, tq), F32)
    for j in range(n_chunks):
        if j + 1 < n_chunks:
            s_scr[hh, (j + 1) % 2] = scores(k_ref[0, hh, (j + 1) * tk:(j + 2) * tk, :])
        elif has_ctx:
            st_ctx = scores(kc_ref[0, hh])
        if j + 1 == n_chunks and not has_ctx:
            hook()
        acc = acc + _dot(vt_ref[0, hh, :, j * tk:(j + 1) * tk], weights(s_scr[hh, j % 2]))
    if has_ctx:
        hook()
        acc = acc + _dot(vtc_ref[0, hh], weights(st_ctx))
    finish(acc)
    return acc[MLA_V:MLA_V + 1]


def _attn_call(q, k, vt, ga, ctx, tq, tk, heads):
    B, H, L, _ = q.shape
    has_ctx = ctx is not None
    in_specs = [pl.BlockSpec((1, heads, tq, QK_PAD), lambda b, h, i: (b, h, i, 0)),
                pl.BlockSpec((1, heads, L, QK_PAD), lambda b, h, i: (b, h, 0, 0)),
                pl.BlockSpec((1, heads, V_AUG, L), lambda b, h, i: (b, h, 0, 0))]
    args = [q, k, vt]
    if has_ctx:
        kc, vtc = ctx
        P = kc.shape[2]
        in_specs += [pl.BlockSpec((1, heads, P, QK_PAD), lambda b, h, i: (b, h, 0, 0)),
                     pl.BlockSpec((1, heads, V_AUG, P), lambda b, h, i: (b, h, 0, 0))]
        args += [kc, vtc]
    in_specs.append(pl.BlockSpec((1, tq, heads * MLA_V), lambda b, h, i: (b, i, h)))
    args.append(ga)
    return pl.pallas_call(
        functools.partial(_attn_kernel, heads=heads, tk=tk, n_chunks=L // tk, has_ctx=has_ctx),
        grid=(B, H // heads, L // tq),
        in_specs=in_specs,
        out_specs=pl.BlockSpec((1, tq, heads * MLA_V), lambda b, h, i: (b, i, h)),
        out_shape=jax.ShapeDtypeStruct((B, L, MLA_W), BF16),
        scratch_shapes=[pltpu.VMEM((heads, 2, tk, tq), F32)],
        compiler_params=pltpu.CompilerParams(
            dimension_semantics=("parallel", "parallel", "parallel"),
            vmem_limit_bytes=VMEM_LIMIT),
        name="attn_ctx" if has_ctx else "attn",
    )(*args)


def _ret_kernel(*refs, heads, **kw):
    for hh in range(heads):
        _ret_head(hh, *refs, heads=heads, **kw)


def _ret_head(hh, *refs, heads, n_chunks, has_state, unroll):
    it = iter(refs)
    lgf_ref, lgb_ref = next(it), next(it)
    q_ref, k_ref, v_ref, gb_ref, gnw_ref = next(it), next(it), next(it), next(it), next(it)
    if has_state:
        s0f_ref, s0b_ref = next(it), next(it)
    out_ref = next(it)
    if not has_state:
        sf_ref, sb_ref = next(it), next(it)
    kv_scr, st_scr, o_scr, at_scr = next(it), next(it), next(it), next(it)

    C = RET_CHUNK
    kcols = slice(hh * RET_DK, (hh + 1) * RET_DK)
    vcols = slice(hh * RET_DV, (hh + 1) * RET_DV)
    hd = pl.program_id(1) * heads + hh
    lgf = lgf_ref[hd]
    lgb = lgb_ref[hd]
    ri = lax.broadcasted_iota(jnp.int32, (C, C), 0).astype(F32)
    ci = lax.broadcasted_iota(jnp.int32, (C, C), 1).astype(F32)
    diff = ri - ci
    low = diff >= 0
    mask = (jnp.where(low, jnp.exp(jnp.where(low, diff, 0.0) * lgf), 0.0)
            + jnp.where(low, 0.0, jnp.exp(jnp.where(low, 0.0, -diff) * lgb)))
    qd_f = jnp.exp((ri + 1.0) * lgf)
    kd_f = jnp.exp((C - 1.0 - ri) * lgf)
    qd_b = jnp.exp((C - ri) * lgb)
    kd_b = jnp.exp(ri * lgb)
    cd_f = jnp.exp(jnp.full((RET_DK, RET_DV), C, F32) * lgf)
    cd_b = jnp.exp(jnp.full((RET_DK, RET_DV), C, F32) * lgb)
    gnw = gnw_ref[hh]

    def rows(n):
        return pl.ds(n * C if isinstance(n, int) else pl.multiple_of(n * C, C), C)

    def loop(lo, hi, body, carry):
        if hi - lo <= 2:
            for i in range(lo, hi):
                carry = body(i, carry)
            return carry
        return lax.fori_loop(lo, hi, body, carry)

    trips = n_chunks // unroll

    def decayed_keys(i):
        for u in range(unroll):
            n = i * unroll + u
            kc = k_ref[0, rows(n), kcols].astype(F32)
            a = jnp.concatenate([kc * kd_f, kc * kd_b], axis=1)
            at_scr[hh, n] = a.T.astype(BF16)

    def increments(i):
        for u in range(unroll):
            n = i * unroll + u
            kv_scr[hh, n] = _dot(at_scr[hh, n], v_ref[0, rows(n), vcols])

    def summarise(i, _):
        increments(i - 1)
        decayed_keys(i)
        return 0

    decayed_keys(0)
    loop(1, trips, summarise, 0)
    increments(trips - 1)

    if has_state:
        init = (s0f_ref[0, hh], s0b_ref[0, hh])
    else:
        init = (jnp.zeros((RET_DK, RET_DV), F32),) * 2

    def scan(i, carry):
        sf, sb = carry
        nb = n_chunks - 1 - i
        st_scr[hh, i, :RET_DK, :] = sf.astype(BF16)
        st_scr[hh, nb, RET_DK:, :] = sb.astype(BF16)
        return (sf * cd_f + kv_scr[hh, i, :RET_DK, :], sb * cd_b + kv_scr[hh, nb, RET_DK:, :])

    sf, sb = loop(0, n_chunks, scan, init)
    if not has_state:
        sf_ref[0, hh] = sf
        sb_ref[0, hh] = sb

    def products(i):
        ns = [i * unroll + u for u in range(unroll)]
        sls = [rows(n) for n in ns]
        qs = [q_ref[0, sl, kcols] for sl in sls]
        ss = [lax.dot_general(q, k_ref[0, sl, kcols], NT_DIMS, preferred_element_type=F32)
              for q, sl in zip(qs, sls)]
        for n, sl, q, s in zip(ns, sls, qs, ss):
            qf = q.astype(F32)
            qq = jnp.concatenate([qf * qd_f, qf * qd_b], axis=1).astype(BF16)
            o_scr[hh, sl, :] = (_dot((s * mask).astype(BF16), v_ref[0, sl, vcols])
                                + _dot(qq, st_scr[hh, n]))

    def normalise(i):
        for u in range(unroll):
            sl = rows(i * unroll + u)
            y = _rms(o_scr[hh, sl, :], gnw, RET_DV) * gb_ref[0, sl, vcols].astype(F32)
            out_ref[0, sl, vcols] = y.astype(BF16)

    def both(i, _):
        normalise(i - 1)
        products(i)
        return 0

    products(0)
    loop(1, trips, both, 0)
    normalise(trips - 1)


def _ret_call(lgf, lgb, rq, rk, rv, gb, gnw, states, heads):
    B, L, _ = rq.shape
    n_chunks = L // RET_CHUNK
    has_state = states is not None
    smem = pl.BlockSpec(memory_space=pltpu.SMEM)
    st_spec = pl.BlockSpec((1, heads, RET_DK, RET_DV), lambda b, h: (b, h, 0, 0))
    in_specs = [smem, smem,
                pl.BlockSpec((1, L, heads * RET_DK), lambda b, h: (b, 0, h)),
                pl.BlockSpec((1, L, heads * RET_DK), lambda b, h: (b, 0, h)),
                pl.BlockSpec((1, L, heads * RET_DV), lambda b, h: (b, 0, h)),
                pl.BlockSpec((1, L, heads * RET_DV), lambda b, h: (b, 0, h)),
                pl.BlockSpec((heads, 1, RET_DV), lambda b, h: (h, 0, 0))]
    args = [lgf, lgb, rq, rk, rv, gb, gnw]
    out_specs = [pl.BlockSpec((1, L, heads * RET_DV), lambda b, h: (b, 0, h))]
    out_shape = [jax.ShapeDtypeStruct((B, L, RET_W), BF16)]
    if has_state:
        in_specs += [st_spec, st_spec]
        args += list(states)
    else:
        out_specs += [st_spec, st_spec]
        out_shape += [jax.ShapeDtypeStruct((B, RET_HEADS, RET_DK, RET_DV), F32)] * 2
    return pl.pallas_call(
        functools.partial(_ret_kernel, heads=heads, n_chunks=n_chunks, has_state=has_state,
                          unroll=math.gcd(n_chunks, 8)),
        grid=(B, RET_HEADS // heads),
        in_specs=in_specs, out_specs=out_specs, out_shape=out_shape,
        scratch_shapes=[pltpu.VMEM((heads, n_chunks, 2 * RET_DK, RET_DV), F32),
                        pltpu.VMEM((heads, n_chunks, 2 * RET_DK, RET_DV), BF16),
                        pltpu.VMEM((heads, L, RET_DV), F32),
                        pltpu.VMEM((heads, n_chunks, 2 * RET_DK, RET_CHUNK), BF16)],
        compiler_params=pltpu.CompilerParams(dimension_semantics=("parallel", "parallel"),
                                             vmem_limit_bytes=VMEM_LIMIT),
        name="ret_state" if has_state else "ret",
    )(*args)


def _out_kernel(ma_ref, mb_ref, w_ref, x_ref, mod_ref, o_ref):
    acc = _dot(ma_ref[0], w_ref[:MLA_W, :]) + _dot(mb_ref[0], w_ref[MLA_W:, :])
    o_ref[0] = x_ref[0] + mod_ref[0, 2:3, :] * acc


def _out_call(mix_a, mix_b, w_out, x, mod3, ctx_mod, tm):
    B, L, _ = x.shape
    mod_map = (lambda b, i: (MOD_ROWS - 1, 0, 0)) if ctx_mod else (lambda b, i: (b, 0, 0))
    return pl.pallas_call(
        _out_kernel,
        grid=(B, L // tm),
        in_specs=[pl.BlockSpec((1, tm, MLA_W), lambda b, i: (b, i, 0)),
                  pl.BlockSpec((1, tm, RET_W), lambda b, i: (b, i, 0)),
                  _const_spec((D_MODEL, D_MODEL)),
                  pl.BlockSpec((1, tm, D_MODEL), lambda b, i: (b, i, 0)),
                  pl.BlockSpec((1, 3, D_MODEL), mod_map)],
        out_specs=pl.BlockSpec((1, tm, D_MODEL), lambda b, i: (b, i, 0)),
        out_shape=jax.ShapeDtypeStruct((B, L, D_MODEL), F32),
        compiler_params=pltpu.CompilerParams(dimension_semantics=("parallel", "parallel"),
                                             vmem_limit_bytes=VMEM_LIMIT),
        name="out_proj",
    )(mix_a, mix_b, w_out, x, mod3)


def _rope_angles(pos, dim):
    half = dim // 2
    freqs = ROPE_BASE ** (-jnp.arange(half, dtype=F32) / half)
    return pos.astype(F32)[:, None] * freqs[None, :]


def _rope_tables(L):
    rows = L // GRID_W
    ar = _rope_angles(jnp.arange(rows), ROPE_DIM // 2)
    ac = _rope_angles(jnp.arange(GRID_W), ROPE_DIM // 2)
    cr, sr, cc, sc = jnp.cos(ar), jnp.sin(ar), jnp.cos(ac), jnp.sin(ac)
    zr, zc = jnp.zeros_like(sr), jnp.zeros_like(sc)

    def grid_table(per_row, per_col):
        w = per_row.shape[-1]
        pad = jnp.zeros((rows, GRID_W, LANE - 2 * w), F32)
        t = jnp.concatenate([jnp.broadcast_to(per_row[:, None, :], (rows, GRID_W, w)),
                             jnp.broadcast_to(per_col[None, :, :], (rows, GRID_W, w)), pad],
                            axis=-1)
        return t.reshape(L, LANE)

    cat = lambda a, b: jnp.concatenate([a, b], axis=-1)
    mc = grid_table(cat(cr, cr), cat(cc, cc))
    msa = grid_table(cat(-sr, zr), cat(-sc, zc))
    msb = grid_table(cat(zr, sr), cat(zc, sc))

    hi = _rope_angles(jnp.arange(rows) * GRID_W, RET_DK)[:, None, :]
    lo = _rope_angles(jnp.arange(GRID_W), RET_DK)[None, :, :]
    cos_t = jnp.cos(hi) * jnp.cos(lo) - jnp.sin(hi) * jnp.sin(lo)
    sin_t = jnp.sin(hi) * jnp.cos(lo) + jnp.cos(hi) * jnp.sin(lo)
    rc = cat(cos_t, cos_t).reshape(L, LANE)
    rs = cat(-sin_t, sin_t).reshape(L, LANE)
    return mc, msa, msb, rc, rs


def _prep_weights(l, norm_w, w_in, q_norm_w, w_uq, kv_norm_w, w_uk, w_uv, qk_q_w, qk_k_w):
    wi = w_in[l]
    w_lat = jnp.pad(wi[:, :LAT_SRC_COLS], ((0, 0), (0, LAT_COLS - LAT_SRC_COLS))).astype(BF16)
    w_wide = wi[:, LAT_SRC_COLS:].astype(BF16)
    wq = w_uq[l].reshape(Q_LORA, MLA_HEADS, QK_HEAD)
    wq = jnp.pad(wq, ((0, 0), (0, 0), (0, QK_PAD - QK_HEAD))).reshape(Q_LORA, MLA_HEADS * QK_PAD)
    qscale = (QK_HEAD ** -0.5) * math.log2(math.e)
    qkq = jnp.pad(qk_q_w[l] * qscale, (0, QK_PAD - QK_HEAD))[None, :]
    kk = qk_k_w[l]
    return {
        "norm_w": norm_w[l][None, :],
        "w_lat": w_lat,
        "w_wide": w_wide,
        "q_norm_w": q_norm_w[l][None, :],
        "w_uq": wq.astype(BF16),
        "qk_q_w": qkq,
        "kv_norm_w": kv_norm_w[l][None, :],
        "w_uk": w_uk[l].astype(BF16),
        "w_uvt": w_uv[l].T.astype(BF16),
        "qk_k_n": kk[None, :MLA_NOPE],
        "qk_k_r": jnp.pad(kk[MLA_NOPE:], (0, LANE - ROPE_DIM))[None, :],
    }


def _tiles(L):
    tm = min(L, 256)
    tm_out = min(L, 512)
    tq = min(L, 1024)
    tk = min(L, 256)
    heads = MLA_HEADS if L <= 512 else 4
    ret_heads = RET_HEADS if L <= 512 else 1
    return tm, tm_out, tq, tk, heads, ret_heads


def _layer(x, mod3, ctx_mod, wts, w_out, lgf, lgb, gnw, rope_tabs, ctx, states):
    tm, tm_out, tq, tk, heads, ret_heads = _tiles(x.shape[1])
    outs = _in_call(x, mod3, ctx_mod, wts, rope_tabs, tm)
    q, k, vt, ga, rq, rk, rv, gb = outs[:8]
    mix_a = _attn_call(q, k, vt, ga, ctx, tq, tk, heads)
    ret = _ret_call(lgf, lgb, rq, rk, rv, gb, gnw, states, ret_heads)
    y = _out_call(mix_a, ret[0], w_out, x, mod3, ctx_mod, tm_out)
    return y, outs[8:], ret[1:]


def kernel(x_prompt, x_sample, c, cache_mla_ckv, cache_mla_krope, state_ret_fwd, state_ret_bwd,
           c_ctx, norm_w, w_mod, b_mod, w_in, mla_q_norm_w, mla_w_uq, mla_kv_norm_w, mla_w_uk,
           mla_w_uv, mla_qk_q_w, mla_qk_k_w, ret_log_decay_fwd, ret_log_decay_bwd, ret_gn_w, w_out):
    depth = w_in.shape[0]
    dec_b = x_sample.shape[0]
    assert dec_b < MOD_ROWS
    cvec = jnp.zeros((MOD_ROWS, D_MODEL), F32).at[:dec_b].set(c).at[MOD_ROWS - 1].set(c_ctx)
    rope_tabs = _rope_tables(x_sample.shape[1])

    xp, xs = x_prompt, x_sample
    ckv_l, kr_l, sf_l, sb_l = [], [], [], []
    for l in range(depth):
        wts = _prep_weights(l, norm_w, w_in, mla_q_norm_w, mla_w_uq, mla_kv_norm_w, mla_w_uk,
                            mla_w_uv, mla_qk_q_w, mla_qk_k_w)
        w_out_b = w_out[l].astype(BF16)
        lgf = -jnp.exp(ret_log_decay_fwd[l].astype(F32))
        lgb = -jnp.exp(ret_log_decay_bwd[l].astype(F32))
        gnw = ret_gn_w[l][:, None, :]
        mod3 = _mod_call(cvec, w_mod[l], b_mod[l][None, :]).reshape(MOD_ROWS, 3, D_MODEL)

        xp, (ckv, kr), (sf, sb) = _layer(xp, mod3, True, wts, w_out_b, lgf, lgb, gnw,
                                         None, None, None)
        ckv_l.append(ckv)
        kr_l.append(kr)
        sf_l.append(sf)
        sb_l.append(sb)

        kr_pad = jnp.pad(cache_mla_krope[:, l], ((0, 0), (0, 0), (0, LANE - ROPE_DIM)))
        ctx = _ctx_call(cache_mla_ckv[:, l], kr_pad, wts)
        xs, _, _ = _layer(xs, mod3, False, wts, w_out_b, lgf, lgb, gnw, rope_tabs, ctx,
                          (state_ret_fwd[:, l], state_ret_bwd[:, l]))

    return (xp, xs, jnp.stack(ckv_l, axis=1), jnp.stack(kr_l, axis=1),
            jnp.stack(sf_l, axis=1), jnp.stack(sb_l, axis=1))
```

```python
import functools
import math

import jax
import jax.numpy as jnp
from jax import lax
from jax.experimental import pallas as pl
from jax.experimental.pallas import tpu as pltpu

F32 = jnp.float32
BF16 = jnp.bfloat16

D_MODEL = 2048
GRID_W = 64
MLA_W = 1024
RET_W = 1024
MLA_NOPE = 128
ROPE_DIM = 64
QK_HEAD = MLA_NOPE + ROPE_DIM
MLA_V = 128
MLA_HEADS = 8
Q_LORA = 384
KV_LORA = 256
RET_DV = 256
RET_DK = 128
RET_HEADS = 4
RET_QK = RET_HEADS * RET_DK
RET_CHUNK = 128
ROPE_BASE = 10000.0
EPS = 1e-6

LANE = 128
QK_PAD = 2 * LANE
V_AUG = MLA_V + 16
OVERFLOW_GUARD = 2.0 ** 100
MOD_ROWS = 16
VMEM_LIMIT = 56 * 1024 * 1024

OFF_QLAT = 0
OFF_CKV = OFF_QLAT + Q_LORA
OFF_KROPE = OFF_CKV + KV_LORA
LAT_COLS = OFF_KROPE + LANE
LAT_SRC_COLS = OFF_KROPE + ROPE_DIM
OFF_GA = 0
OFF_RQ = OFF_GA + MLA_W
OFF_RK = OFF_RQ + RET_QK
OFF_RV = OFF_RK + RET_QK
OFF_GB = OFF_RV + RET_W
WIDE_COLS = OFF_GB + RET_W

NT_DIMS = (((1,), (1,)), ((), ()))


def _silu(x):
    return x * (1.0 / (1.0 + jnp.exp(-x)))


def _rms(x, w, n):
    ms = jnp.sum(x * x, axis=-1, keepdims=True) * (1.0 / n)
    return x * lax.rsqrt(ms + EPS) * w


def _dot(a, b):
    return jnp.dot(a, b, preferred_element_type=F32)


def _mod_kernel(c_ref, w_ref, b_ref, o_ref):
    s = _silu(c_ref[...]).astype(BF16)
    o_ref[...] = _dot(s, w_ref[...].astype(BF16)) + b_ref[...]


def _mod_call(cvec, w_mod, b_mod):
    tn = 512
    n = w_mod.shape[1]
    return pl.pallas_call(
        _mod_kernel,
        grid=(n // tn,),
        in_specs=[pl.BlockSpec((MOD_ROWS, D_MODEL), lambda j: (0, 0)),
                  pl.BlockSpec((D_MODEL, tn), lambda j: (0, j)),
                  pl.BlockSpec((1, tn), lambda j: (0, j))],
        out_specs=pl.BlockSpec((MOD_ROWS, tn), lambda j: (0, j)),
        out_shape=jax.ShapeDtypeStruct((MOD_ROWS, n), F32),
        compiler_params=pltpu.CompilerParams(dimension_semantics=("arbitrary",),
                                             vmem_limit_bytes=VMEM_LIMIT),
        name="mod",
    )(cvec, w_mod, b_mod)


def _mla_rope(r, c, sa, sb):
    return r * c + pltpu.roll(r, LANE - 16, 1) * sa + pltpu.roll(r, 16, 1) * sb


def _keys_values(ckv, kr, w_uk_ref, w_uvt_ref, kkn, kkr, rope, k_ref, vt_ref):
    ckv_b = ckv.astype(BF16)
    kn_all = _dot(ckv_b, w_uk_ref[...])
    krw = kr * kkr
    if rope is not None:
        krw = _mla_rope(krw, *rope)
    kr_ss = jnp.sum(kr * kr, axis=-1, keepdims=True)
    for h in range(MLA_HEADS):
        kn = kn_all[:, h * MLA_NOPE:(h + 1) * MLA_NOPE]
        ss = jnp.sum(kn * kn, axis=-1, keepdims=True) + kr_ss
        inv = lax.rsqrt(ss * (1.0 / QK_HEAD) + EPS)
        k_ref[0, h, :, :LANE] = (kn * inv * kkn).astype(BF16)
        k_ref[0, h, :, LANE:] = (krw * inv).astype(BF16)
    vt = lax.dot_general(w_uvt_ref[...], ckv_b, NT_DIMS, preferred_element_type=F32)
    ones_row = jnp.where(lax.broadcasted_iota(jnp.int32, (V_AUG - MLA_V, vt.shape[1]), 0) == 0,
                         1.0, 0.0).astype(BF16)
    for h in range(MLA_HEADS):
        vt_ref[0, h, :MLA_V, :] = vt[h * MLA_V:(h + 1) * MLA_V].astype(BF16)
        vt_ref[0, h, MLA_V:, :] = ones_row


def _in_kernel(*refs, rope, emit_cache):
    it = iter(refs)
    x_ref, mod_ref, nw_ref, wl_ref, w_ref = next(it), next(it), next(it), next(it), next(it)
    qnw_ref, wuq_ref, qkq_ref = next(it), next(it), next(it)
    kvw_ref, wuk_ref, wuvt_ref, kkn_ref, kkr_ref = next(it), next(it), next(it), next(it), next(it)
    if rope:
        mc_ref, msa_ref, msb_ref, rc_ref, rs_ref = next(it), next(it), next(it), next(it), next(it)
    q_ref, k_ref, vt_ref, ga_ref = next(it), next(it), next(it), next(it)
    rq_ref, rk_ref, rv_ref, gb_ref = next(it), next(it), next(it), next(it)
    if emit_cache:
        ckv_ref, kr_ref = next(it), next(it)

    x = x_ref[0]
    shift = mod_ref[0, 0:1, :]
    scale = mod_ref[0, 1:2, :]
    h = (_rms(x, nw_ref[...], D_MODEL) * (1.0 + scale) + shift).astype(BF16)

    mla_rope = (mc_ref[...], msa_ref[...], msb_ref[...]) if rope else None

    a = _dot(h, wl_ref[...])
    qn = _rms(a[:, OFF_QLAT:OFF_CKV], qnw_ref[...], Q_LORA).astype(BF16)
    ckv = _rms(a[:, OFF_CKV:OFF_KROPE], kvw_ref[...], KV_LORA)
    kr = a[:, OFF_KROPE:LAT_COLS]
    if emit_cache:
        ckv_ref[0] = ckv
        kr_ref[0] = kr[:, :ROPE_DIM]

    q_all = _dot(qn, wuq_ref[...])
    qkq = qkq_ref[...]
    for hh in range(MLA_HEADS):
        qh = q_all[:, hh * QK_PAD:(hh + 1) * QK_PAD]
        inv = lax.rsqrt(jnp.sum(qh * qh, axis=-1, keepdims=True) * (1.0 / QK_HEAD) + EPS)
        qh = qh * inv * qkq
        r = qh[:, LANE:]
        if rope:
            r = _mla_rope(r, *mla_rope)
        q_ref[0, hh, :, :LANE] = qh[:, :LANE].astype(BF16)
        q_ref[0, hh, :, LANE:] = r.astype(BF16)

    _keys_values(ckv, kr, wuk_ref, wuvt_ref, kkn_ref[...], kkr_ref[...], mla_rope, k_ref, vt_ref)

    ga_ref[0] = _silu(_dot(h, w_ref[:, OFF_GA:OFF_RQ])).astype(BF16)
    gb_ref[0] = _silu(_dot(h, w_ref[:, OFF_GB:WIDE_COLS])).astype(BF16)

    rqk = _dot(h, w_ref[:, OFF_RQ:OFF_RV])
    for hh in range(2 * RET_HEADS):
        t = rqk[:, hh * RET_DK:(hh + 1) * RET_DK]
        if hh >= RET_HEADS:
            t = t * (RET_DK ** -0.5)
        if rope:
            t = t * rc_ref[...] + pltpu.roll(t, RET_DK // 2, 1) * rs_ref[...]
        if hh < RET_HEADS:
            rq_ref[0, :, hh * RET_DK:(hh + 1) * RET_DK] = t.astype(BF16)
        else:
            g = hh - RET_HEADS
            rk_ref[0, :, g * RET_DK:(g + 1) * RET_DK] = t.astype(BF16)
    rv_ref[0] = _dot(h, w_ref[:, OFF_RV:OFF_GB]).astype(BF16)


def _const_spec(shape):
    return pl.BlockSpec(shape, lambda b, i: (0,) * len(shape), pipeline_mode=pl.Buffered(1))


def _in_call(x, mod3, ctx_mod, wts, rope_tabs, tm):
    B, L, _ = x.shape
    rope = rope_tabs is not None
    emit_cache = not rope
    mod_map = (lambda b, i: (MOD_ROWS - 1, 0, 0)) if ctx_mod else (lambda b, i: (b, 0, 0))
    in_specs = [pl.BlockSpec((1, tm, D_MODEL), lambda b, i: (b, i, 0)),
                pl.BlockSpec((1, 3, D_MODEL), mod_map),
                _const_spec((1, D_MODEL)),
                _const_spec((D_MODEL, LAT_COLS)),
                _const_spec((D_MODEL, WIDE_COLS)),
                _const_spec((1, Q_LORA)),
                _const_spec((Q_LORA, MLA_HEADS * QK_PAD)),
                _const_spec((1, QK_PAD)),
                _const_spec((1, KV_LORA)),
                _const_spec((KV_LORA, MLA_HEADS * MLA_NOPE)),
                _const_spec((MLA_W, KV_LORA)),
                _const_spec((1, LANE)),
                _const_spec((1, LANE))]
    args = [x, mod3, wts["norm_w"], wts["w_lat"], wts["w_wide"], wts["q_norm_w"], wts["w_uq"], wts["qk_q_w"],
            wts["kv_norm_w"], wts["w_uk"], wts["w_uvt"], wts["qk_k_n"], wts["qk_k_r"]]
    if rope:
        in_specs += [pl.BlockSpec((tm, LANE), lambda b, i: (i, 0))] * 5
        args += list(rope_tabs)
    tok = lambda w: pl.BlockSpec((1, tm, w), lambda b, i: (b, i, 0))
    out_specs = [pl.BlockSpec((1, MLA_HEADS, tm, QK_PAD), lambda b, i: (b, 0, i, 0)),
                 pl.BlockSpec((1, MLA_HEADS, tm, QK_PAD), lambda b, i: (b, 0, i, 0)),
                 pl.BlockSpec((1, MLA_HEADS, V_AUG, tm), lambda b, i: (b, 0, 0, i)),
                 tok(MLA_W), tok(RET_QK), tok(RET_QK), tok(RET_W), tok(RET_W)]
    out_shape = [jax.ShapeDtypeStruct((B, MLA_HEADS, L, QK_PAD), BF16),
                 jax.ShapeDtypeStruct((B, MLA_HEADS, L, QK_PAD), BF16),
                 jax.ShapeDtypeStruct((B, MLA_HEADS, V_AUG, L), BF16),
                 jax.ShapeDtypeStruct((B, L, MLA_W), BF16),
                 jax.ShapeDtypeStruct((B, L, RET_QK), BF16),
                 jax.ShapeDtypeStruct((B, L, RET_QK), BF16),
                 jax.ShapeDtypeStruct((B, L, RET_W), BF16),
                 jax.ShapeDtypeStruct((B, L, RET_W), BF16)]
    if emit_cache:
        out_specs += [tok(KV_LORA), tok(ROPE_DIM)]
        out_shape += [jax.ShapeDtypeStruct((B, L, KV_LORA), F32),
                      jax.ShapeDtypeStruct((B, L, ROPE_DIM), F32)]
    return pl.pallas_call(
        functools.partial(_in_kernel, rope=rope, emit_cache=emit_cache),
        grid=(B, L // tm),
        in_specs=in_specs, out_specs=out_specs, out_shape=out_shape,
        compiler_params=pltpu.CompilerParams(dimension_semantics=("parallel", "parallel"),
                                             vmem_limit_bytes=VMEM_LIMIT),
        name="in_proj_rope" if rope else "in_proj",
    )(*args)


def _ctx_kernel(ckv_ref, kr_ref, wuk_ref, wuvt_ref, kkn_ref, kkr_ref, k_ref, vt_ref):
    _keys_values(ckv_ref[0], kr_ref[0], wuk_ref, wuvt_ref, kkn_ref[...], kkr_ref[...], None,
                 k_ref, vt_ref)


def _ctx_call(ckv, kr_pad, wts):
    B, P, _ = ckv.shape
    cs = lambda shape: pl.BlockSpec(shape, lambda b: (0,) * len(shape))
    return pl.pallas_call(
        _ctx_kernel,
        grid=(B,),
        in_specs=[pl.BlockSpec((1, P, KV_LORA), lambda b: (b, 0, 0)),
                  pl.BlockSpec((1, P, LANE), lambda b: (b, 0, 0)),
                  cs((KV_LORA, MLA_HEADS * MLA_NOPE)), cs((MLA_W, KV_LORA)),
                  cs((1, LANE)), cs((1, LANE))],
        out_specs=[pl.BlockSpec((1, MLA_HEADS, P, QK_PAD), lambda b: (b, 0, 0, 0)),
                   pl.BlockSpec((1, MLA_HEADS, V_AUG, P), lambda b: (b, 0, 0, 0))],
        out_shape=[jax.ShapeDtypeStruct((B, MLA_HEADS, P, QK_PAD), BF16),
                   jax.ShapeDtypeStruct((B, MLA_HEADS, V_AUG, P), BF16)],
        compiler_params=pltpu.CompilerParams(dimension_semantics=("parallel",),
                                             vmem_limit_bytes=VMEM_LIMIT),
        name="ctx_kv",
    )(ckv, kr_pad, wts["w_uk"], wts["w_uvt"], wts["qk_k_n"], wts["qk_k_r"])


def _attn_kernel(*refs, heads, **kw):
    def first_scores(hh):
        if hh < heads:
            _attn_head(hh, *refs, phase="first", **kw)

    first_scores(0)
    sums = [_attn_head(hh, *refs, phase="fast", hook=functools.partial(first_scores, hh + 1),
                       **kw) for hh in range(heads)]
    worst = functools.reduce(jnp.maximum, sums)

    @pl.when(jnp.logical_not(jnp.max(worst) < OVERFLOW_GUARD))
    def _():
        for hh in range(heads):
            _attn_head(hh, *refs, phase="robust", **kw)


def _attn_head(hh, *refs, tk, n_chunks, has_ctx, phase, hook=None):
    if has_ctx:
        q_ref, k_ref, vt_ref, kc_ref, vtc_ref, ga_ref, o_ref, s_scr = refs
    else:
        q_ref, k_ref, vt_ref, ga_ref, o_ref, s_scr = refs
    q = q_ref[0, hh]
    tq = q.shape[0]
    cols = slice(hh * MLA_V, (hh + 1) * MLA_V)

    def scores(kc):
        return lax.dot_general(kc, q, NT_DIMS, preferred_element_type=F32)

    def chunk(j):
        s0 = pl.multiple_of(j * tk, tk)
        return k_ref[0, hh, pl.ds(s0, tk), :], vt_ref[0, hh, :, pl.ds(s0, tk)]

    def finish(acc):
        out = (acc[:MLA_V] * (1.0 / acc[MLA_V:MLA_V + 1])).T
        o_ref[0, :, cols] = (out * ga_ref[0, :, cols].astype(F32)).astype(BF16)

    if phase == "first":
        s_scr[hh, 0] = scores(k_ref[0, hh, 0:tk, :])
        return None

    if phase == "robust":
        def step(kc, vtc, carry):
            m, acc = carry
            st = scores(kc)
            m_new = jnp.maximum(m, jnp.max(st, axis=0, keepdims=True))
            p = jnp.exp2(st - m_new).astype(BF16)
            return m_new, jnp.exp2(m - m_new) * acc + _dot(vtc, p)

        carry = (jnp.full((1, tq), -1e30, F32), jnp.zeros((V_AUG, tq), F32))
        carry = lax.fori_loop(0, n_chunks, lambda j, c: step(*chunk(j), c), carry)
        if has_ctx:
            carry = step(kc_ref[0, hh], vtc_ref[0, hh], carry)
        finish(carry[1])
        return None

    m_fix = jnp.max(s_scr[hh, 0], axis=0, keepdims=True)

    def weights(st):
        return jnp.exp2(st - m_fix).astype(BF16)

    acc = jnp.zeros((V_AUG, tq), F32)
    for j in range(n_chunks):
        if j + 1 < n_chunks:
            s_scr[hh, (j + 1) % 2] = scores(k_ref[0, hh, (j + 1) * tk:(j + 2) * tk, :])
        elif has_ctx:
            st_ctx = scores(kc_ref[0, hh])
        if j + 1 == n_chunks and not has_ctx:
            hook()
        acc = acc + _dot(vt_ref[0, hh, :, j * tk:(j + 1) * tk], weights(s_scr[hh, j % 2]))
    if has_ctx:
        hook()
        acc = acc + _dot(vtc_ref[0, hh], weights(st_ctx))
    finish(acc)
    return acc[MLA_V:MLA_V + 1]


def _attn_call(q, k, vt, ga, ctx, tq, tk, heads):
    B, H, L, _ = q.shape
    has_ctx = ctx is not None
    in_specs = [pl.BlockSpec((1, heads, tq, QK_PAD), lambda b, h, i: (b, h, i, 0)),
                pl.BlockSpec((1, heads, L, QK_PAD), lambda b, h, i: (b, h, 0, 0)),
                pl.BlockSpec((1, heads, V_AUG, L), lambda b, h, i: (b, h, 0, 0))]
    args = [q, k, vt]
    if has_ctx:
        kc, vtc = ctx
        P = kc.shape[2]
        in_specs += [pl.BlockSpec((1, heads, P, QK_PAD), lambda b, h, i: (b, h, 0, 0)),
                     pl.BlockSpec((1, heads, V_AUG, P), lambda b, h, i: (b, h, 0, 0))]
        args += [kc, vtc]
    in_specs.append(pl.BlockSpec((1, tq, heads * MLA_V), lambda b, h, i: (b, i, h)))
    args.append(ga)
    return pl.pallas_call(
        functools.partial(_attn_kernel, heads=heads, tk=tk, n_chunks=L // tk, has_ctx=has_ctx),
        grid=(B, H // heads, L // tq),
        in_specs=in_specs,
        out_specs=pl.BlockSpec((1, tq, heads * MLA_V), lambda b, h, i: (b, i, h)),
        out_shape=jax.ShapeDtypeStruct((B, L, MLA_W), BF16),
        scratch_shapes=[pltpu.VMEM((heads, 2, tk, tq), F32)],
        compiler_params=pltpu.CompilerParams(
            dimension_semantics=("parallel", "parallel", "parallel"),
            vmem_limit_bytes=VMEM_LIMIT),
        name="attn_ctx" if has_ctx else "attn",
    )(*args)


def _ret_kernel(*refs, heads, **kw):
    for hh in range(heads):
        _ret_head(hh, *refs, heads=heads, **kw)


def _ret_head(hh, *refs, heads, n_chunks, has_state, unroll):
    it = iter(refs)
    lgf_ref, lgb_ref = next(it), next(it)
    q_ref, k_ref, v_ref, gb_ref, gnw_ref = next(it), next(it), next(it), next(it), next(it)
    if has_state:
        s0f_ref, s0b_ref = next(it), next(it)
    out_ref = next(it)
    if not has_state:
        sf_ref, sb_ref = next(it), next(it)
    kv_scr, st_scr, o_scr, at_scr = next(it), next(it), next(it), next(it)

    C = RET_CHUNK
    kcols = slice(hh * RET_DK, (hh + 1) * RET_DK)
    vcols = slice(hh * RET_DV, (hh + 1) * RET_DV)
    hd = pl.program_id(1) * heads + hh
    lgf = lgf_ref[hd]
    lgb = lgb_ref[hd]
    ri = lax.broadcasted_iota(jnp.int32, (C, C), 0).astype(F32)
    ci = lax.broadcasted_iota(jnp.int32, (C, C), 1).astype(F32)
    diff = ri - ci
    low = diff >= 0
    mask = (jnp.where(low, jnp.exp(jnp.where(low, diff, 0.0) * lgf), 0.0)
            + jnp.where(low, 0.0, jnp.exp(jnp.where(low, 0.0, -diff) * lgb)))
    qd_f = jnp.exp((ri + 1.0) * lgf)
    kd_f = jnp.exp((C - 1.0 - ri) * lgf)
    qd_b = jnp.exp((C - ri) * lgb)
    kd_b = jnp.exp(ri * lgb)
    cd_f = jnp.exp(jnp.full((RET_DK, RET_DV), C, F32) * lgf)
    cd_b = jnp.exp(jnp.full((RET_DK, RET_DV), C, F32) * lgb)
    gnw = gnw_ref[hh]

    def rows(n):
        return pl.ds(n * C if isinstance(n, int) else pl.multiple_of(n * C, C), C)

    def loop(lo, hi, body, carry):
        if hi - lo <= 2:
            for i in range(lo, hi):
                carry = body(i, carry)
            return carry
        return lax.fori_loop(lo, hi, body, carry)

    trips = n_chunks // unroll

    def decayed_keys(i):
        for u in range(unroll):
            n = i * unroll + u
            kc = k_ref[0, rows(n), kcols].astype(F32)
            a = jnp.concatenate([kc * kd_f, kc * kd_b], axis=1)
            at_scr[hh, n] = a.T.astype(BF16)

    def increments(i):
        for u in range(unroll):
            n = i * unroll + u
            kv_scr[hh, n] = _dot(at_scr[hh, n], v_ref[0, rows(n), vcols])

    def summarise(i, _):
        increments(i - 1)
        decayed_keys(i)
        return 0

    decayed_keys(0)
    loop(1, trips, summarise, 0)
    increments(trips - 1)

    if has_state:
        init = (s0f_ref[0, hh], s0b_ref[0, hh])
    else:
        init = (jnp.zeros((RET_DK, RET_DV), F32),) * 2

    def scan(i, carry):
        sf, sb = carry
        nb = n_chunks - 1 - i
        st_scr[hh, i, :RET_DK, :] = sf.astype(BF16)
        st_scr[hh, nb, RET_DK:, :] = sb.astype(BF16)
        return (sf * cd_f + kv_scr[hh, i, :RET_DK, :], sb * cd_b + kv_scr[hh, nb, RET_DK:, :])

    sf, sb = loop(0, n_chunks, scan, init)
    if not has_state:
        sf_ref[0, hh] = sf
        sb_ref[0, hh] = sb

    def products(i):
        ns = [i * unroll + u for u in range(unroll)]
        sls = [rows(n) for n in ns]
        qs = [q_ref[0, sl, kcols] for sl in sls]
        ss = [lax.dot_general(q, k_ref[0, sl, kcols], NT_DIMS, preferred_element_type=F32)
              for q, sl in zip(qs, sls)]
        for n, sl, q, s in zip(ns, sls, qs, ss):
            qf = q.astype(F32)
            qq = jnp.concatenate([qf * qd_f, qf * qd_b], axis=1).astype(BF16)
            o_scr[hh, sl, :] = (_dot((s * mask).astype(BF16), v_ref[0, sl, vcols])
                                + _dot(qq, st_scr[hh, n]))

    def normalise(i):
        for u in range(unroll):
            sl = rows(i * unroll + u)
            y = _rms(o_scr[hh, sl, :], gnw, RET_DV) * gb_ref[0, sl, vcols].astype(F32)
            out_ref[0, sl, vcols] = y.astype(BF16)

    def both(i, _):
        normalise(i - 1)
        products(i)
        return 0

    products(0)
    loop(1, trips, both, 0)
    normalise(trips - 1)


def _ret_call(lgf, lgb, rq, rk, rv, gb, gnw, states, heads):
    B, L, _ = rq.shape
    n_chunks = L // RET_CHUNK
    has_state = states is not None
    smem = pl.BlockSpec(memory_space=pltpu.SMEM)
    st_spec = pl.BlockSpec((1, heads, RET_DK, RET_DV), lambda b, h: (b, h, 0, 0))
    in_specs = [smem, smem,
                pl.BlockSpec((1, L, heads * RET_DK), lambda b, h: (b, 0, h)),
                pl.BlockSpec((1, L, heads * RET_DK), lambda b, h: (b, 0, h)),
                pl.BlockSpec((1, L, heads * RET_DV), lambda b, h: (b, 0, h)),
                pl.BlockSpec((1, L, heads * RET_DV), lambda b, h: (b, 0, h)),
                pl.BlockSpec((heads, 1, RET_DV), lambda b, h: (h, 0, 0))]
    args = [lgf, lgb, rq, rk, rv, gb, gnw]
    out_specs = [pl.BlockSpec((1, L, heads * RET_DV), lambda b, h: (b, 0, h))]
    out_shape = [jax.ShapeDtypeStruct((B, L, RET_W), BF16)]
    if has_state:
        in_specs += [st_spec, st_spec]
        args += list(states)
    else:
        out_specs += [st_spec, st_spec]
        out_shape += [jax.ShapeDtypeStruct((B, RET_HEADS, RET_DK, RET_DV), F32)] * 2
    return pl.pallas_call(
        functools.partial(_ret_kernel, heads=heads, n_chunks=n_chunks, has_state=has_state,
                          unroll=math.gcd(n_chunks, 8)),
        grid=(B, RET_HEADS // heads),
        in_specs=in_specs, out_specs=out_specs, out_shape=out_shape,
        scratch_shapes=[pltpu.VMEM((heads, n_chunks, 2 * RET_DK, RET_DV), F32),
                        pltpu.VMEM((heads, n_chunks, 2 * RET_DK, RET_DV), BF16),
                        pltpu.VMEM((heads, L, RET_DV), F32),
                        pltpu.VMEM((heads, n_chunks, 2 * RET_DK, RET_CHUNK), BF16)],
        compiler_params=pltpu.CompilerParams(dimension_semantics=("parallel", "parallel"),
                                             vmem_limit_bytes=VMEM_LIMIT),
        name="ret_state" if has_state else "ret",
    )(*args)


def _out_kernel(ma_ref, mb_ref, w_ref, x_ref, mod_ref, o_ref):
    acc = _dot(ma_ref[0], w_ref[:MLA_W, :]) + _dot(mb_ref[0], w_ref[MLA_W:, :])
    o_ref[0] = x_ref[0] + mod_ref[0, 2:3, :] * acc


def _out_call(mix_a, mix_b, w_out, x, mod3, ctx_mod, tm):
    B, L, _ = x.shape
    mod_map = (lambda b, i: (MOD_ROWS - 1, 0, 0)) if ctx_mod else (lambda b, i: (b, 0, 0))
    return pl.pallas_call(
        _out_kernel,
        grid=(B, L // tm),
        in_specs=[pl.BlockSpec((1, tm, MLA_W), lambda b, i: (b, i, 0)),
                  pl.BlockSpec((1, tm, RET_W), lambda b, i: (b, i, 0)),
                  _const_spec((D_MODEL, D_MODEL)),
                  pl.BlockSpec((1, tm, D_MODEL), lambda b, i: (b, i, 0)),
                  pl.BlockSpec((1, 3, D_MODEL), mod_map)],
        out_specs=pl.BlockSpec((1, tm, D_MODEL), lambda b, i: (b, i, 0)),
        out_shape=jax.ShapeDtypeStruct((B, L, D_MODEL), F32),
        compiler_params=pltpu.CompilerParams(dimension_semantics=("parallel", "parallel"),
                                             vmem_limit_bytes=VMEM_LIMIT),
        name="out_proj",
    )(mix_a, mix_b, w_out, x, mod3)


def _rope_angles(pos, dim):
    half = dim // 2
    freqs = ROPE_BASE ** (-jnp.arange(half, dtype=F32) / half)
    return pos.astype(F32)[:, None] * freqs[None, :]


def _rope_tables(L):
    rows = L // GRID_W
    ar = _rope_angles(jnp.arange(rows), ROPE_DIM // 2)
    ac = _rope_angles(jnp.arange(GRID_W), ROPE_DIM // 2)
    cr, sr, cc, sc = jnp.cos(ar), jnp.sin(ar), jnp.cos(ac), jnp.sin(ac)
    zr, zc = jnp.zeros_like(sr), jnp.zeros_like(sc)

    def grid_table(per_row, per_col):
        w = per_row.shape[-1]
        pad = jnp.zeros((rows, GRID_W, LANE - 2 * w), F32)
        t = jnp.concatenate([jnp.broadcast_to(per_row[:, None, :], (rows, GRID_W, w)),
                             jnp.broadcast_to(per_col[None, :, :], (rows, GRID_W, w)), pad],
                            axis=-1)
        return t.reshape(L, LANE)

    cat = lambda a, b: jnp.concatenate([a, b], axis=-1)
    mc = grid_table(cat(cr, cr), cat(cc, cc))
    msa = grid_table(cat(-sr, zr), cat(-sc, zc))
    msb = grid_table(cat(zr, sr), cat(zc, sc))

    hi = _rope_angles(jnp.arange(rows) * GRID_W, RET_DK)[:, None, :]
    lo = _rope_angles(jnp.arange(GRID_W), RET_DK)[None, :, :]
    cos_t = jnp.cos(hi) * jnp.cos(lo) - jnp.sin(hi) * jnp.sin(lo)
    sin_t = jnp.sin(hi) * jnp.cos(lo) + jnp.cos(hi) * jnp.sin(lo)
    rc = cat(cos_t, cos_t).reshape(L, LANE)
    rs = cat(-sin_t, sin_t).reshape(L, LANE)
    return mc, msa, msb, rc, rs


def _prep_weights(l, norm_w, w_in, q_norm_w, w_uq, kv_norm_w, w_uk, w_uv, qk_q_w, qk_k_w):
    wi = w_in[l]
    w_lat = jnp.pad(wi[:, :LAT_SRC_COLS], ((0, 0), (0, LAT_COLS - LAT_SRC_COLS))).astype(BF16)
    w_wide = wi[:, LAT_SRC_COLS:].astype(BF16)
    wq = w_uq[l].reshape(Q_LORA, MLA_HEADS, QK_HEAD)
    wq = jnp.pad(wq, ((0, 0), (0, 0), (0, QK_PAD - QK_HEAD))).reshape(Q_LORA, MLA_HEADS * QK_PAD)
    qscale = (QK_HEAD ** -0.5) * math.log2(math.e)
    qkq = jnp.pad(qk_q_w[l] * qscale, (0, QK_PAD - QK_HEAD))[None, :]
    kk = qk_k_w[l]
    return {
        "norm_w": norm_w[l][None, :],
        "w_lat": w_lat,
        "w_wide": w_wide,
        "q_norm_w": q_norm_w[l][None, :],
        "w_uq": wq.astype(BF16),
        "qk_q_w": qkq,
        "kv_norm_w": kv_norm_w[l][None, :],
        "w_uk": w_uk[l].astype(BF16),
        "w_uvt": w_uv[l].T.astype(BF16),
        "qk_k_n": kk[None, :MLA_NOPE],
        "qk_k_r": jnp.pad(kk[MLA_NOPE:], (0, LANE - ROPE_DIM))[None, :],
    }


def _tiles(L):
    tm = min(L, 256)
    tm_out = min(L, 512)
    tq = min(L, 1024)
    tk = min(L, 512)
    heads = MLA_HEADS if L <= 512 else 4
    ret_heads = RET_HEADS if L <= 512 else 1
    return tm, tm_out, tq, tk, heads, ret_heads


def _layer(x, mod3, ctx_mod, wts, w_out, lgf, lgb, gnw, rope_tabs, ctx, states):
    tm, tm_out, tq, tk, heads, ret_heads = _tiles(x.shape[1])
    outs = _in_call(x, mod3, ctx_mod, wts, rope_tabs, tm)
    q, k, vt, ga, rq, rk, rv, gb = outs[:8]
    mix_a = _attn_call(q, k, vt, ga, ctx, tq, tk, heads)
    ret = _ret_call(lgf, lgb, rq, rk, rv, gb, gnw, states, ret_heads)
    y = _out_call(mix_a, ret[0], w_out, x, mod3, ctx_mod, tm_out)
    return y, outs[8:], ret[1:]


def kernel(x_prompt, x_sample, c, cache_mla_ckv, cache_mla_krope, state_ret_fwd, state_ret_bwd,
           c_ctx, norm_w, w_mod, b_mod, w_in, mla_q_norm_w, mla_w_uq, mla_kv_norm_w, mla_w_uk,
           mla_w_uv, mla_qk_q_w, mla_qk_k_w, ret_log_decay_fwd, ret_log_decay_bwd, ret_gn_w, w_out):
    depth = w_in.shape[0]
    dec_b = x_sample.shape[0]
    assert dec_b < MOD_ROWS
    cvec = jnp.zeros((MOD_ROWS, D_MODEL), F32).at[:dec_b].set(c).at[MOD_ROWS - 1].set(c_ctx)
    rope_tabs = _rope_tables(x_sample.shape[1])

    xp, xs = x_prompt, x_sample
    ckv_l, kr_l, sf_l, sb_l = [], [], [], []
    for l in range(depth):
        wts = _prep_weights(l, norm_w, w_in, mla_q_norm_w, mla_w_uq, mla_kv_norm_w, mla_w_uk,
                            mla_w_uv, mla_qk_q_w, mla_qk_k_w)
        w_out_b = w_out[l].astype(BF16)
        lgf = -jnp.exp(ret_log_decay_fwd[l].astype(F32))
        lgb = -jnp.exp(ret_log_decay_bwd[l].astype(F32))
        gnw = ret_gn_w[l][:, None, :]
        mod3 = _mod_call(cvec, w_mod[l], b_mod[l][None, :]).reshape(MOD_ROWS, 3, D_MODEL)

        xp, (ckv, kr), (sf, sb) = _layer(xp, mod3, True, wts, w_out_b, lgf, lgb, gnw,
                                         None, None, None)
        ckv_l.append(ckv)
        kr_l.append(kr)
        sf_l.append(sf)
        sb_l.append(sb)

        kr_pad = jnp.pad(cache_mla_krope[:, l], ((0, 0), (0, 0), (0, LANE - ROPE_DIM)))
        ctx = _ctx_call(cache_mla_ckv[:, l], kr_pad, wts)
        xs, _, _ = _layer(xs, mod3, False, wts, w_out_b, lgf, lgb, gnw, rope_tabs, ctx,
                          (state_ret_fwd[:, l], state_ret_bwd[:, l]))

    return (xp, xs, jnp.stack(ckv_l, axis=1), jnp.stack(kr_l, axis=1),
            jnp.stack(sf_l, axis=1), jnp.stack(sb_l, axis=1))
```

```python
import functools
import math

import jax
import jax.numpy as jnp
from jax import lax
from jax.experimental import pallas as pl
from jax.experimental.pallas import tpu as pltpu

F32 = jnp.float32
BF16 = jnp.bfloat16

D_MODEL = 2048
GRID_W = 64
MLA_W = 1024
RET_W = 1024
MLA_NOPE = 128
ROPE_DIM = 64
QK_HEAD = MLA_NOPE + ROPE_DIM
MLA_V = 128
MLA_HEADS = 8
Q_LORA = 384
KV_LORA = 256
RET_DV = 256
RET_DK = 128
RET_HEADS = 4
RET_QK = RET_HEADS * RET_DK
RET_CHUNK = 128
ROPE_BASE = 10000.0
EPS = 1e-6

LANE = 128
QK_PAD = 2 * LANE
V_AUG = MLA_V + 16
OVERFLOW_GUARD = 2.0 ** 100
MOD_ROWS = 16
VMEM_LIMIT = 56 * 1024 * 1024

OFF_QLAT = 0
OFF_CKV = OFF_QLAT + Q_LORA
OFF_KROPE = OFF_CKV + KV_LORA
LAT_COLS = OFF_KROPE + LANE
LAT_SRC_COLS = OFF_KROPE + ROPE_DIM
OFF_GA = 0
OFF_RQ = OFF_GA + MLA_W
OFF_RK = OFF_RQ + RET_QK
OFF_RV = OFF_RK + RET_QK
OFF_GB = OFF_RV + RET_W
WIDE_COLS = OFF_GB + RET_W

NT_DIMS = (((1,), (1,)), ((), ()))


def _silu(x):
    return x * (1.0 / (1.0 + jnp.exp(-x)))


def _rms(x, w, n):
    ms = jnp.sum(x * x, axis=-1, keepdims=True) * (1.0 / n)
    return x * lax.rsqrt(ms + EPS) * w


def _dot(a, b):
    return jnp.dot(a, b, preferred_element_type=F32)


def _mod_kernel(c_ref, w_ref, b_ref, o_ref):
    s = _silu(c_ref[...]).astype(BF16)
    o_ref[...] = _dot(s, w_ref[...].astype(BF16)) + b_ref[...]


def _mod_call(cvec, w_mod, b_mod):
    tn = 512
    n = w_mod.shape[1]
    return pl.pallas_call(
        _mod_kernel,
        grid=(n // tn,),
        in_specs=[pl.BlockSpec((MOD_ROWS, D_MODEL), lambda j: (0, 0)),
                  pl.BlockSpec((D_MODEL, tn), lambda j: (0, j)),
                  pl.BlockSpec((1, tn), lambda j: (0, j))],
        out_specs=pl.BlockSpec((MOD_ROWS, tn), lambda j: (0, j)),
        out_shape=jax.ShapeDtypeStruct((MOD_ROWS, n), F32),
        compiler_params=pltpu.CompilerParams(dimension_semantics=("arbitrary",),
                                             vmem_limit_bytes=VMEM_LIMIT),
        name="mod",
    )(cvec, w_mod, b_mod)


def _mla_rope(r, c, sa, sb):
    return r * c + pltpu.roll(r, LANE - 16, 1) * sa + pltpu.roll(r, 16, 1) * sb


def _keys_values(ckv, kr, w_uk_ref, w_uvt_ref, kkn, kkr, rope, k_ref, vt_ref):
    ckv_b = ckv.astype(BF16)
    kn_all = _dot(ckv_b, w_uk_ref[...])
    krw = kr * kkr
    if rope is not None:
        krw = _mla_rope(krw, *rope)
    kr_ss = jnp.sum(kr * kr, axis=-1, keepdims=True)
    for h in range(MLA_HEADS):
        kn = kn_all[:, h * MLA_NOPE:(h + 1) * MLA_NOPE]
        ss = jnp.sum(kn * kn, axis=-1, keepdims=True) + kr_ss
        inv = lax.rsqrt(ss * (1.0 / QK_HEAD) + EPS)
        k_ref[0, h, :, :LANE] = (kn * inv * kkn).astype(BF16)
        k_ref[0, h, :, LANE:] = (krw * inv).astype(BF16)
    vt = lax.dot_general(w_uvt_ref[...], ckv_b, NT_DIMS, preferred_element_type=F32)
    ones_row = jnp.where(lax.broadcasted_iota(jnp.int32, (V_AUG - MLA_V, vt.shape[1]), 0) == 0,
                         1.0, 0.0).astype(BF16)
    for h in range(MLA_HEADS):
        vt_ref[0, h, :MLA_V, :] = vt[h * MLA_V:(h + 1) * MLA_V].astype(BF16)
        vt_ref[0, h, MLA_V:, :] = ones_row


def _in_kernel(*refs, rope, emit_cache):
    it = iter(refs)
    x_ref, mod_ref, nw_ref, wl_ref, w_ref = next(it), next(it), next(it), next(it), next(it)
    qnw_ref, wuq_ref, qkq_ref = next(it), next(it), next(it)
    kvw_ref, wuk_ref, wuvt_ref, kkn_ref, kkr_ref = next(it), next(it), next(it), next(it), next(it)
    if rope:
        mc_ref, msa_ref, msb_ref, rc_ref, rs_ref = next(it), next(it), next(it), next(it), next(it)
    q_ref, k_ref, vt_ref, ga_ref = next(it), next(it), next(it), next(it)
    rq_ref, rk_ref, rv_ref, gb_ref = next(it), next(it), next(it), next(it)
    if emit_cache:
        ckv_ref, kr_ref = next(it), next(it)

    x = x_ref[0]
    shift = mod_ref[0, 0:1, :]
    scale = mod_ref[0, 1:2, :]
    h = (_rms(x, nw_ref[...], D_MODEL) * (1.0 + scale) + shift).astype(BF16)

    mla_rope = (mc_ref[...], msa_ref[...], msb_ref[...]) if rope else None

    a = _dot(h, wl_ref[...])
    qn = _rms(a[:, OFF_QLAT:OFF_CKV], qnw_ref[...], Q_LORA).astype(BF16)
    ckv = _rms(a[:, OFF_CKV:OFF_KROPE], kvw_ref[...], KV_LORA)
    kr = a[:, OFF_KROPE:LAT_COLS]
    if emit_cache:
        ckv_ref[0] = ckv
        kr_ref[0] = kr[:, :ROPE_DIM]

    q_all = _dot(qn, wuq_ref[...])
    qkq = qkq_ref[...]
    for hh in range(MLA_HEADS):
        qh = q_all[:, hh * QK_PAD:(hh + 1) * QK_PAD]
        inv = lax.rsqrt(jnp.sum(qh * qh, axis=-1, keepdims=True) * (1.0 / QK_HEAD) + EPS)
        qh = qh * inv * qkq
        r = qh[:, LANE:]
        if rope:
            r = _mla_rope(r, *mla_rope)
        q_ref[0, hh, :, :LANE] = qh[:, :LANE].astype(BF16)
        q_ref[0, hh, :, LANE:] = r.astype(BF16)

    _keys_values(ckv, kr, wuk_ref, wuvt_ref, kkn_ref[...], kkr_ref[...], mla_rope, k_ref, vt_ref)

    ga_ref[0] = _silu(_dot(h, w_ref[:, OFF_GA:OFF_RQ])).astype(BF16)
    gb_ref[0] = _silu(_dot(h, w_ref[:, OFF_GB:WIDE_COLS])).astype(BF16)

    rqk = _dot(h, w_ref[:, OFF_RQ:OFF_RV])
    for hh in range(2 * RET_HEADS):
        t = rqk[:, hh * RET_DK:(hh + 1) * RET_DK]
        if hh >= RET_HEADS:
            t = t * (RET_DK ** -0.5)
        if rope:
            t = t * rc_ref[...] + pltpu.roll(t, RET_DK // 2, 1) * rs_ref[...]
        if hh < RET_HEADS:
            rq_ref[0, :, hh * RET_DK:(hh + 1) * RET_DK] = t.astype(BF16)
        else:
            g = hh - RET_HEADS
            rk_ref[0, :, g * RET_DK:(g + 1) * RET_DK] = t.astype(BF16)
    rv_ref[0] = _dot(h, w_ref[:, OFF_RV:OFF_GB]).astype(BF16)


def _const_spec(shape):
    return pl.BlockSpec(shape, lambda b, i: (0,) * len(shape), pipeline_mode=pl.Buffered(1))


def _in_call(x, mod3, ctx_mod, wts, rope_tabs, tm):
    B, L, _ = x.shape
    rope = rope_tabs is not None
    emit_cache = not rope
    mod_map = (lambda b, i: (MOD_ROWS - 1, 0, 0)) if ctx_mod else (lambda b, i: (b, 0, 0))
    in_specs = [pl.BlockSpec((1, tm, D_MODEL), lambda b, i: (b, i, 0)),
                pl.BlockSpec((1, 3, D_MODEL), mod_map),
                _const_spec((1, D_MODEL)),
                _const_spec((D_MODEL, LAT_COLS)),
                _const_spec((D_MODEL, WIDE_COLS)),
                _const_spec((1, Q_LORA)),
                _const_spec((Q_LORA, MLA_HEADS * QK_PAD)),
                _const_spec((1, QK_PAD)),
                _const_spec((1, KV_LORA)),
                _const_spec((KV_LORA, MLA_HEADS * MLA_NOPE)),
                _const_spec((MLA_W, KV_LORA)),
                _const_spec((1, LANE)),
                _const_spec((1, LANE))]
    args = [x, mod3, wts["norm_w"], wts["w_lat"], wts["w_wide"], wts["q_norm_w"], wts["w_uq"], wts["qk_q_w"],
            wts["kv_norm_w"], wts["w_uk"], wts["w_uvt"], wts["qk_k_n"], wts["qk_k_r"]]
    if rope:
        in_specs += [pl.BlockSpec((tm, LANE), lambda b, i: (i, 0))] * 5
        args += list(rope_tabs)
    tok = lambda w: pl.BlockSpec((1, tm, w), lambda b, i: (b, i, 0))
    out_specs = [pl.BlockSpec((1, MLA_HEADS, tm, QK_PAD), lambda b, i: (b, 0, i, 0)),
                 pl.BlockSpec((1, MLA_HEADS, tm, QK_PAD), lambda b, i: (b, 0, i, 0)),
                 pl.BlockSpec((1, MLA_HEADS, V_AUG, tm), lambda b, i: (b, 0, 0, i)),
                 tok(MLA_W), tok(RET_QK), tok(RET_QK), tok(RET_W), tok(RET_W)]
    out_shape = [jax.ShapeDtypeStruct((B, MLA_HEADS, L, QK_PAD), BF16),
                 jax.ShapeDtypeStruct((B, MLA_HEADS, L, QK_PAD), BF16),
                 jax.ShapeDtypeStruct((B, MLA_HEADS, V_AUG, L), BF16),
                 jax.ShapeDtypeStruct((B, L, MLA_W), BF16),
                 jax.ShapeDtypeStruct((B, L, RET_QK), BF16),
                 jax.ShapeDtypeStruct((B, L, RET_QK), BF16),
                 jax.ShapeDtypeStruct((B, L, RET_W), BF16),
                 jax.ShapeDtypeStruct((B, L, RET_W), BF16)]
    if emit_cache:
        out_specs += [tok(KV_LORA), tok(ROPE_DIM)]
        out_shape += [jax.ShapeDtypeStruct((B, L, KV_LORA), F32),
                      jax.ShapeDtypeStruct((B, L, ROPE_DIM), F32)]
    return pl.pallas_call(
        functools.partial(_in_kernel, rope=rope, emit_cache=emit_cache),
        grid=(B, L // tm),
        in_specs=in_specs, out_specs=out_specs, out_shape=out_shape,
        compiler_params=pltpu.CompilerParams(dimension_semantics=("parallel", "parallel"),
                                             vmem_limit_bytes=VMEM_LIMIT),
        name="in_proj_rope" if rope else "in_proj",
    )(*args)


def _ctx_kernel(ckv_ref, kr_ref, wuk_ref, wuvt_ref, kkn_ref, kkr_ref, k_ref, vt_ref):
    _keys_values(ckv_ref[0], kr_ref[0], wuk_ref, wuvt_ref, kkn_ref[...], kkr_ref[...], None,
                 k_ref, vt_ref)


def _ctx_call(ckv, kr_pad, wts):
    B, P, _ = ckv.shape
    cs = lambda shape: pl.BlockSpec(shape, lambda b: (0,) * len(shape))
    return pl.pallas_call(
        _ctx_kernel,
        grid=(B,),
        in_specs=[pl.BlockSpec((1, P, KV_LORA), lambda b: (b, 0, 0)),
                  pl.BlockSpec((1, P, LANE), lambda b: (b, 0, 0)),
                  cs((KV_LORA, MLA_HEADS * MLA_NOPE)), cs((MLA_W, KV_LORA)),
                  cs((1, LANE)), cs((1, LANE))],
        out_specs=[pl.BlockSpec((1, MLA_HEADS, P, QK_PAD), lambda b: (b, 0, 0, 0)),
                   pl.BlockSpec((1, MLA_HEADS, V_AUG, P), lambda b: (b, 0, 0, 0))],
        out_shape=[jax.ShapeDtypeStruct((B, MLA_HEADS, P, QK_PAD), BF16),
                   jax.ShapeDtypeStruct((B, MLA_HEADS, V_AUG, P), BF16)],
        compiler_params=pltpu.CompilerParams(dimension_semantics=("parallel",),
                                             vmem_limit_bytes=VMEM_LIMIT),
        name="ctx_kv",
    )(ckv, kr_pad, wts["w_uk"], wts["w_uvt"], wts["qk_k_n"], wts["qk_k_r"])


def _attn_kernel(*refs, heads, **kw):
    def first_scores(hh):
        if hh < heads:
            _attn_head(hh, *refs, phase="first", **kw)

    first_scores(0)
    sums = [_attn_head(hh, *refs, phase="fast", hook=functools.partial(first_scores, hh + 1),
                       **kw) for hh in range(heads)]
    worst = functools.reduce(jnp.maximum, sums)

    @pl.when(jnp.logical_not(jnp.max(worst) < OVERFLOW_GUARD))
    def _():
        for hh in range(heads):
            _attn_head(hh, *refs, phase="robust", **kw)


def _attn_head(hh, *refs, tk, n_chunks, has_ctx, phase, hook=None):
    if has_ctx:
        q_ref, k_ref, vt_ref, kc_ref, vtc_ref, ga_ref, o_ref, s_scr = refs
    else:
        q_ref, k_ref, vt_ref, ga_ref, o_ref, s_scr = refs
    q = q_ref[0, hh]
    tq = q.shape[0]
    cols = slice(hh * MLA_V, (hh + 1) * MLA_V)

    def scores(kc):
        return lax.dot_general(kc, q, NT_DIMS, preferred_element_type=F32)

    def chunk(j):
        s0 = pl.multiple_of(j * tk, tk)
        return k_ref[0, hh, pl.ds(s0, tk), :], vt_ref[0, hh, :, pl.ds(s0, tk)]

    def finish(acc):
        out = (acc[:MLA_V] * (1.0 / acc[MLA_V:MLA_V + 1])).T
        o_ref[0, :, cols] = (out * ga_ref[0, :, cols].astype(F32)).astype(BF16)

    if phase == "first":
        s_scr[hh, 0] = scores(k_ref[0, hh, 0:tk, :])
        return None

    if phase == "robust":
        def step(kc, vtc, carry):
            m, acc = carry
            st = scores(kc)
            m_new = jnp.maximum(m, jnp.max(st, axis=0, keepdims=True))
            p = jnp.exp2(st - m_new).astype(BF16)
            return m_new, jnp.exp2(m - m_new) * acc + _dot(vtc, p)

        carry = (jnp.full((1, tq), -1e30, F32), jnp.zeros((V_AUG, tq), F32))
        carry = lax.fori_loop(0, n_chunks, lambda j, c: step(*chunk(j), c), carry)
        if has_ctx:
            carry = step(kc_ref[0, hh], vtc_ref[0, hh], carry)
        finish(carry[1])
        return None

    m_fix = jnp.max(s_scr[hh, 0], axis=0, keepdims=True)

    def weights(st):
        return jnp.exp2(st - m_fix).astype(BF16)

    acc = jnp.zeros((V_AUG, tq), F32)
    for j in range(n_chunks):
        if j + 1 < n_chunks:
            s_scr[hh, (j + 1) % 2] = scores(k_ref[0, hh, (j + 1) * tk:(j + 2) * tk, :])
        elif has_ctx:
            st_ctx = scores(kc_ref[0, hh])
        if j + 1 == n_chunks and not has_ctx:
            hook()
        acc = acc + _dot(vt_ref[0, hh, :, j * tk:(j + 1) * tk], weights(s_scr[hh, j % 2]))
    if has_ctx:
        hook()
        acc = acc + _dot(vtc_ref[0, hh], weights(st_ctx))
    finish(acc)
    return acc[MLA_V:MLA_V + 1]


def _attn_call(q, k, vt, ga, ctx, tq, tk, heads):
    B, H, L, _ = q.shape
    has_ctx = ctx is not None
    in_specs = [pl.BlockSpec((1, heads, tq, QK_PAD), lambda b, h, i: (b, h, i, 0)),
                pl.BlockSpec((1, heads, L, QK_PAD), lambda b, h, i: (b, h, 0, 0)),
                pl.BlockSpec((1, heads, V_AUG, L), lambda b, h, i: (b, h, 0, 0))]
    args = [q, k, vt]
    if has_ctx:
        kc, vtc = ctx
        P = kc.shape[2]
        in_specs += [pl.BlockSpec((1, heads, P, QK_PAD), lambda b, h, i: (b, h, 0, 0)),
                     pl.BlockSpec((1, heads, V_AUG, P), lambda b, h, i: (b, h, 0, 0))]
        args += [kc, vtc]
    in_specs.append(pl.BlockSpec((1, tq, heads * MLA_V), lambda b, h, i: (b, i, h)))
    args.append(ga)
    return pl.pallas_call(
        functools.partial(_attn_kernel, heads=heads, tk=tk, n_chunks=L // tk, has_ctx=has_ctx),
        grid=(B, H // heads, L // tq),
        in_specs=in_specs,
        out_specs=pl.BlockSpec((1, tq, heads * MLA_V), lambda b, h, i: (b, i, h)),
        out_shape=jax.ShapeDtypeStruct((B, L, MLA_W), BF16),
        scratch_shapes=[pltpu.VMEM((heads, 2, tk, tq), F32)],
        compiler_params=pltpu.CompilerParams(
            dimension_semantics=("parallel", "parallel", "parallel"),
            vmem_limit_bytes=VMEM_LIMIT),
        name="attn_ctx" if has_ctx else "attn",
    )(*args)


def _ret_kernel(*refs, heads, **kw):
    for hh in range(heads):
        _ret_head(hh, *refs, heads=heads, **kw)


def _ret_head(hh, *refs, heads, n_chunks, has_state, unroll):
    it = iter(refs)
    lgf_ref, lgb_ref = next(it), next(it)
    q_ref, k_ref, v_ref, gb_ref, gnw_ref = next(it), next(it), next(it), next(it), next(it)
    if has_state:
        s0f_ref, s0b_ref = next(it), next(it)
    out_ref = next(it)
    if not has_state:
        sf_ref, sb_ref = next(it), next(it)
    kv_scr, st_scr, o_scr, at_scr = next(it), next(it), next(it), next(it)

    C = RET_CHUNK
    kcols = slice(hh * RET_DK, (hh + 1) * RET_DK)
    vcols = slice(hh * RET_DV, (hh + 1) * RET_DV)
    hd = pl.program_id(1) * heads + hh
    lgf = lgf_ref[hd]
    lgb = lgb_ref[hd]
    ri = lax.broadcasted_iota(jnp.int32, (C, C), 0).astype(F32)
    ci = lax.broadcasted_iota(jnp.int32, (C, C), 1).astype(F32)
    diff = ri - ci
    low = diff >= 0
    mask = (jnp.where(low, jnp.exp(jnp.where(low, diff, 0.0) * lgf), 0.0)
            + jnp.where(low, 0.0, jnp.exp(jnp.where(low, 0.0, -diff) * lgb)))
    qd_f = jnp.exp((ri + 1.0) * lgf)
    kd_f = jnp.exp((C - 1.0 - ri) * lgf)
    qd_b = jnp.exp((C - ri) * lgb)
    kd_b = jnp.exp(ri * lgb)
    cd_f = jnp.exp(jnp.full((RET_DK, RET_DV), C, F32) * lgf)
    cd_b = jnp.exp(jnp.full((RET_DK, RET_DV), C, F32) * lgb)
    gnw = gnw_ref[hh]

    def rows(n):
        return pl.ds(n * C if isinstance(n, int) else pl.multiple_of(n * C, C), C)

    def loop(lo, hi, body, carry):
        if hi - lo <= 2:
            for i in range(lo, hi):
                carry = body(i, carry)
            return carry
        return lax.fori_loop(lo, hi, body, carry)

    trips = n_chunks // unroll

    def decayed_keys(i):
        for u in range(unroll):
            n = i * unroll + u
            kc = k_ref[0, rows(n), kcols].astype(F32)
            a = jnp.concatenate([kc * kd_f, kc * kd_b], axis=1)
            at_scr[hh, n] = a.T.astype(BF16)

    def increments(i):
        for u in range(unroll):
            n = i * unroll + u
            kv_scr[hh, n] = _dot(at_scr[hh, n], v_ref[0, rows(n), vcols])

    def summarise(i, _):
        increments(i - 1)
        decayed_keys(i)
        return 0

    decayed_keys(0)
    loop(1, trips, summarise, 0)
    increments(trips - 1)

    if has_state:
        init = (s0f_ref[0, hh], s0b_ref[0, hh])
    else:
        init = (jnp.zeros((RET_DK, RET_DV), F32),) * 2

    def scan(i, carry):
        sf, sb = carry
        nb = n_chunks - 1 - i
        st_scr[hh, i, :RET_DK, :] = sf.astype(BF16)
        st_scr[hh, nb, RET_DK:, :] = sb.astype(BF16)
        return (sf * cd_f + kv_scr[hh, i, :RET_DK, :], sb * cd_b + kv_scr[hh, nb, RET_DK:, :])

    sf, sb = loop(0, n_chunks, scan, init)
    if not has_state:
        sf_ref[0, hh] = sf
        sb_ref[0, hh] = sb

    def products(i):
        ns = [i * unroll + u for u in range(unroll)]
        sls = [rows(n) for n in ns]
        qs = [q_ref[0, sl, kcols] for sl in sls]
        ss = [lax.dot_general(q, k_ref[0, sl, kcols], NT_DIMS, preferred_element_type=F32)
              for q, sl in zip(qs, sls)]
        for n, sl, q, s in zip(ns, sls, qs, ss):
            qf = q.astype(F32)
            qq = jnp.concatenate([qf * qd_f, qf * qd_b], axis=1).astype(BF16)
            o_scr[hh, sl, :] = (_dot((s * mask).astype(BF16), v_ref[0, sl, vcols])
                                + _dot(qq, st_scr[hh, n]))

    def normalise(i):
        for u in range(unroll):
            sl = rows(i * unroll + u)
            y = _rms(o_scr[hh, sl, :], gnw, RET_DV) * gb_ref[0, sl, vcols].astype(F32)
            out_ref[0, sl, vcols] = y.astype(BF16)

    def both(i, _):
        normalise(i - 1)
        products(i)
        return 0

    products(0)
    loop(1, trips, both, 0)
    normalise(trips - 1)


def _ret_call(lgf, lgb, rq, rk, rv, gb, gnw, states, heads):
    B, L, _ = rq.shape
    n_chunks = L // RET_CHUNK
    has_state = states is not None
    smem = pl.BlockSpec(memory_space=pltpu.SMEM)
    st_spec = pl.BlockSpec((1, heads, RET_DK, RET_DV), lambda b, h: (b, h, 0, 0))
    in_specs = [smem, smem,
                pl.BlockSpec((1, L, heads * RET_DK), lambda b, h: (b, 0, h)),
                pl.BlockSpec((1, L, heads * RET_DK), lambda b, h: (b, 0, h)),
                pl.BlockSpec((1, L, heads * RET_DV), lambda b, h: (b, 0, h)),
                pl.BlockSpec((1, L, heads * RET_DV), lambda b, h: (b, 0, h)),
                pl.BlockSpec((heads, 1, RET_DV), lambda b, h: (h, 0, 0))]
    args = [lgf, lgb, rq, rk, rv, gb, gnw]
    out_specs = [pl.BlockSpec((1, L, heads * RET_DV), lambda b, h: (b, 0, h))]
    out_shape = [jax.ShapeDtypeStruct((B, L, RET_W), BF16)]
    if has_state:
        in_specs += [st_spec, st_spec]
        args += list(states)
    else:
        out_specs += [st_spec, st_spec]
        out_shape += [jax.ShapeDtypeStruct((B, RET_HEADS, RET_DK, RET_DV), F32)] * 2
    return pl.pallas_call(
        functools.partial(_ret_kernel, heads=heads, n_chunks=n_chunks, has_state=has_state,
                          unroll=math.gcd(n_chunks, 8)),
        grid=(B, RET_HEADS // heads),
        in_specs=in_specs, out_specs=out_specs, out_shape=out_shape,
        scratch_shapes=[pltpu.VMEM((heads, n_chunks, 2 * RET_DK, RET_DV), F32),
                        pltpu.VMEM((heads, n_chunks, 2 * RET_DK, RET_DV), BF16),
                        pltpu.VMEM((heads, L, RET_DV), F32),
                        pltpu.VMEM((heads, n_chunks, 2 * RET_DK, RET_CHUNK), BF16)],
        compiler_params=pltpu.CompilerParams(dimension_semantics=("parallel", "parallel"),
                                             vmem_limit_bytes=VMEM_LIMIT),
        name="ret_state" if has_state else "ret",
    )(*args)


def _out_kernel(ma_ref, mb_ref, w_ref, x_ref, mod_ref, o_ref):
    acc = _dot(ma_ref[0], w_ref[:MLA_W, :]) + _dot(mb_ref[0], w_ref[MLA_W:, :])
    o_ref[0] = x_ref[0] + mod_ref[0, 2:3, :] * acc


def _out_call(mix_a, mix_b, w_out, x, mod3, ctx_mod, tm):
    B, L, _ = x.shape
    mod_map = (lambda b, i: (MOD_ROWS - 1, 0, 0)) if ctx_mod else (lambda b, i: (b, 0, 0))
    return pl.pallas_call(
        _out_kernel,
        grid=(B, L // tm),
        in_specs=[pl.BlockSpec((1, tm, MLA_W), lambda b, i: (b, i, 0)),
                  pl.BlockSpec((1, tm, RET_W), lambda b, i: (b, i, 0)),
                  _const_spec((D_MODEL, D_MODEL)),
                  pl.BlockSpec((1, tm, D_MODEL), lambda b, i: (b, i, 0)),
                  pl.BlockSpec((1, 3, D_MODEL), mod_map)],
        out_specs=pl.BlockSpec((1, tm, D_MODEL), lambda b, i: (b, i, 0)),
        out_shape=jax.ShapeDtypeStruct((B, L, D_MODEL), F32),
        compiler_params=pltpu.CompilerParams(dimension_semantics=("parallel", "parallel"),
                                             vmem_limit_bytes=VMEM_LIMIT),
        name="out_proj",
    )(mix_a, mix_b, w_out, x, mod3)


def _rope_angles(pos, dim):
    half = dim // 2
    freqs = ROPE_BASE ** (-jnp.arange(half, dtype=F32) / half)
    return pos.astype(F32)[:, None] * freqs[None, :]


def _rope_tables(L):
    rows = L // GRID_W
    ar = _rope_angles(jnp.arange(rows), ROPE_DIM // 2)
    ac = _rope_angles(jnp.arange(GRID_W), ROPE_DIM // 2)
    cr, sr, cc, sc = jnp.cos(ar), jnp.sin(ar), jnp.cos(ac), jnp.sin(ac)
    zr, zc = jnp.zeros_like(sr), jnp.zeros_like(sc)

    def grid_table(per_row, per_col):
        w = per_row.shape[-1]
        pad = jnp.zeros((rows, GRID_W, LANE - 2 * w), F32)
        t = jnp.concatenate([jnp.broadcast_to(per_row[:, None, :], (rows, GRID_W, w)),
                             jnp.broadcast_to(per_col[None, :, :], (rows, GRID_W, w)), pad],
                            axis=-1)
        return t.reshape(L, LANE)

    cat = lambda a, b: jnp.concatenate([a, b], axis=-1)
    mc = grid_table(cat(cr, cr), cat(cc, cc))
    msa = grid_table(cat(-sr, zr), cat(-sc, zc))
    msb = grid_table(cat(zr, sr), cat(zc, sc))

    hi = _rope_angles(jnp.arange(rows) * GRID_W, RET_DK)[:, None, :]
    lo = _rope_angles(jnp.arange(GRID_W), RET_DK)[None, :, :]
    cos_t = jnp.cos(hi) * jnp.cos(lo) - jnp.sin(hi) * jnp.sin(lo)
    sin_t = jnp.sin(hi) * jnp.cos(lo) + jnp.cos(hi) * jnp.sin(lo)
    rc = cat(cos_t, cos_t).reshape(L, LANE)
    rs = cat(-sin_t, sin_t).reshape(L, LANE)
    return mc, msa, msb, rc, rs


def _prep_weights(l, norm_w, w_in, q_norm_w, w_uq, kv_norm_w, w_uk, w_uv, qk_q_w, qk_k_w):
    wi = w_in[l]
    w_lat = jnp.pad(wi[:, :LAT_SRC_COLS], ((0, 0), (0, LAT_COLS - LAT_SRC_COLS))).astype(BF16)
    w_wide = wi[:, LAT_SRC_COLS:].astype(BF16)
    wq = w_uq[l].reshape(Q_LORA, MLA_HEADS, QK_HEAD)
    wq = jnp.pad(wq, ((0, 0), (0, 0), (0, QK_PAD - QK_HEAD))).reshape(Q_LORA, MLA_HEADS * QK_PAD)
    qscale = (QK_HEAD ** -0.5) * math.log2(math.e)
    qkq = jnp.pad(qk_q_w[l] * qscale, (0, QK_PAD - QK_HEAD))[None, :]
    kk = qk_k_w[l]
    return {
        "norm_w": norm_w[l][None, :],
        "w_lat": w_lat,
        "w_wide": w_wide,
        "q_norm_w": q_norm_w[l][None, :],
        "w_uq": wq.astype(BF16),
        "qk_q_w": qkq,
        "kv_norm_w": kv_norm_w[l][None, :],
        "w_uk": w_uk[l].astype(BF16),
        "w_uvt": w_uv[l].T.astype(BF16),
        "qk_k_n": kk[None, :MLA_NOPE],
        "qk_k_r": jnp.pad(kk[MLA_NOPE:], (0, LANE - ROPE_DIM))[None, :],
    }


def _tiles(L):
    tm = min(L, 256)
    tm_out = min(L, 512)
    tq = min(L, 1024)
    tk = min(L, 1024)
    heads = MLA_HEADS if L <= 512 else 4
    ret_heads = RET_HEADS if L <= 512 else 1
    return tm, tm_out, tq, tk, heads, ret_heads


def _layer(x, mod3, ctx_mod, wts, w_out, lgf, lgb, gnw, rope_tabs, ctx, states):
    tm, tm_out, tq, tk, heads, ret_heads = _tiles(x.shape[1])
    outs = _in_call(x, mod3, ctx_mod, wts, rope_tabs, tm)
    q, k, vt, ga, rq, rk, rv, gb = outs[:8]
    mix_a = _attn_call(q, k, vt, ga, ctx, tq, tk, heads)
    ret = _ret_call(lgf, lgb, rq, rk, rv, gb, gnw, states, ret_heads)
    y = _out_call(mix_a, ret[0], w_out, x, mod3, ctx_mod, tm_out)
    return y, outs[8:], ret[1:]


def kernel(x_prompt, x_sample, c, cache_mla_ckv, cache_mla_krope, state_ret_fwd, state_ret_bwd,
           c_ctx, norm_w, w_mod, b_mod, w_in, mla_q_norm_w, mla_w_uq, mla_kv_norm_w, mla_w_uk,
           mla_w_uv, mla_qk_q_w, mla_qk_k_w, ret_log_decay_fwd, ret_log_decay_bwd, ret_gn_w, w_out):
    depth = w_in.shape[0]
    dec_b = x_sample.shape[0]
    assert dec_b < MOD_ROWS
    cvec = jnp.zeros((MOD_ROWS, D_MODEL), F32).at[:dec_b].set(c).at[MOD_ROWS - 1].set(c_ctx)
    rope_tabs = _rope_tables(x_sample.shape[1])

    xp, xs = x_prompt, x_sample
    ckv_l, kr_l, sf_l, sb_l = [], [], [], []
    for l in range(depth):
        wts = _prep_weights(l, norm_w, w_in, mla_q_norm_w, mla_w_uq, mla_kv_norm_w, mla_w_uk,
                            mla_w_uv, mla_qk_q_w, mla_qk_k_w)
        w_out_b = w_out[l].astype(BF16)
        lgf = -jnp.exp(ret_log_decay_fwd[l].astype(F32))
        lgb = -jnp.exp(ret_log_decay_bwd[l].astype(F32))
        gnw = ret_gn_w[l][:, None, :]
        mod3 = _mod_call(cvec, w_mod[l], b_mod[l][None, :]).reshape(MOD_ROWS, 3, D_MODEL)

        xp, (ckv, kr), (sf, sb) = _layer(xp, mod3, True, wts, w_out_b, lgf, lgb, gnw,
                                         None, None, None)
        ckv_l.append(ckv)
        kr_l.append(kr)
        sf_l.append(sf)
        sb_l.append(sb)

        kr_pad = jnp.pad(cache_mla_krope[:, l], ((0, 0), (0, 0), (0, LANE - ROPE_DIM)))
        ctx = _ctx_call(cache_mla_ckv[:, l], kr_pad, wts)
        xs, _, _ = _layer(xs, mod3, False, wts, w_out_b, lgf, lgb, gnw, rope_tabs, ctx,
                          (state_ret_fwd[:, l], state_ret_bwd[:, l]))

    return (xp, xs, jnp.stack(ckv_l, axis=1), jnp.stack(kr_l, axis=1),
            jnp.stack(sf_l, axis=1), jnp.stack(sb_l, axis=1))
```

```python
import functools
import math

import jax
import jax.numpy as jnp
from jax import lax
from jax.experimental import pallas as pl
from jax.experimental.pallas import tpu as pltpu

F32 = jnp.float32
BF16 = jnp.bfloat16

D_MODEL = 2048
GRID_W = 64
MLA_W = 1024
RET_W = 1024
MLA_NOPE = 128
ROPE_DIM = 64
QK_HEAD = MLA_NOPE + ROPE_DIM
MLA_V = 128
MLA_HEADS = 8
Q_LORA = 384
KV_LORA = 256
RET_DV = 256
RET_DK = 128
RET_HEADS = 4
RET_QK = RET_HEADS * RET_DK
RET_CHUNK = 128
ROPE_BASE = 10000.0
EPS = 1e-6

LANE = 128
QK_PAD = 2 * LANE
V_AUG = MLA_V + 16
OVERFLOW_GUARD = 2.0 ** 100
MOD_ROWS = 16
VMEM_LIMIT = 56 * 1024 * 1024

OFF_QLAT = 0
OFF_CKV = OFF_QLAT + Q_LORA
OFF_KROPE = OFF_CKV + KV_LORA
LAT_COLS = OFF_KROPE + LANE
LAT_SRC_COLS = OFF_KROPE + ROPE_DIM
OFF_GA = 0
OFF_RQ = OFF_GA + MLA_W
OFF_RK = OFF_RQ + RET_QK
OFF_RV = OFF_RK + RET_QK
OFF_GB = OFF_RV + RET_W
WIDE_COLS = OFF_GB + RET_W

NT_DIMS = (((1,), (1,)), ((), ()))


def _silu(x):
    return x * (1.0 / (1.0 + jnp.exp(-x)))


def _rms(x, w, n):
    ms = jnp.sum(x * x, axis=-1, keepdims=True) * (1.0 / n)
    return x * lax.rsqrt(ms + EPS) * w


def _dot(a, b):
    return jnp.dot(a, b, preferred_element_type=F32)


def _mod_kernel(c_ref, w_ref, b_ref, o_ref):
    s = _silu(c_ref[...]).astype(BF16)
    o_ref[...] = _dot(s, w_ref[...].astype(BF16)) + b_ref[...]


def _mod_call(cvec, w_mod, b_mod):
    tn = 512
    n = w_mod.shape[1]
    return pl.pallas_call(
        _mod_kernel,
        grid=(n // tn,),
        in_specs=[pl.BlockSpec((MOD_ROWS, D_MODEL), lambda j: (0, 0)),
                  pl.BlockSpec((D_MODEL, tn), lambda j: (0, j)),
                  pl.BlockSpec((1, tn), lambda j: (0, j))],
        out_specs=pl.BlockSpec((MOD_ROWS, tn), lambda j: (0, j)),
        out_shape=jax.ShapeDtypeStruct((MOD_ROWS, n), F32),
        compiler_params=pltpu.CompilerParams(dimension_semantics=("arbitrary",),
                                             vmem_limit_bytes=VMEM_LIMIT),
        name="mod",
    )(cvec, w_mod, b_mod)


def _mla_rope(r, c, sa, sb):
    return r * c + pltpu.roll(r, LANE - 16, 1) * sa + pltpu.roll(r, 16, 1) * sb


def _keys_values(ckv, kr, w_uk_ref, w_uvt_ref, kkn, kkr, rope, k_ref, vt_ref):
    ckv_b = ckv.astype(BF16)
    kn_all = _dot(ckv_b, w_uk_ref[...])
    krw = kr * kkr
    if rope is not None:
        krw = _mla_rope(krw, *rope)
    kr_ss = jnp.sum(kr * kr, axis=-1, keepdims=True)
    for h in range(MLA_HEADS):
        kn = kn_all[:, h * MLA_NOPE:(h + 1) * MLA_NOPE]
        ss = jnp.sum(kn * kn, axis=-1, keepdims=True) + kr_ss
        inv = lax.rsqrt(ss * (1.0 / QK_HEAD) + EPS)
        k_ref[0, h, :, :LANE] = (kn * inv * kkn).astype(BF16)
        k_ref[0, h, :, LANE:] = (krw * inv).astype(BF16)
    vt = lax.dot_general(w_uvt_ref[...], ckv_b, NT_DIMS, preferred_element_type=F32)
    ones_row = jnp.where(lax.broadcasted_iota(jnp.int32, (V_AUG - MLA_V, vt.shape[1]), 0) == 0,
                         1.0, 0.0).astype(BF16)
    for h in range(MLA_HEADS):
        vt_ref[0, h, :MLA_V, :] = vt[h * MLA_V:(h + 1) * MLA_V].astype(BF16)
        vt_ref[0, h, MLA_V:, :] = ones_row


def _in_kernel(*refs, rope, emit_cache):
    it = iter(refs)
    x_ref, mod_ref, nw_ref, wl_ref, w_ref = next(it), next(it), next(it), next(it), next(it)
    qnw_ref, wuq_ref, qkq_ref = next(it), next(it), next(it)
    kvw_ref, wuk_ref, wuvt_ref, kkn_ref, kkr_ref = next(it), next(it), next(it), next(it), next(it)
    if rope:
        mc_ref, msa_ref, msb_ref, rc_ref, rs_ref = next(it), next(it), next(it), next(it), next(it)
    q_ref, k_ref, vt_ref, ga_ref = next(it), next(it), next(it), next(it)
    rq_ref, rk_ref, rv_ref, gb_ref = next(it), next(it), next(it), next(it)
    if emit_cache:
        ckv_ref, kr_ref = next(it), next(it)

    x = x_ref[0]
    shift = mod_ref[0, 0:1, :]
    scale = mod_ref[0, 1:2, :]
    h = (_rms(x, nw_ref[...], D_MODEL) * (1.0 + scale) + shift).astype(BF16)

    mla_rope = (mc_ref[...], msa_ref[...], msb_ref[...]) if rope else None

    a = _dot(h, wl_ref[...])
    qn = _rms(a[:, OFF_QLAT:OFF_CKV], qnw_ref[...], Q_LORA).astype(BF16)
    ckv = _rms(a[:, OFF_CKV:OFF_KROPE], kvw_ref[...], KV_LORA)
    kr = a[:, OFF_KROPE:LAT_COLS]
    if emit_cache:
        ckv_ref[0] = ckv
        kr_ref[0] = kr[:, :ROPE_DIM]

    q_all = _dot(qn, wuq_ref[...])
    qkq = qkq_ref[...]
    for hh in range(MLA_HEADS):
        qh = q_all[:, hh * QK_PAD:(hh + 1) * QK_PAD]
        inv = lax.rsqrt(jnp.sum(qh * qh, axis=-1, keepdims=True) * (1.0 / QK_HEAD) + EPS)
        qh = qh * inv * qkq
        r = qh[:, LANE:]
        if rope:
            r = _mla_rope(r, *mla_rope)
        q_ref[0, hh, :, :LANE] = qh[:, :LANE].astype(BF16)
        q_ref[0, hh, :, LANE:] = r.astype(BF16)

    _keys_values(ckv, kr, wuk_ref, wuvt_ref, kkn_ref[...], kkr_ref[...], mla_rope, k_ref, vt_ref)

    ga_ref[0] = _silu(_dot(h, w_ref[:, OFF_GA:OFF_RQ])).astype(BF16)
    gb_ref[0] = _silu(_dot(h, w_ref[:, OFF_GB:WIDE_COLS])).astype(BF16)

    rqk = _dot(h, w_ref[:, OFF_RQ:OFF_RV])
    for hh in range(2 * RET_HEADS):
        t = rqk[:, hh * RET_DK:(hh + 1) * RET_DK]
        if hh >= RET_HEADS:
            t = t * (RET_DK ** -0.5)
        if rope:
            t = t * rc_ref[...] + pltpu.roll(t, RET_DK // 2, 1) * rs_ref[...]
        if hh < RET_HEADS:
            rq_ref[0, :, hh * RET_DK:(hh + 1) * RET_DK] = t.astype(BF16)
        else:
            g = hh - RET_HEADS
            rk_ref[0, :, g * RET_DK:(g + 1) * RET_DK] = t.astype(BF16)
    rv_ref[0] = _dot(h, w_ref[:, OFF_RV:OFF_GB]).astype(BF16)


def _const_spec(shape):
    return pl.BlockSpec(shape, lambda b, i: (0,) * len(shape), pipeline_mode=pl.Buffered(1))


def _in_call(x, mod3, ctx_mod, wts, rope_tabs, tm):
    B, L, _ = x.shape
    rope = rope_tabs is not None
    emit_cache = not rope
    mod_map = (lambda b, i: (MOD_ROWS - 1, 0, 0)) if ctx_mod else (lambda b, i: (b, 0, 0))
    in_specs = [pl.BlockSpec((1, tm, D_MODEL), lambda b, i: (b, i, 0)),
                pl.BlockSpec((1, 3, D_MODEL), mod_map),
                _const_spec((1, D_MODEL)),
                _const_spec((D_MODEL, LAT_COLS)),
                _const_spec((D_MODEL, WIDE_COLS)),
                _const_spec((1, Q_LORA)),
                _const_spec((Q_LORA, MLA_HEADS * QK_PAD)),
                _const_spec((1, QK_PAD)),
                _const_spec((1, KV_LORA)),
                _const_spec((KV_LORA, MLA_HEADS * MLA_NOPE)),
                _const_spec((MLA_W, KV_LORA)),
                _const_spec((1, LANE)),
                _const_spec((1, LANE))]
    args = [x, mod3, wts["norm_w"], wts["w_lat"], wts["w_wide"], wts["q_norm_w"], wts["w_uq"], wts["qk_q_w"],
            wts["kv_norm_w"], wts["w_uk"], wts["w_uvt"], wts["qk_k_n"], wts["qk_k_r"]]
    if rope:
        in_specs += [pl.BlockSpec((tm, LANE), lambda b, i: (i, 0))] * 5
        args += list(rope_tabs)
    tok = lambda w: pl.BlockSpec((1, tm, w), lambda b, i: (b, i, 0))
    out_specs = [pl.BlockSpec((1, MLA_HEADS, tm, QK_PAD), lambda b, i: (b, 0, i, 0)),
                 pl.BlockSpec((1, MLA_HEADS, tm, QK_PAD), lambda b, i: (b, 0, i, 0)),
                 pl.BlockSpec((1, MLA_HEADS, V_AUG, tm), lambda b, i: (b, 0, 0, i)),
                 tok(MLA_W), tok(RET_QK), tok(RET_QK), tok(RET_W), tok(RET_W)]
    out_shape = [jax.ShapeDtypeStruct((B, MLA_HEADS, L, QK_PAD), BF16),
                 jax.ShapeDtypeStruct((B, MLA_HEADS, L, QK_PAD), BF16),
                 jax.ShapeDtypeStruct((B, MLA_HEADS, V_AUG, L), BF16),
                 jax.ShapeDtypeStruct((B, L, MLA_W), BF16),
                 jax.ShapeDtypeStruct((B, L, RET_QK), BF16),
                 jax.ShapeDtypeStruct((B, L, RET_QK), BF16),
                 jax.ShapeDtypeStruct((B, L, RET_W), BF16),
                 jax.ShapeDtypeStruct((B, L, RET_W), BF16)]
    if emit_cache:
        out_specs += [tok(KV_LORA), tok(ROPE_DIM)]
        out_shape += [jax.ShapeDtypeStruct((B, L, KV_LORA), F32),
                      jax.ShapeDtypeStruct((B, L, ROPE_DIM), F32)]
    return pl.pallas_call(
        functools.partial(_in_kernel, rope=rope, emit_cache=emit_cache),
        grid=(B, L // tm),
        in_specs=in_specs, out_specs=out_specs, out_shape=out_shape,
        compiler_params=pltpu.CompilerParams(dimension_semantics=("parallel", "parallel"),
                                             vmem_limit_bytes=VMEM_LIMIT),
        name="in_proj_rope" if rope else "in_proj",
    )(*args)


def _ctx_kernel(ckv_ref, kr_ref, wuk_ref, wuvt_ref, kkn_ref, kkr_ref, k_ref, vt_ref):
    _keys_values(ckv_ref[0], kr_ref[0], wuk_ref, wuvt_ref, kkn_ref[...], kkr_ref[...], None,
                 k_ref, vt_ref)


def _ctx_call(ckv, kr_pad, wts):
    B, P, _ = ckv.shape
    cs = lambda shape: pl.BlockSpec(shape, lambda b: (0,) * len(shape))
    return pl.pallas_call(
        _ctx_kernel,
        grid=(B,),
        in_specs=[pl.BlockSpec((1, P, KV_LORA), lambda b: (b, 0, 0)),
                  pl.BlockSpec((1, P, LANE), lambda b: (b, 0, 0)),
                  cs((KV_LORA, MLA_HEADS * MLA_NOPE)), cs((MLA_W, KV_LORA)),
                  cs((1, LANE)), cs((1, LANE))],
        out_specs=[pl.BlockSpec((1, MLA_HEADS, P, QK_PAD), lambda b: (b, 0, 0, 0)),
                   pl.BlockSpec((1, MLA_HEADS, V_AUG, P), lambda b: (b, 0, 0, 0))],
        out_shape=[jax.ShapeDtypeStruct((B, MLA_HEADS, P, QK_PAD), BF16),
                   jax.ShapeDtypeStruct((B, MLA_HEADS, V_AUG, P), BF16)],
        compiler_params=pltpu.CompilerParams(dimension_semantics=("parallel",),
                                             vmem_limit_bytes=VMEM_LIMIT),
        name="ctx_kv",
    )(ckv, kr_pad, wts["w_uk"], wts["w_uvt"], wts["qk_k_n"], wts["qk_k_r"])


def _attn_kernel(*refs, heads, **kw):
    def first_scores(hh):
        if hh < heads:
            _attn_head(hh, *refs, phase="first", **kw)

    first_scores(0)
    sums = [_attn_head(hh, *refs, phase="fast", hook=functools.partial(first_scores, hh + 1),
                       **kw) for hh in range(heads)]
    worst = functools.reduce(jnp.maximum, sums)

    @pl.when(jnp.logical_not(jnp.max(worst) < OVERFLOW_GUARD))
    def _():
        for hh in range(heads):
            _attn_head(hh, *refs, phase="robust", **kw)


def _attn_head(hh, *refs, tk, n_chunks, has_ctx, phase, hook=None):
    if has_ctx:
        q_ref, k_ref, vt_ref, kc_ref, vtc_ref, ga_ref, o_ref, s_scr = refs
    else:
        q_ref, k_ref, vt_ref, ga_ref, o_ref, s_scr = refs
    q = q_ref[0, hh]
    tq = q.shape[0]
    cols = slice(hh * MLA_V, (hh + 1) * MLA_V)

    def scores(kc):
        return lax.dot_general(kc, q, NT_DIMS, preferred_element_type=F32)

    def chunk(j):
        s0 = pl.multiple_of(j * tk, tk)
        return k_ref[0, hh, pl.ds(s0, tk), :], vt_ref[0, hh, :, pl.ds(s0, tk)]

    def finish(acc):
        out = (acc[:MLA_V] * (1.0 / acc[MLA_V:MLA_V + 1])).T
        o_ref[0, :, cols] = (out * ga_ref[0, :, cols].astype(F32)).astype(BF16)

    if phase == "first":
        s_scr[hh, 0] = scores(k_ref[0, hh, 0:tk, :])
        return None

    if phase == "robust":
        def step(kc, vtc, carry):
            m, acc = carry
            st = scores(kc)
            m_new = jnp.maximum(m, jnp.max(st, axis=0, keepdims=True))
            p = jnp.exp2(st - m_new).astype(BF16)
            return m_new, jnp.exp2(m - m_new) * acc + _dot(vtc, p)

        carry = (jnp.full((1, tq), -1e30, F32), jnp.zeros((V_AUG, tq), F32))
        carry = lax.fori_loop(0, n_chunks, lambda j, c: step(*chunk(j), c), carry)
        if has_ctx:
            carry = step(kc_ref[0, hh], vtc_ref[0, hh], carry)
        finish(carry[1])
        return None

    m_fix = jnp.max(s_scr[hh, 0], axis=0, keepdims=True)

    def weights(st):
        return jnp.exp2(st - m_fix).astype(BF16)

    acc = jnp.zeros((V_AUG, tq), F32)
    for j in range(n_chunks):
        if j + 1 < n_chunks:
            s_scr[hh, (j + 1) % 2] = scores(k_ref[0, hh, (j + 1) * tk:(j + 2) * tk, :])
        elif has_ctx:
            st_ctx = scores(kc_ref[0, hh])
        if j + 1 == n_chunks and not has_ctx:
            hook()
        acc = acc + _dot(vt_ref[0, hh, :, j * tk:(j + 1) * tk], weights(s_scr[hh, j % 2]))
    if has_ctx:
        hook()
        acc = acc + _dot(vtc_ref[0, hh], weights(st_ctx))
    finish(acc)
    return acc[MLA_V:MLA_V + 1]


def _attn_call(q, k, vt, ga, ctx, tq, tk, heads):
    B, H, L, _ = q.shape
    has_ctx = ctx is not None
    in_specs = [pl.BlockSpec((1, heads, tq, QK_PAD), lambda b, h, i: (b, h, i, 0)),
                pl.BlockSpec((1, heads, L, QK_PAD), lambda b, h, i: (b, h, 0, 0)),
                pl.BlockSpec((1, heads, V_AUG, L), lambda b, h, i: (b, h, 0, 0))]
    args = [q, k, vt]
    if has_ctx:
        kc, vtc = ctx
        P = kc.shape[2]
        in_specs += [pl.BlockSpec((1, heads, P, QK_PAD), lambda b, h, i: (b, h, 0, 0)),
                     pl.BlockSpec((1, heads, V_AUG, P), lambda b, h, i: (b, h, 0, 0))]
        args += [kc, vtc]
    in_specs.append(pl.BlockSpec((1, tq, heads * MLA_V), lambda b, h, i: (b, i, h)))
    args.append(ga)
    return pl.pallas_call(
        functools.partial(_attn_kernel, heads=heads, tk=tk, n_chunks=L // tk, has_ctx=has_ctx),
        grid=(B, H // heads, L // tq),
        in_specs=in_specs,
        out_specs=pl.BlockSpec((1, tq, heads * MLA_V), lambda b, h, i: (b, i, h)),
        out_shape=jax.ShapeDtypeStruct((B, L, MLA_W), BF16),
        scratch_shapes=[pltpu.VMEM((heads, 2, tk, tq), F32)],
        compiler_params=pltpu.CompilerParams(
            dimension_semantics=("parallel", "parallel", "parallel"),
            vmem_limit_bytes=VMEM_LIMIT),
        name="attn_ctx" if has_ctx else "attn",
    )(*args)


def _ret_kernel(*refs, heads, **kw):
    for hh in range(heads):
        _ret_head(hh, *refs, heads=heads, **kw)


def _ret_head(hh, *refs, heads, n_chunks, has_state, unroll):
    it = iter(refs)
    lgf_ref, lgb_ref = next(it), next(it)
    q_ref, k_ref, v_ref, gb_ref, gnw_ref = next(it), next(it), next(it), next(it), next(it)
    if has_state:
        s0f_ref, s0b_ref = next(it), next(it)
    out_ref = next(it)
    if not has_state:
        sf_ref, sb_ref = next(it), next(it)
    kv_scr, st_scr, o_scr, at_scr = next(it), next(it), next(it), next(it)

    C = RET_CHUNK
    kcols = slice(hh * RET_DK, (hh + 1) * RET_DK)
    vcols = slice(hh * RET_DV, (hh + 1) * RET_DV)
    hd = pl.program_id(1) * heads + hh
    lgf = lgf_ref[hd]
    lgb = lgb_ref[hd]
    ri = lax.broadcasted_iota(jnp.int32, (C, C), 0).astype(F32)
    ci = lax.broadcasted_iota(jnp.int32, (C, C), 1).astype(F32)
    diff = ri - ci
    low = diff >= 0
    mask = (jnp.where(low, jnp.exp(jnp.where(low, diff, 0.0) * lgf), 0.0)
            + jnp.where(low, 0.0, jnp.exp(jnp.where(low, 0.0, -diff) * lgb)))
    qd_f = jnp.exp((ri + 1.0) * lgf)
    kd_f = jnp.exp((C - 1.0 - ri) * lgf)
    qd_b = jnp.exp((C - ri) * lgb)
    kd_b = jnp.exp(ri * lgb)
    cd_f = jnp.exp(jnp.full((RET_DK, RET_DV), C, F32) * lgf)
    cd_b = jnp.exp(jnp.full((RET_DK, RET_DV), C, F32) * lgb)
    gnw = gnw_ref[hh]

    def rows(n):
        return pl.ds(n * C if isinstance(n, int) else pl.multiple_of(n * C, C), C)

    def loop(lo, hi, body, carry):
        if hi - lo <= 2:
            for i in range(lo, hi):
                carry = body(i, carry)
            return carry
        return lax.fori_loop(lo, hi, body, carry)

    trips = n_chunks // unroll

    def decayed_keys(i):
        for u in range(unroll):
            n = i * unroll + u
            kc = k_ref[0, rows(n), kcols].astype(F32)
            a = jnp.concatenate([kc * kd_f, kc * kd_b], axis=1)
            at_scr[hh, n] = a.T.astype(BF16)

    def increments(i):
        for u in range(unroll):
            n = i * unroll + u
            kv_scr[hh, n] = _dot(at_scr[hh, n], v_ref[0, rows(n), vcols])

    def summarise(i, _):
        increments(i - 1)
        decayed_keys(i)
        return 0

    decayed_keys(0)
    loop(1, trips, summarise, 0)
    increments(trips - 1)

    if has_state:
        init = (s0f_ref[0, hh], s0b_ref[0, hh])
    else:
        init = (jnp.zeros((RET_DK, RET_DV), F32),) * 2

    def scan(i, carry):
        sf, sb = carry
        nb = n_chunks - 1 - i
        st_scr[hh, i, :RET_DK, :] = sf.astype(BF16)
        st_scr[hh, nb, RET_DK:, :] = sb.astype(BF16)
        return (sf * cd_f + kv_scr[hh, i, :RET_DK, :], sb * cd_b + kv_scr[hh, nb, RET_DK:, :])

    sf, sb = loop(0, n_chunks, scan, init)
    if not has_state:
        sf_ref[0, hh] = sf
        sb_ref[0, hh] = sb

    def products(i):
        ns = [i * unroll + u for u in range(unroll)]
        sls = [rows(n) for n in ns]
        qs = [q_ref[0, sl, kcols] for sl in sls]
        ss = [lax.dot_general(q, k_ref[0, sl, kcols], NT_DIMS, preferred_element_type=F32)
              for q, sl in zip(qs, sls)]
        for n, sl, q, s in zip(ns, sls, qs, ss):
            qf = q.astype(F32)
            qq = jnp.concatenate([qf * qd_f, qf * qd_b], axis=1).astype(BF16)
            o_scr[hh, sl, :] = (_dot((s * mask).astype(BF16), v_ref[0, sl, vcols])
                                + _dot(qq, st_scr[hh, n]))

    def normalise(i):
        for u in range(unroll):
            sl = rows(i * unroll + u)
            y = _rms(o_scr[hh, sl, :], gnw, RET_DV) * gb_ref[0, sl, vcols].astype(F32)
            out_ref[0, sl, vcols] = y.astype(BF16)

    def both(i, _):
        normalise(i - 1)
        products(i)
        return 0

    products(0)
    loop(1, trips, both, 0)
    normalise(trips - 1)


def _ret_call(lgf, lgb, rq, rk, rv, gb, gnw, states, heads):
    B, L, _ = rq.shape
    n_chunks = L // RET_CHUNK
    has_state = states is not None
    smem = pl.BlockSpec(memory_space=pltpu.SMEM)
    st_spec = pl.BlockSpec((1, heads, RET_DK, RET_DV), lambda b, h: (b, h, 0, 0))
    in_specs = [smem, smem,
                pl.BlockSpec((1, L, heads * RET_DK), lambda b, h: (b, 0, h)),
                pl.BlockSpec((1, L, heads * RET_DK), lambda b, h: (b, 0, h)),
                pl.BlockSpec((1, L, heads * RET_DV), lambda b, h: (b, 0, h)),
                pl.BlockSpec((1, L, heads * RET_DV), lambda b, h: (b, 0, h)),
                pl.BlockSpec((heads, 1, RET_DV), lambda b, h: (h, 0, 0))]
    args = [lgf, lgb, rq, rk, rv, gb, gnw]
    out_specs = [pl.BlockSpec((1, L, heads * RET_DV), lambda b, h: (b, 0, h))]
    out_shape = [jax.ShapeDtypeStruct((B, L, RET_W), BF16)]
    if has_state:
        in_specs += [st_spec, st_spec]
        args += list(states)
    else:
        out_specs += [st_spec, st_spec]
        out_shape += [jax.ShapeDtypeStruct((B, RET_HEADS, RET_DK, RET_DV), F32)] * 2
    return pl.pallas_call(
        functools.partial(_ret_kernel, heads=heads, n_chunks=n_chunks, has_state=has_state,
                          unroll=math.gcd(n_chunks, 8)),
        grid=(B, RET_HEADS // heads),
        in_specs=in_specs, out_specs=out_specs, out_shape=out_shape,
        scratch_shapes=[pltpu.VMEM((heads, n_chunks, 2 * RET_DK, RET_DV), F32),
                        pltpu.VMEM((heads, n_chunks, 2 * RET_DK, RET_DV), BF16),
                        pltpu.VMEM((heads, L, RET_DV), F32),
                        pltpu.VMEM((heads, n_chunks, 2 * RET_DK, RET_CHUNK), BF16)],
        compiler_params=pltpu.CompilerParams(dimension_semantics=("parallel", "parallel"),
                                             vmem_limit_bytes=VMEM_LIMIT),
        name="ret_state" if has_state else "ret",
    )(*args)


def _out_kernel(ma_ref, mb_ref, w_ref, x_ref, mod_ref, o_ref):
    acc = _dot(ma_ref[0], w_ref[:MLA_W, :]) + _dot(mb_ref[0], w_ref[MLA_W:, :])
    o_ref[0] = x_ref[0] + mod_ref[0, 2:3, :] * acc


def _out_call(mix_a, mix_b, w_out, x, mod3, ctx_mod, tm):
    B, L, _ = x.shape
    mod_map = (lambda b, i: (MOD_ROWS - 1, 0, 0)) if ctx_mod else (lambda b, i: (b, 0, 0))
    return pl.pallas_call(
        _out_kernel,
        grid=(B, L // tm),
        in_specs=[pl.BlockSpec((1, tm, MLA_W), lambda b, i: (b, i, 0)),
                  pl.BlockSpec((1, tm, RET_W), lambda b, i: (b, i, 0)),
                  _const_spec((D_MODEL, D_MODEL)),
                  pl.BlockSpec((1, tm, D_MODEL), lambda b, i: (b, i, 0)),
                  pl.BlockSpec((1, 3, D_MODEL), mod_map)],
        out_specs=pl.BlockSpec((1, tm, D_MODEL), lambda b, i: (b, i, 0)),
        out_shape=jax.ShapeDtypeStruct((B, L, D_MODEL), F32),
        compiler_params=pltpu.CompilerParams(dimension_semantics=("parallel", "parallel"),
                                             vmem_limit_bytes=VMEM_LIMIT),
        name="out_proj",
    )(mix_a, mix_b, w_out, x, mod3)


def _rope_angles(pos, dim):
    half = dim // 2
    freqs = ROPE_BASE ** (-jnp.arange(half, dtype=F32) / half)
    return pos.astype(F32)[:, None] * freqs[None, :]


def _rope_tables(L):
    rows = L // GRID_W
    ar = _rope_angles(jnp.arange(rows), ROPE_DIM // 2)
    ac = _rope_angles(jnp.arange(GRID_W), ROPE_DIM // 2)
    cr, sr, cc, sc = jnp.cos(ar), jnp.sin(ar), jnp.cos(ac), jnp.sin(ac)
    zr, zc = jnp.zeros_like(sr), jnp.zeros_like(sc)

    def grid_table(per_row, per_col):
        w = per_row.shape[-1]
        pad = jnp.zeros((rows, GRID_W, LANE - 2 * w), F32)
        t = jnp.concatenate([jnp.broadcast_to(per_row[:, None, :], (rows, GRID_W, w)),
                             jnp.broadcast_to(per_col[None, :, :], (rows, GRID_W, w)), pad],
                            axis=-1)
        return t.reshape(L, LANE)

    cat = lambda a, b: jnp.concatenate([a, b], axis=-1)
    mc = grid_table(cat(cr, cr), cat(cc, cc))
    msa = grid_table(cat(-sr, zr), cat(-sc, zc))
    msb = grid_table(cat(zr, sr), cat(zc, sc))

    hi = _rope_angles(jnp.arange(rows) * GRID_W, RET_DK)[:, None, :]
    lo = _rope_angles(jnp.arange(GRID_W), RET_DK)[None, :, :]
    cos_t = jnp.cos(hi) * jnp.cos(lo) - jnp.sin(hi) * jnp.sin(lo)
    sin_t = jnp.sin(hi) * jnp.cos(lo) + jnp.cos(hi) * jnp.sin(lo)
    rc = cat(cos_t, cos_t).reshape(L, LANE)
    rs = cat(-sin_t, sin_t).reshape(L, LANE)
    return mc, msa, msb, rc, rs


def _prep_weights(l, norm_w, w_in, q_norm_w, w_uq, kv_norm_w, w_uk, w_uv, qk_q_w, qk_k_w):
    wi = w_in[l]
    w_lat = jnp.pad(wi[:, :LAT_SRC_COLS], ((0, 0), (0, LAT_COLS - LAT_SRC_COLS))).astype(BF16)
    w_wide = wi[:, LAT_SRC_COLS:].astype(BF16)
    wq = w_uq[l].reshape(Q_LORA, MLA_HEADS, QK_HEAD)
    wq = jnp.pad(wq, ((0, 0), (0, 0), (0, QK_PAD - QK_HEAD))).reshape(Q_LORA, MLA_HEADS * QK_PAD)
    qscale = (QK_HEAD ** -0.5) * math.log2(math.e)
    qkq = jnp.pad(qk_q_w[l] * qscale, (0, QK_PAD - QK_HEAD))[None, :]
    kk = qk_k_w[l]
    return {
        "norm_w": norm_w[l][None, :],
        "w_lat": w_lat,
        "w_wide": w_wide,
        "q_norm_w": q_norm_w[l][None, :],
        "w_uq": wq.astype(BF16),
        "qk_q_w": qkq,
        "kv_norm_w": kv_norm_w[l][None, :],
        "w_uk": w_uk[l].astype(BF16),
        "w_uvt": w_uv[l].T.astype(BF16),
        "qk_k_n": kk[None, :MLA_NOPE],
        "qk_k_r": jnp.pad(kk[MLA_NOPE:], (0, LANE - ROPE_DIM))[None, :],
    }


def _tiles(L):
    tm = min(L, 256)
    tm_out = min(L, 512)
    tq = min(L, 2048)
    tk = min(L, 512)
    heads = MLA_HEADS if L <= 512 else 2
    ret_heads = RET_HEADS if L <= 512 else 1
    return tm, tm_out, tq, tk, heads, ret_heads


def _layer(x, mod3, ctx_mod, wts, w_out, lgf, lgb, gnw, rope_tabs, ctx, states):
    tm, tm_out, tq, tk, heads, ret_heads = _tiles(x.shape[1])
    outs = _in_call(x, mod3, ctx_mod, wts, rope_tabs, tm)
    q, k, vt, ga, rq, rk, rv, gb = outs[:8]
    mix_a = _attn_call(q, k, vt, ga, ctx, tq, tk, heads)
    ret = _ret_call(lgf, lgb, rq, rk, rv, gb, gnw, states, ret_heads)
    y = _out_call(mix_a, ret[0], w_out, x, mod3, ctx_mod, tm_out)
    return y, outs[8:], ret[1:]


def kernel(x_prompt, x_sample, c, cache_mla_ckv, cache_mla_krope, state_ret_fwd, state_ret_bwd,
           c_ctx, norm_w, w_mod, b_mod, w_in, mla_q_norm_w, mla_w_uq, mla_kv_norm_w, mla_w_uk,
           mla_w_uv, mla_qk_q_w, mla_qk_k_w, ret_log_decay_fwd, ret_log_decay_bwd, ret_gn_w, w_out):
    depth = w_in.shape[0]
    dec_b = x_sample.shape[0]
    assert dec_b < MOD_ROWS
    cvec = jnp.zeros((MOD_ROWS, D_MODEL), F32).at[:dec_b].set(c).at[MOD_ROWS - 1].set(c_ctx)
    rope_tabs = _rope_tables(x_sample.shape[1])

    xp, xs = x_prompt, x_sample
    ckv_l, kr_l, sf_l, sb_l = [], [], [], []
    for l in range(depth):
        wts = _prep_weights(l, norm_w, w_in, mla_q_norm_w, mla_w_uq, mla_kv_norm_w, mla_w_uk,
                            mla_w_uv, mla_qk_q_w, mla_qk_k_w)
        w_out_b = w_out[l].astype(BF16)
        lgf = -jnp.exp(ret_log_decay_fwd[l].astype(F32))
        lgb = -jnp.exp(ret_log_decay_bwd[l].astype(F32))
        gnw = ret_gn_w[l][:, None, :]
        mod3 = _mod_call(cvec, w_mod[l], b_mod[l][None, :]).reshape(MOD_ROWS, 3, D_MODEL)

        xp, (ckv, kr), (sf, sb) = _layer(xp, mod3, True, wts, w_out_b, lgf, lgb, gnw,
                                         None, None, None)
        ckv_l.append(ckv)
        kr_l.append(kr)
        sf_l.append(sf)
        sb_l.append(sb)

        kr_pad = jnp.pad(cache_mla_krope[:, l], ((0, 0), (0, 0), (0, LANE - ROPE_DIM)))
        ctx = _ctx_call(cache_mla_ckv[:, l], kr_pad, wts)
        xs, _, _ = _layer(xs, mod3, False, wts, w_out_b, lgf, lgb, gnw, rope_tabs, ctx,
                          (state_ret_fwd[:, l], state_ret_bwd[:, l]))

    return (xp, xs, jnp.stack(ckv_l, axis=1), jnp.stack(kr_l, axis=1),
            jnp.stack(sf_l, axis=1), jnp.stack(sb_l, axis=1))
```

```python
import functools
import math

import jax
import jax.numpy as jnp
from jax import lax
from jax.experimental import pallas as pl
from jax.experimental.pallas import tpu as pltpu

F32 = jnp.float32
BF16 = jnp.bfloat16

D_MODEL = 2048
GRID_W = 64
MLA_W = 1024
RET_W = 1024
MLA_NOPE = 128
ROPE_DIM = 64
QK_HEAD = MLA_NOPE + ROPE_DIM
MLA_V = 128
MLA_HEADS = 8
Q_LORA = 384
KV_LORA = 256
RET_DV = 256
RET_DK = 128
RET_HEADS = 4
RET_QK = RET_HEADS * RET_DK
RET_CHUNK = 128
ROPE_BASE = 10000.0
EPS = 1e-6

LANE = 128
QK_PAD = 2 * LANE
V_AUG = MLA_V + 16
OVERFLOW_GUARD = 2.0 ** 100
MOD_ROWS = 16
VMEM_LIMIT = 56 * 1024 * 1024

OFF_QLAT = 0
OFF_CKV = OFF_QLAT + Q_LORA
OFF_KROPE = OFF_CKV + KV_LORA
LAT_COLS = OFF_KROPE + LANE
LAT_SRC_COLS = OFF_KROPE + ROPE_DIM
OFF_GA = 0
OFF_RQ = OFF_GA + MLA_W
OFF_RK = OFF_RQ + RET_QK
OFF_RV = OFF_RK + RET_QK
OFF_GB = OFF_RV + RET_W
WIDE_COLS = OFF_GB + RET_W

NT_DIMS = (((1,), (1,)), ((), ()))


def _silu(x):
    return x * (1.0 / (1.0 + jnp.exp(-x)))


def _rms(x, w, n):
    ms = jnp.sum(x * x, axis=-1, keepdims=True) * (1.0 / n)
    return x * lax.rsqrt(ms + EPS) * w


def _dot(a, b):
    return jnp.dot(a, b, preferred_element_type=F32)


def _mod_kernel(c_ref, w_ref, b_ref, o_ref):
    s = _silu(c_ref[...]).astype(BF16)
    o_ref[...] = _dot(s, w_ref[...].astype(BF16)) + b_ref[...]


def _mod_call(cvec, w_mod, b_mod):
    tn = 512
    n = w_mod.shape[1]
    return pl.pallas_call(
        _mod_kernel,
        grid=(n // tn,),
        in_specs=[pl.BlockSpec((MOD_ROWS, D_MODEL), lambda j: (0, 0)),
                  pl.BlockSpec((D_MODEL, tn), lambda j: (0, j)),
                  pl.BlockSpec((1, tn), lambda j: (0, j))],
        out_specs=pl.BlockSpec((MOD_ROWS, tn), lambda j: (0, j)),
        out_shape=jax.ShapeDtypeStruct((MOD_ROWS, n), F32),
        compiler_params=pltpu.CompilerParams(dimension_semantics=("arbitrary",),
                                             vmem_limit_bytes=VMEM_LIMIT),
        name="mod",
    )(cvec, w_mod, b_mod)


def _mla_rope(r, c, sa, sb):
    return r * c + pltpu.roll(r, LANE - 16, 1) * sa + pltpu.roll(r, 16, 1) * sb


def _keys_values(ckv, kr, w_uk_ref, w_uvt_ref, kkn, kkr, rope, k_ref, vt_ref):
    ckv_b = ckv.astype(BF16)
    kn_all = _dot(ckv_b, w_uk_ref[...])
    krw = kr * kkr
    if rope is not None:
        krw = _mla_rope(krw, *rope)
    kr_ss = jnp.sum(kr * kr, axis=-1, keepdims=True)
    for h in range(MLA_HEADS):
        kn = kn_all[:, h * MLA_NOPE:(h + 1) * MLA_NOPE]
        ss = jnp.sum(kn * kn, axis=-1, keepdims=True) + kr_ss
        inv = lax.rsqrt(ss * (1.0 / QK_HEAD) + EPS)
        k_ref[0, h, :, :LANE] = (kn * inv * kkn).astype(BF16)
        k_ref[0, h, :, LANE:] = (krw * inv).astype(BF16)
    vt = lax.dot_general(w_uvt_ref[...], ckv_b, NT_DIMS, preferred_element_type=F32)
    ones_row = jnp.where(lax.broadcasted_iota(jnp.int32, (V_AUG - MLA_V, vt.shape[1]), 0) == 0,
                         1.0, 0.0).astype(BF16)
    for h in range(MLA_HEADS):
        vt_ref[0, h, :MLA_V, :] = vt[h * MLA_V:(h + 1) * MLA_V].astype(BF16)
        vt_ref[0, h, MLA_V:, :] = ones_row


def _in_kernel(*refs, rope, emit_cache):
    it = iter(refs)
    x_ref, mod_ref, nw_ref, wl_ref, w_ref = next(it), next(it), next(it), next(it), next(it)
    qnw_ref, wuq_ref, qkq_ref = next(it), next(it), next(it)
    kvw_ref, wuk_ref, wuvt_ref, kkn_ref, kkr_ref = next(it), next(it), next(it), next(it), next(it)
    if rope:
        mc_ref, msa_ref, msb_ref, rc_ref, rs_ref = next(it), next(it), next(it), next(it), next(it)
    q_ref, k_ref, vt_ref, ga_ref = next(it), next(it), next(it), next(it)
    rq_ref, rk_ref, rv_ref, gb_ref = next(it), next(it), next(it), next(it)
    if emit_cache:
        ckv_ref, kr_ref = next(it), next(it)

    x = x_ref[0]
    shift = mod_ref[0, 0:1, :]
    scale = mod_ref[0, 1:2, :]
    h = (_rms(x, nw_ref[...], D_MODEL) * (1.0 + scale) + shift).astype(BF16)

    mla_rope = (mc_ref[...], msa_ref[...], msb_ref[...]) if rope else None

    a = _dot(h, wl_ref[...])
    qn = _rms(a[:, OFF_QLAT:OFF_CKV], qnw_ref[...], Q_LORA).astype(BF16)
    ckv = _rms(a[:, OFF_CKV:OFF_KROPE], kvw_ref[...], KV_LORA)
    kr = a[:, OFF_KROPE:LAT_COLS]
    if emit_cache:
        ckv_ref[0] = ckv
        kr_ref[0] = kr[:, :ROPE_DIM]

    q_all = _dot(qn, wuq_ref[...])
    qkq = qkq_ref[...]
    for hh in range(MLA_HEADS):
        qh = q_all[:, hh * QK_PAD:(hh + 1) * QK_PAD]
        inv = lax.rsqrt(jnp.sum(qh * qh, axis=-1, keepdims=True) * (1.0 / QK_HEAD) + EPS)
        qh = qh * inv * qkq
        r = qh[:, LANE:]
        if rope:
            r = _mla_rope(r, *mla_rope)
        q_ref[0, hh, :, :LANE] = qh[:, :LANE].astype(BF16)
        q_ref[0, hh, :, LANE:] = r.astype(BF16)

    _keys_values(ckv, kr, wuk_ref, wuvt_ref, kkn_ref[...], kkr_ref[...], mla_rope, k_ref, vt_ref)

    ga_ref[0] = _silu(_dot(h, w_ref[:, OFF_GA:OFF_RQ])).astype(BF16)
    gb_ref[0] = _silu(_dot(h, w_ref[:, OFF_GB:WIDE_COLS])).astype(BF16)

    rqk = _dot(h, w_ref[:, OFF_RQ:OFF_RV])
    for hh in range(2 * RET_HEADS):
        t = rqk[:, hh * RET_DK:(hh + 1) * RET_DK]
        if hh >= RET_HEADS:
            t = t * (RET_DK ** -0.5)
        if rope:
            t = t * rc_ref[...] + pltpu.roll(t, RET_DK // 2, 1) * rs_ref[...]
        if hh < RET_HEADS:
            rq_ref[0, :, hh * RET_DK:(hh + 1) * RET_DK] = t.astype(BF16)
        else:
            g = hh - RET_HEADS
            rk_ref[0, :, g * RET_DK:(g + 1) * RET_DK] = t.astype(BF16)
    rv_ref[0] = _dot(h, w_ref[:, OFF_RV:OFF_GB]).astype(BF16)


def _const_spec(shape):
    return pl.BlockSpec(shape, lambda b, i: (0,) * len(shape), pipeline_mode=pl.Buffered(1))


def _in_call(x, mod3, ctx_mod, wts, rope_tabs, tm):
    B, L, _ = x.shape
    rope = rope_tabs is not None
    emit_cache = not rope
    mod_map = (lambda b, i: (MOD_ROWS - 1, 0, 0)) if ctx_mod else (lambda b, i: (b, 0, 0))
    in_specs = [pl.BlockSpec((1, tm, D_MODEL), lambda b, i: (b, i, 0)),
                pl.BlockSpec((1, 3, D_MODEL), mod_map),
                _const_spec((1, D_MODEL)),
                _const_spec((D_MODEL, LAT_COLS)),
                _const_spec((D_MODEL, WIDE_COLS)),
                _const_spec((1, Q_LORA)),
                _const_spec((Q_LORA, MLA_HEADS * QK_PAD)),
                _const_spec((1, QK_PAD)),
                _const_spec((1, KV_LORA)),
                _const_spec((KV_LORA, MLA_HEADS * MLA_NOPE)),
                _const_spec((MLA_W, KV_LORA)),
                _const_spec((1, LANE)),
                _const_spec((1, LANE))]
    args = [x, mod3, wts["norm_w"], wts["w_lat"], wts["w_wide"], wts["q_norm_w"], wts["w_uq"], wts["qk_q_w"],
            wts["kv_norm_w"], wts["w_uk"], wts["w_uvt"], wts["qk_k_n"], wts["qk_k_r"]]
    if rope:
        in_specs += [pl.BlockSpec((tm, LANE), lambda b, i: (i, 0))] * 5
        args += list(rope_tabs)
    tok = lambda w: pl.BlockSpec((1, tm, w), lambda b, i: (b, i, 0))
    out_specs = [pl.BlockSpec((1, MLA_HEADS, tm, QK_PAD), lambda b, i: (b, 0, i, 0)),
                 pl.BlockSpec((1, MLA_HEADS, tm, QK_PAD), lambda b, i: (b, 0, i, 0)),
                 pl.BlockSpec((1, MLA_HEADS, V_AUG, tm), lambda b, i: (b, 0, 0, i)),
                 tok(MLA_W), tok(RET_QK), tok(RET_QK), tok(RET_W), tok(RET_W)]
    out_shape = [jax.ShapeDtypeStruct((B, MLA_HEADS, L, QK_PAD), BF16),
                 jax.ShapeDtypeStruct((B, MLA_HEADS, L, QK_PAD), BF16),
                 jax.ShapeDtypeStruct((B, MLA_HEADS, V_AUG, L), BF16),
                 jax.ShapeDtypeStruct((B, L, MLA_W), BF16),
                 jax.ShapeDtypeStruct((B, L, RET_QK), BF16),
                 jax.ShapeDtypeStruct((B, L, RET_QK), BF16),
                 jax.ShapeDtypeStruct((B, L, RET_W), BF16),
                 jax.ShapeDtypeStruct((B, L, RET_W), BF16)]
    if emit_cache:
        out_specs += [tok(KV_LORA), tok(ROPE_DIM)]
        out_shape += [jax.ShapeDtypeStruct((B, L, KV_LORA), F32),
                      jax.ShapeDtypeStruct((B, L, ROPE_DIM), F32)]
    return pl.pallas_call(
        functools.partial(_in_kernel, rope=rope, emit_cache=emit_cache),
        grid=(B, L // tm),
        in_specs=in_specs, out_specs=out_specs, out_shape=out_shape,
        compiler_params=pltpu.CompilerParams(dimension_semantics=("parallel", "parallel"),
                                             vmem_limit_bytes=VMEM_LIMIT),
        name="in_proj_rope" if rope else "in_proj",
    )(*args)


def _ctx_kernel(ckv_ref, kr_ref, wuk_ref, wuvt_ref, kkn_ref, kkr_ref, k_ref, vt_ref):
    _keys_values(ckv_ref[0], kr_ref[0], wuk_ref, wuvt_ref, kkn_ref[...], kkr_ref[...], None,
                 k_ref, vt_ref)


def _ctx_call(ckv, kr_pad, wts):
    B, P, _ = ckv.shape
    cs = lambda shape: pl.BlockSpec(shape, lambda b: (0,) * len(shape))
    return pl.pallas_call(
        _ctx_kernel,
        grid=(B,),
        in_specs=[pl.BlockSpec((1, P, KV_LORA), lambda b: (b, 0, 0)),
                  pl.BlockSpec((1, P, LANE), lambda b: (b, 0, 0)),
                  cs((KV_LORA, MLA_HEADS * MLA_NOPE)), cs((MLA_W, KV_LORA)),
                  cs((1, LANE)), cs((1, LANE))],
        out_specs=[pl.BlockSpec((1, MLA_HEADS, P, QK_PAD), lambda b: (b, 0, 0, 0)),
                   pl.BlockSpec((1, MLA_HEADS, V_AUG, P), lambda b: (b, 0, 0, 0))],
        out_shape=[jax.ShapeDtypeStruct((B, MLA_HEADS, P, QK_PAD), BF16),
                   jax.ShapeDtypeStruct((B, MLA_HEADS, V_AUG, P), BF16)],
        compiler_params=pltpu.CompilerParams(dimension_semantics=("parallel",),
                                             vmem_limit_bytes=VMEM_LIMIT),
        name="ctx_kv",
    )(ckv, kr_pad, wts["w_uk"], wts["w_uvt"], wts["qk_k_n"], wts["qk_k_r"])


def _attn_kernel(*refs, heads, **kw):
    def first_scores(hh):
        if hh < heads:
            _attn_head(hh, *refs, phase="first", **kw)

    first_scores(0)
    sums = [_attn_head(hh, *refs, phase="fast", hook=functools.partial(first_scores, hh + 1),
                       **kw) for hh in range(heads)]
    worst = functools.reduce(jnp.maximum, sums)

    @pl.when(jnp.logical_not(jnp.max(worst) < OVERFLOW_GUARD))
    def _():
        for hh in range(heads):
            _attn_head(hh, *refs, phase="robust", **kw)


def _attn_head(hh, *refs, tk, n_chunks, has_ctx, phase, hook=None):
    if has_ctx:
        q_ref, k_ref, vt_ref, kc_ref, vtc_ref, ga_ref, o_ref, s_scr = refs
    else:
        q_ref, k_ref, vt_ref, ga_ref, o_ref, s_scr = refs
    q = q_ref[0, hh]
    tq = q.shape[0]
    cols = slice(hh * MLA_V, (hh + 1) * MLA_V)

    def scores(kc):
        return lax.dot_general(kc, q, NT_DIMS, preferred_element_type=F32)

    def chunk(j):
        s0 = pl.multiple_of(j * tk, tk)
        return k_ref[0, hh, pl.ds(s0, tk), :], vt_ref[0, hh, :, pl.ds(s0, tk)]

    def finish(acc):
        out = (acc[:MLA_V] * (1.0 / acc[MLA_V:MLA_V + 1])).T
        o_ref[0, :, cols] = (out * ga_ref[0, :, cols].astype(F32)).astype(BF16)

    if phase == "first":
        s_scr[hh, 0] = scores(k_ref[0, hh, 0:tk, :])
        return None

    if phase == "robust":
        def step(kc, vtc, carry):
            m, acc = carry
            st = scores(kc)
            m_new = jnp.maximum(m, jnp.max(st, axis=0, keepdims=True))
            p = jnp.exp2(st - m_new).astype(BF16)
            return m_new, jnp.exp2(m - m_new) * acc + _dot(vtc, p)

        carry = (jnp.full((1, tq), -1e30, F32), jnp.zeros((V_AUG, tq), F32))
        carry = lax.fori_loop(0, n_chunks, lambda j, c: step(*chunk(j), c), carry)
        if has_ctx:
            carry = step(kc_ref[0, hh], vtc_ref[0, hh], carry)
        finish(carry[1])
        return None

    m_fix = jnp.max(s_scr[hh, 0], axis=0, keepdims=True)

    def weights(st):
        return jnp.exp2(st - m_fix).astype(BF16)

    acc = jnp.zeros((V_AUG, tq), F32)
    for j in range(n_chunks):
        if j + 1 < n_chunks:
            s_scr[hh, (j + 1) % 2] = scores(k_ref[0, hh, (j + 1) * tk:(j + 2) * tk, :])
        elif has_ctx:
            st_ctx = scores(kc_ref[0, hh])
        if j + 1 == n_chunks and not has_ctx:
            hook()
        acc = acc + _dot(vt_ref[0, hh, :, j * tk:(j + 1) * tk], weights(s_scr[hh, j % 2]))
    if has_ctx:
        hook()
        acc = acc + _dot(vtc_ref[0, hh], weights(st_ctx))
    finish(acc)
    return acc[MLA_V:MLA_V + 1]


def _attn_call(q, k, vt, ga, ctx, tq, tk, heads):
    B, H, L, _ = q.shape
    has_ctx = ctx is not None
    in_specs = [pl.BlockSpec((1, heads, tq, QK_PAD), lambda b, h, i: (b, h, i, 0)),
                pl.BlockSpec((1, heads, L, QK_PAD), lambda b, h, i: (b, h, 0, 0)),
                pl.BlockSpec((1, heads, V_AUG, L), lambda b, h, i: (b, h, 0, 0))]
    args = [q, k, vt]
    if has_ctx:
        kc, vtc = ctx
        P = kc.shape[2]
        in_specs += [pl.BlockSpec((1, heads, P, QK_PAD), lambda b, h, i: (b, h, 0, 0)),
                     pl.BlockSpec((1, heads, V_AUG, P), lambda b, h, i: (b, h, 0, 0))]
        args += [kc, vtc]
    in_specs.append(pl.BlockSpec((1, tq, heads * MLA_V), lambda b, h, i: (b, i, h)))
    args.append(ga)
    return pl.pallas_call(
        functools.partial(_attn_kernel, heads=heads, tk=tk, n_chunks=L // tk, has_ctx=has_ctx),
        grid=(B, H // heads, L // tq),
        in_specs=in_specs,
        out_specs=pl.BlockSpec((1, tq, heads * MLA_V), lambda b, h, i: (b, i, h)),
        out_shape=jax.ShapeDtypeStruct((B, L, MLA_W), BF16),
        scratch_shapes=[pltpu.VMEM((heads, 2, tk, tq), F32)],
        compiler_params=pltpu.CompilerParams(
            dimension_semantics=("parallel", "parallel", "parallel"),
            vmem_limit_bytes=VMEM_LIMIT),
        name="attn_ctx" if has_ctx else "attn",
    )(*args)


def _ret_kernel(*refs, heads, **kw):
    for hh in range(heads):
        _ret_head(hh, *refs, heads=heads, **kw)


def _ret_head(hh, *refs, heads, n_chunks, has_state, unroll):
    it = iter(refs)
    lgf_ref, lgb_ref = next(it), next(it)
    q_ref, k_ref, v_ref, gb_ref, gnw_ref = next(it), next(it), next(it), next(it), next(it)
    if has_state:
        s0f_ref, s0b_ref = next(it), next(it)
    out_ref = next(it)
    if not has_state:
        sf_ref, sb_ref = next(it), next(it)
    kv_scr, st_scr, o_scr, at_scr = next(it), next(it), next(it), next(it)

    C = RET_CHUNK
    kcols = slice(hh * RET_DK, (hh + 1) * RET_DK)
    vcols = slice(hh * RET_DV, (hh + 1) * RET_DV)
    hd = pl.program_id(1) * heads + hh
    lgf = lgf_ref[hd]
    lgb = lgb_ref[hd]
    ri = lax.broadcasted_iota(jnp.int32, (C, C), 0).astype(F32)
    ci = lax.broadcasted_iota(jnp.int32, (C, C), 1).astype(F32)
    diff = ri - ci
    low = diff >= 0
    mask = (jnp.where(low, jnp.exp(jnp.where(low, diff, 0.0) * lgf), 0.0)
            + jnp.where(low, 0.0, jnp.exp(jnp.where(low, 0.0, -diff) * lgb)))
    qd_f = jnp.exp((ri + 1.0) * lgf)
    kd_f = jnp.exp((C - 1.0 - ri) * lgf)
    qd_b = jnp.exp((C - ri) * lgb)
    kd_b = jnp.exp(ri * lgb)
    cd_f = jnp.exp(jnp.full((RET_DK, RET_DV), C, F32) * lgf)
    cd_b = jnp.exp(jnp.full((RET_DK, RET_DV), C, F32) * lgb)
    gnw = gnw_ref[hh]

    def rows(n):
        return pl.ds(n * C if isinstance(n, int) else pl.multiple_of(n * C, C), C)

    def loop(lo, hi, body, carry):
        if hi - lo <= 2:
            for i in range(lo, hi):
                carry = body(i, carry)
            return carry
        return lax.fori_loop(lo, hi, body, carry)

    trips = n_chunks // unroll

    def decayed_keys(i):
        for u in range(unroll):
            n = i * unroll + u
            kc = k_ref[0, rows(n), kcols].astype(F32)
            a = jnp.concatenate([kc * kd_f, kc * kd_b], axis=1)
            at_scr[hh, n] = a.T.astype(BF16)

    def increments(i):
        for u in range(unroll):
            n = i * unroll + u
            kv_scr[hh, n] = _dot(at_scr[hh, n], v_ref[0, rows(n), vcols])

    def summarise(i, _):
        increments(i - 1)
        decayed_keys(i)
        return 0

    decayed_keys(0)
    loop(1, trips, summarise, 0)
    increments(trips - 1)

    if has_state:
        init = (s0f_ref[0, hh], s0b_ref[0, hh])
    else:
        init = (jnp.zeros((RET_DK, RET_DV), F32),) * 2

    def scan(i, carry):
        sf, sb = carry
        nb = n_chunks - 1 - i
        st_scr[hh, i, :RET_DK, :] = sf.astype(BF16)
        st_scr[hh, nb, RET_DK:, :] = sb.astype(BF16)
        return (sf * cd_f + kv_scr[hh, i, :RET_DK, :], sb * cd_b + kv_scr[hh, nb, RET_DK:, :])

    sf, sb = loop(0, n_chunks, scan, init)
    if not has_state:
        sf_ref[0, hh] = sf
        sb_ref[0, hh] = sb

    def products(i):
        ns = [i * unroll + u for u in range(unroll)]
        sls = [rows(n) for n in ns]
        qs = [q_ref[0, sl, kcols] for sl in sls]
        ss = [lax.dot_general(q, k_ref[0, sl, kcols], NT_DIMS, preferred_element_type=F32)
              for q, sl in zip(qs, sls)]
        for n, sl, q, s in zip(ns, sls, qs, ss):
            qf = q.astype(F32)
            qq = jnp.concatenate([qf * qd_f, qf * qd_b], axis=1).astype(BF16)
            o_scr[hh, sl, :] = (_dot((s * mask).astype(BF16), v_ref[0, sl, vcols])
                                + _dot(qq, st_scr[hh, n]))

    def normalise(i):
        for u in range(unroll):
            sl = rows(i * unroll + u)
            y = _rms(o_scr[hh, sl, :], gnw, RET_DV) * gb_ref[0, sl, vcols].astype(F32)
            out_ref[0, sl, vcols] = y.astype(BF16)

    def both(i, _):
        normalise(i - 1)
        products(i)
        return 0

    products(0)
    loop(1, trips, both, 0)
    normalise(trips - 1)


def _ret_call(lgf, lgb, rq, rk, rv, gb, gnw, states, heads):
    B, L, _ = rq.shape
    n_chunks = L // RET_CHUNK
    has_state = states is not None
    smem = pl.BlockSpec(memory_space=pltpu.SMEM)
    st_spec = pl.BlockSpec((1, heads, RET_DK, RET_DV), lambda b, h: (b, h, 0, 0))
    in_specs = [smem, smem,
                pl.BlockSpec((1, L, heads * RET_DK), lambda b, h: (b, 0, h)),
                pl.BlockSpec((1, L, heads * RET_DK), lambda b, h: (b, 0, h)),
                pl.BlockSpec((1, L, heads * RET_DV), lambda b, h: (b, 0, h)),
                pl.BlockSpec((1, L, heads * RET_DV), lambda b, h: (b, 0, h)),
                pl.BlockSpec((heads, 1, RET_DV), lambda b, h: (h, 0, 0))]
    args = [lgf, lgb, rq, rk, rv, gb, gnw]
    out_specs = [pl.BlockSpec((1, L, heads * RET_DV), lambda b, h: (b, 0, h))]
    out_shape = [jax.ShapeDtypeStruct((B, L, RET_W), BF16)]
    if has_state:
        in_specs += [st_spec, st_spec]
        args += list(states)
    else:
        out_specs += [st_spec, st_spec]
        out_shape += [jax.ShapeDtypeStruct((B, RET_HEADS, RET_DK, RET_DV), F32)] * 2
    return pl.pallas_call(
        functools.partial(_ret_kernel, heads=heads, n_chunks=n_chunks, has_state=has_state,
                          unroll=math.gcd(n_chunks, 8)),
        grid=(B, RET_HEADS // heads),
        in_specs=in_specs, out_specs=out_specs, out_shape=out_shape,
        scratch_shapes=[pltpu.VMEM((heads, n_chunks, 2 * RET_DK, RET_DV), F32),
                        pltpu.VMEM((heads, n_chunks, 2 * RET_DK, RET_DV), BF16),
                        pltpu.VMEM((heads, L, RET_DV), F32),
                        pltpu.VMEM((heads, n_chunks, 2 * RET_DK, RET_CHUNK), BF16)],
        compiler_params=pltpu.CompilerParams(dimension_semantics=("parallel", "parallel"),
                                             vmem_limit_bytes=VMEM_LIMIT),
        name="ret_state" if has_state else "ret",
    )(*args)


def _out_kernel(ma_ref, mb_ref, w_ref, x_ref, mod_ref, o_ref):
    acc = _dot(ma_ref[0], w_ref[:MLA_W, :]) + _dot(mb_ref[0], w_ref[MLA_W:, :])
    o_ref[0] = x_ref[0] + mod_ref[0, 2:3, :] * acc


def _out_call(mix_a, mix_b, w_out, x, mod3, ctx_mod, tm):
    B, L, _ = x.shape
    mod_map = (lambda b, i: (MOD_ROWS - 1, 0, 0)) if ctx_mod else (lambda b, i: (b, 0, 0))
    return pl.pallas_call(
        _out_kernel,
        grid=(B, L // tm),
        in_specs=[pl.BlockSpec((1, tm, MLA_W), lambda b, i: (b, i, 0)),
                  pl.BlockSpec((1, tm, RET_W), lambda b, i: (b, i, 0)),
                  _const_spec((D_MODEL, D_MODEL)),
                  pl.BlockSpec((1, tm, D_MODEL), lambda b, i: (b, i, 0)),
                  pl.BlockSpec((1, 3, D_MODEL), mod_map)],
        out_specs=pl.BlockSpec((1, tm, D_MODEL), lambda b, i: (b, i, 0)),
        out_shape=jax.ShapeDtypeStruct((B, L, D_MODEL), F32),
        compiler_params=pltpu.CompilerParams(dimension_semantics=("parallel", "parallel"),
                                             vmem_limit_bytes=VMEM_LIMIT),
        name="out_proj",
    )(mix_a, mix_b, w_out, x, mod3)


def _rope_angles(pos, dim):
    half = dim // 2
    freqs = ROPE_BASE ** (-jnp.arange(half, dtype=F32) / half)
    return pos.astype(F32)[:, None] * freqs[None, :]


def _rope_tables(L):
    rows = L // GRID_W
    ar = _rope_angles(jnp.arange(rows), ROPE_DIM // 2)
    ac = _rope_angles(jnp.arange(GRID_W), ROPE_DIM // 2)
    cr, sr, cc, sc = jnp.cos(ar), jnp.sin(ar), jnp.cos(ac), jnp.sin(ac)
    zr, zc = jnp.zeros_like(sr), jnp.zeros_like(sc)

    def grid_table(per_row, per_col):
        w = per_row.shape[-1]
        pad = jnp.zeros((rows, GRID_W, LANE - 2 * w), F32)
        t = jnp.concatenate([jnp.broadcast_to(per_row[:, None, :], (rows, GRID_W, w)),
                             jnp.broadcast_to(per_col[None, :, :], (rows, GRID_W, w)), pad],
                            axis=-1)
        return t.reshape(L, LANE)

    cat = lambda a, b: jnp.concatenate([a, b], axis=-1)
    mc = grid_table(cat(cr, cr), cat(cc, cc))
    msa = grid_table(cat(-sr, zr), cat(-sc, zc))
    msb = grid_table(cat(zr, sr), cat(zc, sc))

    hi = _rope_angles(jnp.arange(rows) * GRID_W, RET_DK)[:, None, :]
    lo = _rope_angles(jnp.arange(GRID_W), RET_DK)[None, :, :]
    cos_t = jnp.cos(hi) * jnp.cos(lo) - jnp.sin(hi) * jnp.sin(lo)
    sin_t = jnp.sin(hi) * jnp.cos(lo) + jnp.cos(hi) * jnp.sin(lo)
    rc = cat(cos_t, cos_t).reshape(L, LANE)
    rs = cat(-sin_t, sin_t).reshape(L, LANE)
    return mc, msa, msb, rc, rs


def _prep_weights(l, norm_w, w_in, q_norm_w, w_uq, kv_norm_w, w_uk, w_uv, qk_q_w, qk_k_w):
    wi = w_in[l]
    w_lat = jnp.pad(wi[:, :LAT_SRC_COLS], ((0, 0), (0, LAT_COLS - LAT_SRC_COLS))).astype(BF16)
    w_wide = wi[:, LAT_SRC_COLS:].astype(BF16)
    wq = w_uq[l].reshape(Q_LORA, MLA_HEADS, QK_HEAD)
    wq = jnp.pad(wq, ((0, 0), (0, 0), (0, QK_PAD - QK_HEAD))).reshape(Q_LORA, MLA_HEADS * QK_PAD)
    qscale = (QK_HEAD ** -0.5) * math.log2(math.e)
    qkq = jnp.pad(qk_q_w[l] * qscale, (0, QK_PAD - QK_HEAD))[None, :]
    kk = qk_k_w[l]
    return {
        "norm_w": norm_w[l][None, :],
        "w_lat": w_lat,
        "w_wide": w_wide,
        "q_norm_w": q_norm_w[l][None, :],
        "w_uq": wq.astype(BF16),
        "qk_q_w": qkq,
        "kv_norm_w": kv_norm_w[l][None, :],
        "w_uk": w_uk[l].astype(BF16),
        "w_uvt": w_uv[l].T.astype(BF16),
        "qk_k_n": kk[None, :MLA_NOPE],
        "qk_k_r": jnp.pad(kk[MLA_NOPE:], (0, LANE - ROPE_DIM))[None, :],
    }


def _tiles(L):
    tm = min(L, 256)
    tm_out = min(L, 512)
    tq = min(L, 4096)
    tk = min(L, 512)
    heads = MLA_HEADS if L <= 512 else 1
    ret_heads = RET_HEADS if L <= 512 else 1
    return tm, tm_out, tq, tk, heads, ret_heads


def _layer(x, mod3, ctx_mod, wts, w_out, lgf, lgb, gnw, rope_tabs, ctx, states):
    tm, tm_out, tq, tk, heads, ret_heads = _tiles(x.shape[1])
    outs = _in_call(x, mod3, ctx_mod, wts, rope_tabs, tm)
    q, k, vt, ga, rq, rk, rv, gb = outs[:8]
    mix_a = _attn_call(q, k, vt, ga, ctx, tq, tk, heads)
    ret = _ret_call(lgf, lgb, rq, rk, rv, gb, gnw, states, ret_heads)
    y = _out_call(mix_a, ret[0], w_out, x, mod3, ctx_mod, tm_out)
    return y, outs[8:], ret[1:]


def kernel(x_prompt, x_sample, c, cache_mla_ckv, cache_mla_krope, state_ret_fwd, state_ret_bwd,
           c_ctx, norm_w, w_mod, b_mod, w_in, mla_q_norm_w, mla_w_uq, mla_kv_norm_w, mla_w_uk,
           mla_w_uv, mla_qk_q_w, mla_qk_k_w, ret_log_decay_fwd, ret_log_decay_bwd, ret_gn_w, w_out):
    depth = w_in.shape[0]
    dec_b = x_sample.shape[0]
    assert dec_b < MOD_ROWS
    cvec = jnp.zeros((MOD_ROWS, D_MODEL), F32).at[:dec_b].set(c).at[MOD_ROWS - 1].set(c_ctx)
    rope_tabs = _rope_tables(x_sample.shape[1])

    xp, xs = x_prompt, x_sample
    ckv_l, kr_l, sf_l, sb_l = [], [], [], []
    for l in range(depth):
        wts = _prep_weights(l, norm_w, w_in, mla_q_norm_w, mla_w_uq, mla_kv_norm_w, mla_w_uk,
                            mla_w_uv, mla_qk_q_w, mla_qk_k_w)
        w_out_b = w_out[l].astype(BF16)
        lgf = -jnp.exp(ret_log_decay_fwd[l].astype(F32))
        lgb = -jnp.exp(ret_log_decay_bwd[l].astype(F32))
        gnw = ret_gn_w[l][:, None, :]
        mod3 = _mod_call(cvec, w_mod[l], b_mod[l][None, :]).reshape(MOD_ROWS, 3, D_MODEL)

        xp, (ckv, kr), (sf, sb) = _layer(xp, mod3, True, wts, w_out_b, lgf, lgb, gnw,
                                         None, None, None)
        ckv_l.append(ckv)
        kr_l.append(kr)
        sf_l.append(sf)
        sb_l.append(sb)

        kr_pad = jnp.pad(cache_mla_krope[:, l], ((0, 0), (0, 0), (0, LANE - ROPE_DIM)))
        ctx = _ctx_call(cache_mla_ckv[:, l], kr_pad, wts)
        xs, _, _ = _layer(xs, mod3, False, wts, w_out_b, lgf, lgb, gnw, rope_tabs, ctx,
                          (state_ret_fwd[:, l], state_ret_bwd[:, l]))

    return (xp, xs, jnp.stack(ckv_l, axis=1), jnp.stack(kr_l, axis=1),
            jnp.stack(sf_l, axis=1), jnp.stack(sb_l, axis=1))
```

```python
import functools
import math

import jax
import jax.numpy as jnp
from jax import lax
from jax.experimental import pallas as pl
from jax.experimental.pallas import tpu as pltpu

F32 = jnp.float32
BF16 = jnp.bfloat16

D_MODEL = 2048
GRID_W = 64
MLA_W = 1024
RET_W = 1024
MLA_NOPE = 128
ROPE_DIM = 64
QK_HEAD = MLA_NOPE + ROPE_DIM
MLA_V = 128
MLA_HEADS = 8
Q_LORA = 384
KV_LORA = 256
RET_DV = 256
RET_DK = 128
RET_HEADS = 4
RET_QK = RET_HEADS * RET_DK
RET_CHUNK = 128
ROPE_BASE = 10000.0
EPS = 1e-6

LANE = 128
QK_PAD = 2 * LANE
V_AUG = MLA_V + 16
OVERFLOW_GUARD = 2.0 ** 100
MOD_ROWS = 16
VMEM_LIMIT = 60 * 1024 * 1024

OFF_QLAT = 0
OFF_CKV = OFF_QLAT + Q_LORA
OFF_KROPE = OFF_CKV + KV_LORA
LAT_COLS = OFF_KROPE + LANE
LAT_SRC_COLS = OFF_KROPE + ROPE_DIM
OFF_GA = 0
OFF_RQ = OFF_GA + MLA_W
OFF_RK = OFF_RQ + RET_QK
OFF_RV = OFF_RK + RET_QK
OFF_GB = OFF_RV + RET_W
WIDE_COLS = OFF_GB + RET_W

NT_DIMS = (((1,), (1,)), ((), ()))


def _silu(x):
    return x * (1.0 / (1.0 + jnp.exp(-x)))


def _rms(x, w, n):
    ms = jnp.sum(x * x, axis=-1, keepdims=True) * (1.0 / n)
    return x * lax.rsqrt(ms + EPS) * w


def _dot(a, b):
    return jnp.dot(a, b, preferred_element_type=F32)


def _mod_kernel(c_ref, w_ref, b_ref, o_ref):
    s = _silu(c_ref[...]).astype(BF16)
    o_ref[...] = _dot(s, w_ref[...].astype(BF16)) + b_ref[...]


def _mod_call(cvec, w_mod, b_mod):
    tn = 512
    n = w_mod.shape[1]
    return pl.pallas_call(
        _mod_kernel,
        grid=(n // tn,),
        in_specs=[pl.BlockSpec((MOD_ROWS, D_MODEL), lambda j: (0, 0)),
                  pl.BlockSpec((D_MODEL, tn), lambda j: (0, j)),
                  pl.BlockSpec((1, tn), lambda j: (0, j))],
        out_specs=pl.BlockSpec((MOD_ROWS, tn), lambda j: (0, j)),
        out_shape=jax.ShapeDtypeStruct((MOD_ROWS, n), F32),
        compiler_params=pltpu.CompilerParams(dimension_semantics=("arbitrary",),
                                             vmem_limit_bytes=VMEM_LIMIT),
        name="mod",
    )(cvec, w_mod, b_mod)


def _mla_rope(r, c, sa, sb):
    return r * c + pltpu.roll(r, LANE - 16, 1) * sa + pltpu.roll(r, 16, 1) * sb


def _keys_values(ckv, kr, w_uk_ref, w_uvt_ref, kkn, kkr, rope, k_ref, vt_ref):
    ckv_b = ckv.astype(BF16)
    kn_all = _dot(ckv_b, w_uk_ref[...])
    krw = kr * kkr
    if rope is not None:
        krw = _mla_rope(krw, *rope)
    kr_ss = jnp.sum(kr * kr, axis=-1, keepdims=True)
    for h in range(MLA_HEADS):
        kn = kn_all[:, h * MLA_NOPE:(h + 1) * MLA_NOPE]
        ss = jnp.sum(kn * kn, axis=-1, keepdims=True) + kr_ss
        inv = lax.rsqrt(ss * (1.0 / QK_HEAD) + EPS)
        k_ref[0, h, :, :LANE] = (kn * inv * kkn).astype(BF16)
        k_ref[0, h, :, LANE:] = (krw * inv).astype(BF16)
    vt = lax.dot_general(w_uvt_ref[...], ckv_b, NT_DIMS, preferred_element_type=F32)
    ones_row = jnp.where(lax.broadcasted_iota(jnp.int32, (V_AUG - MLA_V, vt.shape[1]), 0) == 0,
                         1.0, 0.0).astype(BF16)
    for h in range(MLA_HEADS):
        vt_ref[0, h, :MLA_V, :] = vt[h * MLA_V:(h + 1) * MLA_V].astype(BF16)
        vt_ref[0, h, MLA_V:, :] = ones_row


def _in_kernel(*refs, rope, emit_cache):
    it = iter(refs)
    x_ref, mod_ref, nw_ref, wl_ref, w_ref = next(it), next(it), next(it), next(it), next(it)
    qnw_ref, wuq_ref, qkq_ref = next(it), next(it), next(it)
    kvw_ref, wuk_ref, wuvt_ref, kkn_ref, kkr_ref = next(it), next(it), next(it), next(it), next(it)
    if rope:
        mc_ref, msa_ref, msb_ref, rc_ref, rs_ref = next(it), next(it), next(it), next(it), next(it)
    q_ref, k_ref, vt_ref, ga_ref = next(it), next(it), next(it), next(it)
    rq_ref, rk_ref, rv_ref, gb_ref = next(it), next(it), next(it), next(it)
    if emit_cache:
        ckv_ref, kr_ref = next(it), next(it)

    x = x_ref[0]
    shift = mod_ref[0, 0:1, :]
    scale = mod_ref[0, 1:2, :]
    h = (_rms(x, nw_ref[...], D_MODEL) * (1.0 + scale) + shift).astype(BF16)

    mla_rope = (mc_ref[...], msa_ref[...], msb_ref[...]) if rope else None

    a = _dot(h, wl_ref[...])
    qn = _rms(a[:, OFF_QLAT:OFF_CKV], qnw_ref[...], Q_LORA).astype(BF16)
    ckv = _rms(a[:, OFF_CKV:OFF_KROPE], kvw_ref[...], KV_LORA)
    kr = a[:, OFF_KROPE:LAT_COLS]
    if emit_cache:
        ckv_ref[0] = ckv
        kr_ref[0] = kr[:, :ROPE_DIM]

    q_all = _dot(qn, wuq_ref[...])
    qkq = qkq_ref[...]
    for hh in range(MLA_HEADS):
        qh = q_all[:, hh * QK_PAD:(hh + 1) * QK_PAD]
        inv = lax.rsqrt(jnp.sum(qh * qh, axis=-1, keepdims=True) * (1.0 / QK_HEAD) + EPS)
        qh = qh * inv * qkq
        r = qh[:, LANE:]
        if rope:
            r = _mla_rope(r, *mla_rope)
        q_ref[0, hh, :, :LANE] = qh[:, :LANE].astype(BF16)
        q_ref[0, hh, :, LANE:] = r.astype(BF16)

    _keys_values(ckv, kr, wuk_ref, wuvt_ref, kkn_ref[...], kkr_ref[...], mla_rope, k_ref, vt_ref)

    ga_ref[0] = _silu(_dot(h, w_ref[:, OFF_GA:OFF_RQ])).astype(BF16)
    gb_ref[0] = _silu(_dot(h, w_ref[:, OFF_GB:WIDE_COLS])).astype(BF16)

    rqk = _dot(h, w_ref[:, OFF_RQ:OFF_RV])
    for hh in range(2 * RET_HEADS):
        t = rqk[:, hh * RET_DK:(hh + 1) * RET_DK]
        if hh >= RET_HEADS:
            t = t * (RET_DK ** -0.5)
        if rope:
            t = t * rc_ref[...] + pltpu.roll(t, RET_DK // 2, 1) * rs_ref[...]
        if hh < RET_HEADS:
            rq_ref[0, :, hh * RET_DK:(hh + 1) * RET_DK] = t.astype(BF16)
        else:
            g = hh - RET_HEADS
            rk_ref[0, :, g * RET_DK:(g + 1) * RET_DK] = t.astype(BF16)
    rv_ref[0] = _dot(h, w_ref[:, OFF_RV:OFF_GB]).astype(BF16)


def _const_spec(shape):
    return pl.BlockSpec(shape, lambda b, i: (0,) * len(shape), pipeline_mode=pl.Buffered(1))


def _in_call(x, mod3, ctx_mod, wts, rope_tabs, tm):
    B, L, _ = x.shape
    rope = rope_tabs is not None
    emit_cache = not rope
    mod_map = (lambda b, i: (MOD_ROWS - 1, 0, 0)) if ctx_mod else (lambda b, i: (b, 0, 0))
    in_specs = [pl.BlockSpec((1, tm, D_MODEL), lambda b, i: (b, i, 0)),
                pl.BlockSpec((1, 3, D_MODEL), mod_map),
                _const_spec((1, D_MODEL)),
                _const_spec((D_MODEL, LAT_COLS)),
                _const_spec((D_MODEL, WIDE_COLS)),
                _const_spec((1, Q_LORA)),
                _const_spec((Q_LORA, MLA_HEADS * QK_PAD)),
                _const_spec((1, QK_PAD)),
                _const_spec((1, KV_LORA)),
                _const_spec((KV_LORA, MLA_HEADS * MLA_NOPE)),
                _const_spec((MLA_W, KV_LORA)),
                _const_spec((1, LANE)),
                _const_spec((1, LANE))]
    args = [x, mod3, wts["norm_w"], wts["w_lat"], wts["w_wide"], wts["q_norm_w"], wts["w_uq"], wts["qk_q_w"],
            wts["kv_norm_w"], wts["w_uk"], wts["w_uvt"], wts["qk_k_n"], wts["qk_k_r"]]
    if rope:
        in_specs += [pl.BlockSpec((tm, LANE), lambda b, i: (i, 0))] * 5
        args += list(rope_tabs)
    tok = lambda w: pl.BlockSpec((1, tm, w), lambda b, i: (b, i, 0))
    out_specs = [pl.BlockSpec((1, MLA_HEADS, tm, QK_PAD), lambda b, i: (b, 0, i, 0)),
                 pl.BlockSpec((1, MLA_HEADS, tm, QK_PAD), lambda b, i: (b, 0, i, 0)),
                 pl.BlockSpec((1, MLA_HEADS, V_AUG, tm), lambda b, i: (b, 0, 0, i)),
                 tok(MLA_W), tok(RET_QK), tok(RET_QK), tok(RET_W), tok(RET_W)]
    out_shape = [jax.ShapeDtypeStruct((B, MLA_HEADS, L, QK_PAD), BF16),
                 jax.ShapeDtypeStruct((B, MLA_HEADS, L, QK_PAD), BF16),
                 jax.ShapeDtypeStruct((B, MLA_HEADS, V_AUG, L), BF16),
                 jax.ShapeDtypeStruct((B, L, MLA_W), BF16),
                 jax.ShapeDtypeStruct((B, L, RET_QK), BF16),
                 jax.ShapeDtypeStruct((B, L, RET_QK), BF16),
                 jax.ShapeDtypeStruct((B, L, RET_W), BF16),
                 jax.ShapeDtypeStruct((B, L, RET_W), BF16)]
    if emit_cache:
        out_specs += [tok(KV_LORA), tok(ROPE_DIM)]
        out_shape += [jax.ShapeDtypeStruct((B, L, KV_LORA), F32),
                      jax.ShapeDtypeStruct((B, L, ROPE_DIM), F32)]
    return pl.pallas_call(
        functools.partial(_in_kernel, rope=rope, emit_cache=emit_cache),
        grid=(B, L // tm),
        in_specs=in_specs, out_specs=out_specs, out_shape=out_shape,
        compiler_params=pltpu.CompilerParams(dimension_semantics=("parallel", "parallel"),
                                             vmem_limit_bytes=VMEM_LIMIT),
        name="in_proj_rope" if rope else "in_proj",
    )(*args)


def _ctx_kernel(ckv_ref, kr_ref, wuk_ref, wuvt_ref, kkn_ref, kkr_ref, k_ref, vt_ref):
    _keys_values(ckv_ref[0], kr_ref[0], wuk_ref, wuvt_ref, kkn_ref[...], kkr_ref[...], None,
                 k_ref, vt_ref)


def _ctx_call(ckv, kr_pad, wts):
    B, P, _ = ckv.shape
    cs = lambda shape: pl.BlockSpec(shape, lambda b: (0,) * len(shape))
    return pl.pallas_call(
        _ctx_kernel,
        grid=(B,),
        in_specs=[pl.BlockSpec((1, P, KV_LORA), lambda b: (b, 0, 0)),
                  pl.BlockSpec((1, P, LANE), lambda b: (b, 0, 0)),
                  cs((KV_LORA, MLA_HEADS * MLA_NOPE)), cs((MLA_W, KV_LORA)),
                  cs((1, LANE)), cs((1, LANE))],
        out_specs=[pl.BlockSpec((1, MLA_HEADS, P, QK_PAD), lambda b: (b, 0, 0, 0)),
                   pl.BlockSpec((1, MLA_HEADS, V_AUG, P), lambda b: (b, 0, 0, 0))],
        out_shape=[jax.ShapeDtypeStruct((B, MLA_HEADS, P, QK_PAD), BF16),
                   jax.ShapeDtypeStruct((B, MLA_HEADS, V_AUG, P), BF16)],
        compiler_params=pltpu.CompilerParams(dimension_semantics=("parallel",),
                                             vmem_limit_bytes=VMEM_LIMIT),
        name="ctx_kv",
    )(ckv, kr_pad, wts["w_uk"], wts["w_uvt"], wts["qk_k_n"], wts["qk_k_r"])


def _attn_kernel(*refs, heads, **kw):
    def first_scores(hh):
        if hh < heads:
            _attn_head(hh, *refs, phase="first", **kw)

    first_scores(0)
    sums = [_attn_head(hh, *refs, phase="fast", hook=functools.partial(first_scores, hh + 1),
                       **kw) for hh in range(heads)]
    worst = functools.reduce(jnp.maximum, sums)

    @pl.when(jnp.logical_not(jnp.max(worst) < OVERFLOW_GUARD))
    def _():
        for hh in range(heads):
            _attn_head(hh, *refs, phase="robust", **kw)


def _attn_head(hh, *refs, tk, n_chunks, has_ctx, phase, hook=None):
    if has_ctx:
        q_ref, k_ref, vt_ref, kc_ref, vtc_ref, ga_ref, o_ref, s_scr = refs
    else:
        q_ref, k_ref, vt_ref, ga_ref, o_ref, s_scr = refs
    q = q_ref[0, hh]
    tq = q.shape[0]
    cols = slice(hh * MLA_V, (hh + 1) * MLA_V)

    def scores(kc):
        return lax.dot_general(kc, q, NT_DIMS, preferred_element_type=F32)

    def chunk(j):
        s0 = pl.multiple_of(j * tk, tk)
        return k_ref[0, hh, pl.ds(s0, tk), :], vt_ref[0, hh, :, pl.ds(s0, tk)]

    def finish(acc):
        out = (acc[:MLA_V] * (1.0 / acc[MLA_V:MLA_V + 1])).T
        o_ref[0, :, cols] = (out * ga_ref[0, :, cols].astype(F32)).astype(BF16)

    if phase == "first":
        s_scr[hh, 0] = scores(k_ref[0, hh, 0:tk, :])
        return None

    if phase == "robust":
        def step(kc, vtc, carry):
            m, acc = carry
            st = scores(kc)
            m_new = jnp.maximum(m, jnp.max(st, axis=0, keepdims=True))
            p = jnp.exp2(st - m_new).astype(BF16)
            return m_new, jnp.exp2(m - m_new) * acc + _dot(vtc, p)

        carry = (jnp.full((1, tq), -1e30, F32), jnp.zeros((V_AUG, tq), F32))
        carry = lax.fori_loop(0, n_chunks, lambda j, c: step(*chunk(j), c), carry)
        if has_ctx:
            carry = step(kc_ref[0, hh], vtc_ref[0, hh], carry)
        finish(carry[1])
        return None

    m_fix = jnp.max(s_scr[hh, 0], axis=0, keepdims=True)

    def weights(st):
        return jnp.exp2(st - m_fix).astype(BF16)

    acc = jnp.zeros((V_AUG, tq), F32)
    for j in range(n_chunks):
        if j + 1 < n_chunks:
            s_scr[hh, (j + 1) % 2] = scores(k_ref[0, hh, (j + 1) * tk:(j + 2) * tk, :])
        elif has_ctx:
            st_ctx = scores(kc_ref[0, hh])
        if j + 1 == n_chunks and not has_ctx:
            hook()
        acc = acc + _dot(vt_ref[0, hh, :, j * tk:(j + 1) * tk], weights(s_scr[hh, j % 2]))
    if has_ctx:
        hook()
        acc = acc + _dot(vtc_ref[0, hh], weights(st_ctx))
    finish(acc)
    return acc[MLA_V:MLA_V + 1]


def _attn_call(q, k, vt, ga, ctx, tq, tk, heads):
    B, H, L, _ = q.shape
    has_ctx = ctx is not None
    in_specs = [pl.BlockSpec((1, heads, tq, QK_PAD), lambda b, h, i: (b, h, i, 0)),
                pl.BlockSpec((1, heads, L, QK_PAD), lambda b, h, i: (b, h, 0, 0)),
                pl.BlockSpec((1, heads, V_AUG, L), lambda b, h, i: (b, h, 0, 0))]
    args = [q, k, vt]
    if has_ctx:
        kc, vtc = ctx
        P = kc.shape[2]
        in_specs += [pl.BlockSpec((1, heads, P, QK_PAD), lambda b, h, i: (b, h, 0, 0)),
                     pl.BlockSpec((1, heads, V_AUG, P), lambda b, h, i: (b, h, 0, 0))]
        args += [kc, vtc]
    in_specs.append(pl.BlockSpec((1, tq, heads * MLA_V), lambda b, h, i: (b, i, h)))
    args.append(ga)
    return pl.pallas_call(
        functools.partial(_attn_kernel, heads=heads, tk=tk, n_chunks=L // tk, has_ctx=has_ctx),
        grid=(B, H // heads, L // tq),
        in_specs=in_specs,
        out_specs=pl.BlockSpec((1, tq, heads * MLA_V), lambda b, h, i: (b, i, h)),
        out_shape=jax.ShapeDtypeStruct((B, L, MLA_W), BF16),
        scratch_shapes=[pltpu.VMEM((heads, 2, tk, tq), F32)],
        compiler_params=pltpu.CompilerParams(
            dimension_semantics=("parallel", "parallel", "parallel"),
            vmem_limit_bytes=VMEM_LIMIT),
        name="attn_ctx" if has_ctx else "attn",
    )(*args)


def _ret_kernel(*refs, heads, **kw):
    for hh in range(heads):
        _ret_head(hh, *refs, heads=heads, **kw)


def _ret_head(hh, *refs, heads, n_chunks, has_state, unroll):
    it = iter(refs)
    lgf_ref, lgb_ref = next(it), next(it)
    q_ref, k_ref, v_ref, gb_ref, gnw_ref = next(it), next(it), next(it), next(it), next(it)
    if has_state:
        s0f_ref, s0b_ref = next(it), next(it)
    out_ref = next(it)
    if not has_state:
        sf_ref, sb_ref = next(it), next(it)
    kv_scr, st_scr, o_scr, at_scr = next(it), next(it), next(it), next(it)

    C = RET_CHUNK
    kcols = slice(hh * RET_DK, (hh + 1) * RET_DK)
    vcols = slice(hh * RET_DV, (hh + 1) * RET_DV)
    hd = pl.program_id(1) * heads + hh
    lgf = lgf_ref[hd]
    lgb = lgb_ref[hd]
    ri = lax.broadcasted_iota(jnp.int32, (C, C), 0).astype(F32)
    ci = lax.broadcasted_iota(jnp.int32, (C, C), 1).astype(F32)
    diff = ri - ci
    low = diff >= 0
    mask = (jnp.where(low, jnp.exp(jnp.where(low, diff, 0.0) * lgf), 0.0)
            + jnp.where(low, 0.0, jnp.exp(jnp.where(low, 0.0, -diff) * lgb)))
    qd_f = jnp.exp((ri + 1.0) * lgf)
    kd_f = jnp.exp((C - 1.0 - ri) * lgf)
    qd_b = jnp.exp((C - ri) * lgb)
    kd_b = jnp.exp(ri * lgb)
    cd_f = jnp.exp(jnp.full((RET_DK, RET_DV), C, F32) * lgf)
    cd_b = jnp.exp(jnp.full((RET_DK, RET_DV), C, F32) * lgb)
    gnw = gnw_ref[hh]

    def rows(n):
        return pl.ds(n * C if isinstance(n, int) else pl.multiple_of(n * C, C), C)

    def loop(lo, hi, body, carry):
        if hi - lo <= 2:
            for i in range(lo, hi):
                carry = body(i, carry)
            return carry
        return lax.fori_loop(lo, hi, body, carry)

    trips = n_chunks // unroll

    def decayed_keys(i):
        for u in range(unroll):
            n = i * unroll + u
            kc = k_ref[0, rows(n), kcols].astype(F32)
            a = jnp.concatenate([kc * kd_f, kc * kd_b], axis=1)
            at_scr[hh, n] = a.T.astype(BF16)

    def increments(i):
        for u in range(unroll):
            n = i * unroll + u
            kv_scr[hh, n] = _dot(at_scr[hh, n], v_ref[0, rows(n), vcols])

    def summarise(i, _):
        increments(i - 1)
        decayed_keys(i)
        return 0

    decayed_keys(0)
    loop(1, trips, summarise, 0)
    increments(trips - 1)

    if has_state:
        init = (s0f_ref[0, hh], s0b_ref[0, hh])
    else:
        init = (jnp.zeros((RET_DK, RET_DV), F32),) * 2

    def scan(i, carry):
        sf, sb = carry
        nb = n_chunks - 1 - i
        st_scr[hh, i, :RET_DK, :] = sf.astype(BF16)
        st_scr[hh, nb, RET_DK:, :] = sb.astype(BF16)
        return (sf * cd_f + kv_scr[hh, i, :RET_DK, :], sb * cd_b + kv_scr[hh, nb, RET_DK:, :])

    sf, sb = loop(0, n_chunks, scan, init)
    if not has_state:
        sf_ref[0, hh] = sf
        sb_ref[0, hh] = sb

    def products(i):
        ns = [i * unroll + u for u in range(unroll)]
        sls = [rows(n) for n in ns]
        qs = [q_ref[0, sl, kcols] for sl in sls]
        ss = [lax.dot_general(q, k_ref[0, sl, kcols], NT_DIMS, preferred_element_type=F32)
              for q, sl in zip(qs, sls)]
        for n, sl, q, s in zip(ns, sls, qs, ss):
            qf = q.astype(F32)
            qq = jnp.concatenate([qf * qd_f, qf * qd_b], axis=1).astype(BF16)
            o_scr[hh, sl, :] = (_dot((s * mask).astype(BF16), v_ref[0, sl, vcols])
                                + _dot(qq, st_scr[hh, n]))

    def normalise(i):
        for u in range(unroll):
            sl = rows(i * unroll + u)
            y = _rms(o_scr[hh, sl, :], gnw, RET_DV) * gb_ref[0, sl, vcols].astype(F32)
            out_ref[0, sl, vcols] = y.astype(BF16)

    def both(i, _):
        normalise(i - 1)
        products(i)
        return 0

    products(0)
    loop(1, trips, both, 0)
    normalise(trips - 1)


def _ret_call(lgf, lgb, rq, rk, rv, gb, gnw, states, heads):
    B, L, _ = rq.shape
    n_chunks = L // RET_CHUNK
    has_state = states is not None
    smem = pl.BlockSpec(memory_space=pltpu.SMEM)
    st_spec = pl.BlockSpec((1, heads, RET_DK, RET_DV), lambda b, h: (b, h, 0, 0))
    in_specs = [smem, smem,
                pl.BlockSpec((1, L, heads * RET_DK), lambda b, h: (b, 0, h)),
                pl.BlockSpec((1, L, heads * RET_DK), lambda b, h: (b, 0, h)),
                pl.BlockSpec((1, L, heads * RET_DV), lambda b, h: (b, 0, h)),
                pl.BlockSpec((1, L, heads * RET_DV), lambda b, h: (b, 0, h)),
                pl.BlockSpec((heads, 1, RET_DV), lambda b, h: (h, 0, 0))]
    args = [lgf, lgb, rq, rk, rv, gb, gnw]
    out_specs = [pl.BlockSpec((1, L, heads * RET_DV), lambda b, h: (b, 0, h))]
    out_shape = [jax.ShapeDtypeStruct((B, L, RET_W), BF16)]
    if has_state:
        in_specs += [st_spec, st_spec]
        args += list(states)
    else:
        out_specs += [st_spec, st_spec]
        out_shape += [jax.ShapeDtypeStruct((B, RET_HEADS, RET_DK, RET_DV), F32)] * 2
    return pl.pallas_call(
        functools.partial(_ret_kernel, heads=heads, n_chunks=n_chunks, has_state=has_state,
                          unroll=math.gcd(n_chunks, 8)),
        grid=(B, RET_HEADS // heads),
        in_specs=in_specs, out_specs=out_specs, out_shape=out_shape,
        scratch_shapes=[pltpu.VMEM((heads, n_chunks, 2 * RET_DK, RET_DV), F32),
                        pltpu.VMEM((heads, n_chunks, 2 * RET_DK, RET_DV), BF16),
                        pltpu.VMEM((heads, L, RET_DV), F32),
                        pltpu.VMEM((heads, n_chunks, 2 * RET_DK, RET_CHUNK), BF16)],
        compiler_params=pltpu.CompilerParams(dimension_semantics=("parallel", "parallel"),
                                             vmem_limit_bytes=VMEM_LIMIT),
        name="ret_state" if has_state else "ret",
    )(*args)


def _out_kernel(ma_ref, mb_ref, w_ref, x_ref, mod_ref, o_ref):
    acc = _dot(ma_ref[0], w_ref[:MLA_W, :]) + _dot(mb_ref[0], w_ref[MLA_W:, :])
    o_ref[0] = x_ref[0] + mod_ref[0, 2:3, :] * acc


def _out_call(mix_a, mix_b, w_out, x, mod3, ctx_mod, tm):
    B, L, _ = x.shape
    mod_map = (lambda b, i: (MOD_ROWS - 1, 0, 0)) if ctx_mod else (lambda b, i: (b, 0, 0))
    return pl.pallas_call(
        _out_kernel,
        grid=(B, L // tm),
        in_specs=[pl.BlockSpec((1, tm, MLA_W), lambda b, i: (b, i, 0)),
                  pl.BlockSpec((1, tm, RET_W), lambda b, i: (b, i, 0)),
                  _const_spec((D_MODEL, D_MODEL)),
                  pl.BlockSpec((1, tm, D_MODEL), lambda b, i: (b, i, 0)),
                  pl.BlockSpec((1, 3, D_MODEL), mod_map)],
        out_specs=pl.BlockSpec((1, tm, D_MODEL), lambda b, i: (b, i, 0)),
        out_shape=jax.ShapeDtypeStruct((B, L, D_MODEL), F32),
        compiler_params=pltpu.CompilerParams(dimension_semantics=("parallel", "parallel"),
                                             vmem_limit_bytes=VMEM_LIMIT),
        name="out_proj",
    )(mix_a, mix_b, w_out, x, mod3)


def _rope_angles(pos, dim):
    half = dim // 2
    freqs = ROPE_BASE ** (-jnp.arange(half, dtype=F32) / half)
    return pos.astype(F32)[:, None] * freqs[None, :]


def _rope_tables(L):
    rows = L // GRID_W
    ar = _rope_angles(jnp.arange(rows), ROPE_DIM // 2)
    ac = _rope_angles(jnp.arange(GRID_W), ROPE_DIM // 2)
    cr, sr, cc, sc = jnp.cos(ar), jnp.sin(ar), jnp.cos(ac), jnp.sin(ac)
    zr, zc = jnp.zeros_like(sr), jnp.zeros_like(sc)

    def grid_table(per_row, per_col):
        w = per_row.shape[-1]
        pad = jnp.zeros((rows, GRID_W, LANE - 2 * w), F32)
        t = jnp.concatenate([jnp.broadcast_to(per_row[:, None, :], (rows, GRID_W, w)),
                             jnp.broadcast_to(per_col[None, :, :], (rows, GRID_W, w)), pad],
                            axis=-1)
        return t.reshape(L, LANE)

    cat = lambda a, b: jnp.concatenate([a, b], axis=-1)
    mc = grid_table(cat(cr, cr), cat(cc, cc))
    msa = grid_table(cat(-sr, zr), cat(-sc, zc))
    msb = grid_table(cat(zr, sr), cat(zc, sc))

    hi = _rope_angles(jnp.arange(rows) * GRID_W, RET_DK)[:, None, :]
    lo = _rope_angles(jnp.arange(GRID_W), RET_DK)[None, :, :]
    cos_t = jnp.cos(hi) * jnp.cos(lo) - jnp.sin(hi) * jnp.sin(lo)
    sin_t = jnp.sin(hi) * jnp.cos(lo) + jnp.cos(hi) * jnp.sin(lo)
    rc = cat(cos_t, cos_t).reshape(L, LANE)
    rs = cat(-sin_t, sin_t).reshape(L, LANE)
    return mc, msa, msb, rc, rs


def _prep_weights(l, norm_w, w_in, q_norm_w, w_uq, kv_norm_w, w_uk, w_uv, qk_q_w, qk_k_w):
    wi = w_in[l]
    w_lat = jnp.pad(wi[:, :LAT_SRC_COLS], ((0, 0), (0, LAT_COLS - LAT_SRC_COLS))).astype(BF16)
    w_wide = wi[:, LAT_SRC_COLS:].astype(BF16)
    wq = w_uq[l].reshape(Q_LORA, MLA_HEADS, QK_HEAD)
    wq = jnp.pad(wq, ((0, 0), (0, 0), (0, QK_PAD - QK_HEAD))).reshape(Q_LORA, MLA_HEADS * QK_PAD)
    qscale = (QK_HEAD ** -0.5) * math.log2(math.e)
    qkq = jnp.pad(qk_q_w[l] * qscale, (0, QK_PAD - QK_HEAD))[None, :]
    kk = qk_k_w[l]
    return {
        "norm_w": norm_w[l][None, :],
        "w_lat": w_lat,
        "w_wide": w_wide,
        "q_norm_w": q_norm_w[l][None, :],
        "w_uq": wq.astype(BF16),
        "qk_q_w": qkq,
        "kv_norm_w": kv_norm_w[l][None, :],
        "w_uk": w_uk[l].astype(BF16),
        "w_uvt": w_uv[l].T.astype(BF16),
        "qk_k_n": kk[None, :MLA_NOPE],
        "qk_k_r": jnp.pad(kk[MLA_NOPE:], (0, LANE - ROPE_DIM))[None, :],
    }


def _tiles(L):
    tm = min(L, 512)
    tm_out = min(L, 512)
    tq = min(L, 4096)
    tk = min(L, 512)
    heads = MLA_HEADS if L <= 512 else 1
    ret_heads = RET_HEADS if L <= 512 else 1
    return tm, tm_out, tq, tk, heads, ret_heads


def _layer(x, mod3, ctx_mod, wts, w_out, lgf, lgb, gnw, rope_tabs, ctx, states):
    tm, tm_out, tq, tk, heads, ret_heads = _tiles(x.shape[1])
    outs = _in_call(x, mod3, ctx_mod, wts, rope_tabs, tm)
    q, k, vt, ga, rq, rk, rv, gb = outs[:8]
    mix_a = _attn_call(q, k, vt, ga, ctx, tq, tk, heads)
    ret = _ret_call(lgf, lgb, rq, rk, rv, gb, gnw, states, ret_heads)
    y = _out_call(mix_a, ret[0], w_out, x, mod3, ctx_mod, tm_out)
    return y, outs[8:], ret[1:]


def kernel(x_prompt, x_sample, c, cache_mla_ckv, cache_mla_krope, state_ret_fwd, state_ret_bwd,
           c_ctx, norm_w, w_mod, b_mod, w_in, mla_q_norm_w, mla_w_uq, mla_kv_norm_w, mla_w_uk,
           mla_w_uv, mla_qk_q_w, mla_qk_k_w, ret_log_decay_fwd, ret_log_decay_bwd, ret_gn_w, w_out):
    depth = w_in.shape[0]
    dec_b = x_sample.shape[0]
    assert dec_b < MOD_ROWS
    cvec = jnp.zeros((MOD_ROWS, D_MODEL), F32).at[:dec_b].set(c).at[MOD_ROWS - 1].set(c_ctx)
    rope_tabs = _rope_tables(x_sample.shape[1])

    xp, xs = x_prompt, x_sample
    ckv_l, kr_l, sf_l, sb_l = [], [], [], []
    for l in range(depth):
        wts = _prep_weights(l, norm_w, w_in, mla_q_norm_w, mla_w_uq, mla_kv_norm_w, mla_w_uk,
                            mla_w_uv, mla_qk_q_w, mla_qk_k_w)
        w_out_b = w_out[l].astype(BF16)
        lgf = -jnp.exp(ret_log_decay_fwd[l].astype(F32))
        lgb = -jnp.exp(ret_log_decay_bwd[l].astype(F32))
        gnw = ret_gn_w[l][:, None, :]
        mod3 = _mod_call(cvec, w_mod[l], b_mod[l][None, :]).reshape(MOD_ROWS, 3, D_MODEL)

        xp, (ckv, kr), (sf, sb) = _layer(xp, mod3, True, wts, w_out_b, lgf, lgb, gnw,
                                         None, None, None)
        ckv_l.append(ckv)
        kr_l.append(kr)
        sf_l.append(sf)
        sb_l.append(sb)

        kr_pad = jnp.pad(cache_mla_krope[:, l], ((0, 0), (0, 0), (0, LANE - ROPE_DIM)))
        ctx = _ctx_call(cache_mla_ckv[:, l], kr_pad, wts)
        xs, _, _ = _layer(xs, mod3, False, wts, w_out_b, lgf, lgb, gnw, rope_tabs, ctx,
                          (state_ret_fwd[:, l], state_ret_bwd[:, l]))

    return (xp, xs, jnp.stack(ckv_l, axis=1), jnp.stack(kr_l, axis=1),
            jnp.stack(sf_l, axis=1), jnp.stack(sb_l, axis=1))
```
